```python
import math
import jax, jax.numpy as jnp
from jax import lax
import numpy as np

D_MODEL = 1024
BATCH = 4
SEQ = 8192
DEPTH = 1

HEAD_DIM = 128
HEADS_PER_GROUP = 4
ATTN_PATTERNS = ((128, 1), (512, 4), (2048, 16))
N_ATTN_HEADS = HEADS_PER_GROUP * len(ATTN_PATTERNS)
ATTN_QKV_WIDTH = N_ATTN_HEADS * HEAD_DIM
ATTN_OUT_WIDTH = HEADS_PER_GROUP * HEAD_DIM
FOURIER_GROUPS = 4
FOURIER_GROUP_DIM = 128
FOURIER_WIDTH = FOURIER_GROUPS * FOURIER_GROUP_DIM
N_BRANCHES = 2
IN_WIDTH = 3 * ATTN_QKV_WIDTH + FOURIER_WIDTH + N_BRANCHES * D_MODEL
D_FF = 2752
NUM_BUCKETS = 32
MAX_EXACT = 8
MAX_DISTANCE = 1024
NEG_INF = -1e30
LN_EPS = 1e-5
ALPHA = (2 * DEPTH) ** 0.25
BETA = (8 * DEPTH) ** -0.25

kernel_name = "hybrid_dilated_attn_fnet_macaron_encoder"


def layer_norm(x, g, b):
    xf = x.astype(jnp.float32)
    mu = jnp.mean(xf, axis=-1, keepdims=True)
    var = jnp.mean(jnp.square(xf - mu), axis=-1, keepdims=True)
    y = (xf - mu) * lax.rsqrt(var + LN_EPS) * g.astype(jnp.float32) + b.astype(jnp.float32)
    return y.astype(x.dtype)


def swiglu(x, w_gate, w_up, w_down):
    return (jax.nn.silu(x @ w_gate) * (x @ w_up)) @ w_down


def t5_bucket(rel):
    half = NUM_BUCKETS // 2
    ret = (rel > 0).astype(jnp.int32) * half
    n = jnp.abs(rel)
    nf = jnp.maximum(n, 1).astype(jnp.float32)
    large = MAX_EXACT + (jnp.log(nf / MAX_EXACT) / math.log(MAX_DISTANCE / MAX_EXACT)
                         * (half - MAX_EXACT)).astype(jnp.int32)
    large = jnp.minimum(large, half - 1)
    return ret + jnp.where(n < MAX_EXACT, n, large)


def dilated_window_attention(q, k, v, rel_bias_g, window, dilation):
    B, S, H, E = q.shape
    half = window // (2 * dilation)
    blk = half
    L = S // dilation
    nb = -(-L // blk)
    Lp = nb * blk
    scale = E ** -0.5

    def to_sub(t):
        return t.reshape(B, L, dilation, H, E).transpose(0, 2, 3, 1, 4)

    qs, ks, vs = to_sub(q), to_sub(k), to_sub(v)
    qb = jnp.pad(qs, ((0, 0),) * 3 + ((0, Lp - L), (0, 0))).reshape(B, dilation, H, nb, blk, E)

    def windows(t):
        tp = jnp.pad(t, ((0, 0),) * 3 + ((blk, Lp - L + blk), (0, 0)))
        tp = tp.reshape(B, dilation, H, nb + 2, blk, E)
        return jnp.concatenate([tp[:, :, :, :-2], tp[:, :, :, 1:-1], tp[:, :, :, 2:]], axis=4)

    kw, vw = windows(ks), windows(vs)

    a_idx = jnp.arange(blk, dtype=jnp.int32)[:, None]
    b_idx = jnp.arange(3 * blk, dtype=jnp.int32)[None, :]
    off = b_idx - blk - a_idx
    band = jnp.abs(off) <= half
    key_pos = jnp.arange(nb, dtype=jnp.int32)[:, None] * blk - blk + jnp.arange(3 * blk, dtype=jnp.int32)[None, :]
    valid = (key_pos >= 0) & (key_pos < L)
    mask = band[None] & valid[:, None, :]
    bias = rel_bias_g[t5_bucket(off * dilation)]
    bias = jnp.moveaxis(bias, -1, 0).astype(jnp.float32)
    logit_add = jnp.where(mask[None], bias[:, None], NEG_INF)

    s = jnp.einsum('bdhnqe,bdhnke->bdhnqk', qb, kw).astype(jnp.float32) * scale + logit_add
    mx = jnp.max(s, axis=-1, keepdims=True)
    p = jnp.exp(s - mx)
    den = jnp.sum(p, axis=-1)
    o = jnp.einsum('bdhnqk,bdhnke->bdhnqe', p, vw.astype(jnp.float32)) / den[..., None]
    lse = mx[..., 0] + jnp.log(den)

    o = o.reshape(B, dilation, H, Lp, E)[:, :, :, :L].transpose(0, 3, 1, 2, 4).reshape(B, S, H, E)
    lse = lse.reshape(B, dilation, H, Lp)[:, :, :, :L].transpose(0, 3, 1, 2).reshape(B, S, H)
    return o, lse


def hybrid_mixer(x, w_in, b_in, rel_bias, w_proj_attn, w_proj_fourier, w_out):
    B, S, _ = x.shape
    h = x @ w_in + b_in
    A = ATTN_QKV_WIDTH
    q = h[..., 0:A].reshape(B, S, N_ATTN_HEADS, HEAD_DIM)
    k = h[..., A:2 * A].reshape(B, S, N_ATTN_HEADS, HEAD_DIM)
    v = h[..., 2 * A:3 * A].reshape(B, S, N_ATTN_HEADS, HEAD_DIM)
    u = h[..., 3 * A:3 * A + FOURIER_WIDTH]
    gates = jax.nn.sigmoid(h[..., 3 * A + FOURIER_WIDTH:])

    outs, lses = [], []
    for g, (window, dilation) in enumerate(ATTN_PATTERNS):
        sl = slice(g * HEADS_PER_GROUP, (g + 1) * HEADS_PER_GROUP)
        o, l = dilated_window_attention(q[:, :, sl], k[:, :, sl], v[:, :, sl],
                                        rel_bias[:, sl], window, dilation)
        outs.append(o)
        lses.append(l)
    w_mix = jax.nn.softmax(jnp.stack(lses, axis=0), axis=0)
    attn = jnp.sum(w_mix[..., None] * jnp.stack(outs, axis=0), axis=0)
    attn = attn.reshape(B, S, ATTN_OUT_WIDTH).astype(x.dtype)

    uf = u.astype(jnp.float32).reshape(B, S, FOURIER_GROUPS, FOURIER_GROUP_DIM)
    four = jnp.real(jnp.fft.fft2(uf, axes=(1, 3), norm='ortho'))
    four = four.reshape(B, S, FOURIER_WIDTH).astype(x.dtype)

    g_attn = gates[..., :D_MODEL]
    g_four = gates[..., D_MODEL:]
    merged = g_attn * (attn @ w_proj_attn) + g_four * (four @ w_proj_fourier)
    return merged @ w_out


def setup_inputs(seed: int = 0) -> dict:
    key = jax.random.key(seed)
    ks = jax.random.split(key, 20)

    def nrm(k, shape, scale):
        return jax.random.normal(k, shape, jnp.float32) * scale

    D, F = D_MODEL, D_FF
    return {
        "x": nrm(ks[0], (BATCH, SEQ, D), 1.0),
        "ln1_g": 1.0 + nrm(ks[1], (DEPTH, D), 0.02),
        "ln1_b": nrm(ks[2], (DEPTH, D), 0.02),
        "ffn1_w_gate": nrm(ks[3], (DEPTH, D, F), D ** -0.5),
        "ffn1_w_up": nrm(ks[4], (DEPTH, D, F), D ** -0.5),
        "ffn1_w_down": nrm(ks[5], (DEPTH, F, D), F ** -0.5 * BETA),
        "w_in": nrm(ks[6], (DEPTH, D, IN_WIDTH), D ** -0.5),
        "b_in": nrm(ks[7], (DEPTH, IN_WIDTH), 0.02),
        "rel_bias": nrm(ks[8], (NUM_BUCKETS, N_ATTN_HEADS), 0.5),
        "w_proj_attn": nrm(ks[9], (DEPTH, ATTN_OUT_WIDTH, D), ATTN_OUT_WIDTH ** -0.5),
        "w_proj_fourier": nrm(ks[10], (DEPTH, FOURIER_WIDTH, D), FOURIER_WIDTH ** -0.5),
        "w_out": nrm(ks[11], (DEPTH, D, D), D ** -0.5 * BETA),
        "ln2_g": 1.0 + nrm(ks[12], (DEPTH, D), 0.02),
        "ln2_b": nrm(ks[13], (DEPTH, D), 0.02),
        "ffn2_w_gate": nrm(ks[14], (DEPTH, D, F), D ** -0.5),
        "ffn2_w_up": nrm(ks[15], (DEPTH, D, F), D ** -0.5),
        "ffn2_w_down": nrm(ks[16], (DEPTH, F, D), F ** -0.5 * BETA),
        "ln3_g": 1.0 + nrm(ks[17], (DEPTH, D), 0.02),
        "ln3_b": nrm(ks[18], (DEPTH, D), 0.02),
    }


def reference(x, ln1_g, ln1_b, ffn1_w_gate, ffn1_w_up, ffn1_w_down, w_in, b_in, rel_bias,
              w_proj_attn, w_proj_fourier, w_out, ln2_g, ln2_b, ffn2_w_gate, ffn2_w_up,
              ffn2_w_down, ln3_g, ln3_b):
    h = x
    for l in range(DEPTH):
        h = layer_norm(ALPHA * h + 0.5 * swiglu(h, ffn1_w_gate[l], ffn1_w_up[l], ffn1_w_down[l]),
                       ln1_g[l], ln1_b[l])
        h = layer_norm(ALPHA * h + hybrid_mixer(h, w_in[l], b_in[l], rel_bias, w_proj_attn[l],
                                                w_proj_fourier[l], w_out[l]),
                       ln2_g[l], ln2_b[l])
        h = layer_norm(ALPHA * h + 0.5 * swiglu(h, ffn2_w_gate[l], ffn2_w_up[l], ffn2_w_down[l]),
                       ln3_g[l], ln3_b[l])
    return h
```

```python
import functools
import math

import numpy as np
import jax
import jax.numpy as jnp
from jax import lax
from jax.experimental import pallas as pl
from jax.experimental.pallas import tpu as pltpu

F32 = jnp.float32
BF16 = jnp.bfloat16

HEAD_DIM = 128
HEADS_PER_GROUP = 4
ATTN_PATTERNS = ((128, 1), (512, 4), (2048, 16))
N_GROUPS = len(ATTN_PATTERNS)
GROUP_WIDTH = HEADS_PER_GROUP * HEAD_DIM
ATTN_QKV_WIDTH = N_GROUPS * GROUP_WIDTH
FOURIER_GROUPS = 4
FOURIER_GROUP_DIM = 128
FOURIER_WIDTH = FOURIER_GROUPS * FOURIER_GROUP_DIM
NUM_BUCKETS = 32
MAX_EXACT = 8
MAX_DISTANCE = 1024
NEG_INF = -1e30
LN_EPS = 1e-5

LANES = 128
MXU_DIM = 256
V7X_VMEM_BYTES = 64 * 1024 * 1024
VMEM_LIMIT_BYTES = V7X_VMEM_BYTES * 7 // 8

HALF_WIN = 64
Q_TILE = 128
K_TILE = Q_TILE + 2 * HALF_WIN

FFT_N1 = 128
FFT_N2 = 64
FFT_IN_PITCH = FFT_N2 + 8
FFT_MID_PITCH = FFT_N1 + 8


def _compiler_params(semantics):
    return pltpu.CompilerParams(dimension_semantics=semantics,
                                vmem_limit_bytes=VMEM_LIMIT_BYTES)


def _resident(shape):
    zeros = (0,) * len(shape)
    return pl.BlockSpec(shape, lambda *_: zeros, pipeline_mode=pl.Buffered(1))


def _layer_norm(z, g, b):
    mu = jnp.mean(z, axis=-1, keepdims=True)
    zc = z - mu
    var = jnp.mean(zc * zc, axis=-1, keepdims=True)
    return zc * lax.rsqrt(var + LN_EPS) * g + b


def _ffn_ln_kernel(x_ref, wg_ref, wu_ref, wd_ref, g_ref, b_ref, o_ref, *, alpha, f_chunk):
    x = x_ref[...]
    xb = x.astype(BF16)
    y = jnp.zeros(x.shape, F32)
    for c in range(wg_ref.shape[1] // f_chunk):
        sl = slice(c * f_chunk, (c + 1) * f_chunk)
        gate = jnp.dot(xb, wg_ref[:, sl], preferred_element_type=F32)
        up = jnp.dot(xb, wu_ref[:, sl], preferred_element_type=F32)
        act = (gate * jax.nn.sigmoid(gate) * up).astype(BF16)
        y = y + jnp.dot(act, wd_ref[sl, :], preferred_element_type=F32)
    o_ref[...] = _layer_norm(alpha * x + 0.5 * y, g_ref[...], b_ref[...])


def _ffn_ln(x, w_gate, w_up, w_down, ln_g, ln_b, *, alpha, tm=512, f_chunk=MXU_DIM):
    n, d = x.shape
    f = w_gate.shape[1]
    f_pad = -(-f // f_chunk) * f_chunk
    wg = jnp.pad(w_gate, ((0, 0), (0, f_pad - f))).astype(BF16)
    wu = jnp.pad(w_up, ((0, 0), (0, f_pad - f))).astype(BF16)
    wd = jnp.pad(w_down, ((0, f_pad - f), (0, 0))).astype(BF16)
    row = pl.BlockSpec((tm, d), lambda i: (i, 0))
    return pl.pallas_call(
        functools.partial(_ffn_ln_kernel, alpha=alpha, f_chunk=f_chunk),
        out_shape=jax.ShapeDtypeStruct((n, d), F32),
        grid=(n // tm,),
        in_specs=[row, _resident((d, f_pad)), _resident((d, f_pad)), _resident((f_pad, d)),
                  _resident((1, d)), _resident((1, d))],
        out_specs=row,
        compiler_params=_compiler_params(("parallel",)),
        name="ffn_ln",
    )(x, wg, wu, wd, ln_g.reshape(1, d), ln_b.reshape(1, d))


def _in_proj_kernel(x_ref, w_ref, b_ref, qkv0_ref, qkv1_ref, qkv2_ref, u_ref, stage_ref, *, tm):
    xb = x_ref[...].astype(BF16)
    gw = 3 * GROUP_WIDTH

    def proj(c0, width):
        return (jnp.dot(xb, w_ref[:, c0:c0 + width], preferred_element_type=F32)
                + b_ref[:, c0:c0 + width])

    for j in range(3):
        qkv0_ref[0, 0, :, j * GROUP_WIDTH:(j + 1) * GROUP_WIDTH] = (
            proj(j * GROUP_WIDTH, GROUP_WIDTH).astype(BF16))
    u_ref[0] = proj(N_GROUPS * gw, FOURIER_WIDTH).astype(BF16)

    for g, out_ref in ((1, qkv1_ref), (2, qkv2_ref)):
        dil = ATTN_PATTERNS[g][1]
        for j in range(3):
            res = proj(g * gw + j * GROUP_WIDTH, GROUP_WIDTH)
            for jj in range(GROUP_WIDTH // LANES):
                stage_ref[g - 1, j * 4 + jj] = res[:, jj * LANES:(jj + 1) * LANES]
        for r in range(dil):
            for s in range(gw // LANES):
                out_ref[0, r, :, s * LANES:(s + 1) * LANES] = (
                    stage_ref[g - 1, s, pl.ds(r, tm // dil, stride=dil), :].astype(BF16))


def _in_proj(h, w, b, *, batch, seq, tm=512):
    n, d = h.shape
    width = w.shape[1]
    gw = 3 * GROUP_WIDTH
    tiles_per_seq = seq // tm
    d1, d2 = ATTN_PATTERNS[1][1], ATTN_PATTERNS[2][1]
    out_shape = (
        jax.ShapeDtypeStruct((batch, 1, seq, gw), BF16),
        jax.ShapeDtypeStruct((batch, d1, seq // d1, gw), BF16),
        jax.ShapeDtypeStruct((batch, d2, seq // d2, gw), BF16),
        jax.ShapeDtypeStruct((batch, seq, FOURIER_WIDTH), BF16),
    )

    def bt(i):
        return i // tiles_per_seq, i % tiles_per_seq

    out_specs = (
        pl.BlockSpec((1, 1, tm, gw), lambda i: (bt(i)[0], 0, bt(i)[1], 0)),
        pl.BlockSpec((1, d1, tm // d1, gw), lambda i: (bt(i)[0], 0, bt(i)[1], 0)),
        pl.BlockSpec((1, d2, tm // d2, gw), lambda i: (bt(i)[0], 0, bt(i)[1], 0)),
        pl.BlockSpec((1, tm, FOURIER_WIDTH), lambda i: (bt(i)[0], bt(i)[1], 0)),
    )
    return pl.pallas_call(
        functools.partial(_in_proj_kernel, tm=tm),
        out_shape=out_shape,
        grid=(n // tm,),
        in_specs=[pl.BlockSpec((tm, d), lambda i: (i, 0)), _resident((d, width)),
                  _resident((1, width))],
        out_specs=out_specs,
        scratch_shapes=[pltpu.VMEM((2, gw // LANES, tm, LANES), F32)],
        compiler_params=_compiler_params(("parallel",)),
        name="in_proj",
    )(h, w, b)


def _t5_bucket_np(rel):
    half = NUM_BUCKETS // 2
    ret = (rel > 0).astype(np.int64) * half
    n = np.abs(rel)
    nf = np.maximum(n, 1).astype(np.float64)
    large = MAX_EXACT + (np.log(nf / MAX_EXACT) / math.log(MAX_DISTANCE / MAX_EXACT)
                         * (half - MAX_EXACT)).astype(np.int64)
    large = np.minimum(large, half - 1)
    return ret + np.where(n < MAX_EXACT, n, large)


def _attention_bias(rel_bias):
    offs = np.arange(-HALF_WIN, HALF_WIN + 1)
    band = []
    for g, (_, dil) in enumerate(ATTN_PATTERNS):
        buckets = _t5_bucket_np(offs * dil)
        heads = rel_bias[:, g * HEADS_PER_GROUP:(g + 1) * HEADS_PER_GROUP]
        band.append(jnp.take(heads, jnp.asarray(buckets, jnp.int32), axis=0).T)
    band = jnp.stack(band).astype(F32)
    band = jnp.concatenate([band, jnp.full(band.shape[:2] + (1,), NEG_INF, F32)], axis=-1)
    a = np.arange(Q_TILE)[None, :, None]
    j = np.arange(K_TILE)[None, None, :]
    v = np.arange(3)[:, None, None]
    off = j - a - HALF_WIN * v
    idx = np.where(np.abs(off) <= HALF_WIN, off + HALF_WIN, 2 * HALF_WIN + 1)
    return jnp.take(band, jnp.asarray(idx, jnp.int32), axis=-1)


def _attn_kernel(q0, k0, v0, q1, k1, v1, q2, k2, v2, bias_ref, o_ref, os_ref, ls_ref, *, chunk):
    c = pl.program_id(2)
    scale = HEAD_DIM ** -0.5
    for g, (q_ref, k_ref, v_ref) in enumerate(((q0, k0, v0), (q1, k1, v1), (q2, k2, v2))):
        dil = ATTN_PATTERNS[g][1]
        sub_len = k_ref.shape[2]
        rows_per_sub = chunk // dil
        tiles_per_sub = rows_per_sub // Q_TILE

        def tile(i, carry, g=g, dil=dil, sub_len=sub_len, rows_per_sub=rows_per_sub,
                 tiles_per_sub=tiles_per_sub, q_ref=q_ref, k_ref=k_ref, v_ref=v_ref):
            r = i // tiles_per_sub
            t = i % tiles_per_sub
            q_start = c * rows_per_sub + t * Q_TILE
            k_start = jnp.clip(q_start - HALF_WIN, 0, sub_len - K_TILE)
            variant = (q_start - k_start) // HALF_WIN
            k_start = pl.multiple_of(k_start, HALF_WIN)
            q = q_ref[0, r, pl.ds(pl.multiple_of(t * Q_TILE, Q_TILE), Q_TILE), :]
            k = k_ref[0, r, pl.ds(k_start, K_TILE), :]
            v = v_ref[0, r, pl.ds(k_start, K_TILE), :]
            s = lax.dot_general(q, k, (((1,), (1,)), ((), ())), preferred_element_type=F32)
            s = s * scale + bias_ref[g, 0, variant]
            m = jnp.max(s, axis=-1, keepdims=True)
            p = jnp.exp(s - m)
            den = jnp.sum(p, axis=-1, keepdims=True)
            o = jnp.dot(p.astype(BF16), v, preferred_element_type=F32) * (1.0 / den)
            lse = jnp.broadcast_to(m + jnp.log(den), (Q_TILE, LANES))
            row0 = t * Q_TILE * dil + r
            rows = pl.ds(row0, Q_TILE) if dil == 1 else pl.ds(row0, Q_TILE, stride=dil)
            os_ref[g, rows, :] = o
            ls_ref[g, rows, :] = lse
            return carry

        lax.fori_loop(0, dil * tiles_per_sub, tile, 0)

    blk = 256
    for i in range(chunk // blk):
        rows = slice(i * blk, (i + 1) * blk)
        lses = [ls_ref[g, rows, :] for g in range(N_GROUPS)]
        mx = jnp.maximum(jnp.maximum(lses[0], lses[1]), lses[2])
        es = [jnp.exp(l - mx) for l in lses]
        num = es[0] * os_ref[0, rows, :] + es[1] * os_ref[1, rows, :] + es[2] * os_ref[2, rows, :]
        o_ref[0, rows, :] = (num / (es[0] + es[1] + es[2])).astype(BF16)


def _attention(qkv, bias, *, batch, seq, chunk=2048):
    in_specs = []
    for g, arr in enumerate(qkv):
        dil = ATTN_PATTERNS[g][1]
        sub_len = seq // dil
        in_specs += [
            pl.BlockSpec((1, dil, chunk // dil, HEAD_DIM), lambda b, h, c: (b, 0, c, h)),
            pl.BlockSpec((1, dil, sub_len, HEAD_DIM),
                         lambda b, h, c: (b, 0, 0, HEADS_PER_GROUP + h)),
            pl.BlockSpec((1, dil, sub_len, HEAD_DIM),
                         lambda b, h, c: (b, 0, 0, 2 * HEADS_PER_GROUP + h)),
        ]
    in_specs.append(pl.BlockSpec((N_GROUPS, 1, 3, Q_TILE, K_TILE),
                                 lambda b, h, c: (0, h, 0, 0, 0)))
    args = [a for arr in qkv for a in (arr, arr, arr)] + [bias]
    return pl.pallas_call(
        functools.partial(_attn_kernel, chunk=chunk),
        out_shape=jax.ShapeDtypeStruct((batch, seq, GROUP_WIDTH), BF16),
        grid=(batch, HEADS_PER_GROUP, seq // chunk),
        in_specs=in_specs,
        out_specs=pl.BlockSpec((1, chunk, HEAD_DIM), lambda b, h, c: (b, c, h)),
        scratch_shapes=[pltpu.VMEM((N_GROUPS, chunk, LANES), F32),
                        pltpu.VMEM((N_GROUPS, chunk, LANES), F32)],
        compiler_params=_compiler_params(("parallel", "parallel", "arbitrary")),
        name="attention",
    )(*args)


def _fft_tables(seq):
    assert seq == FFT_N1 * FFT_N2
    k1 = np.arange(FFT_N1)[None, :, None]
    n1 = np.arange(FFT_N1)[None, None, :]
    n2 = np.arange(FFT_N2)[:, None, None]
    ang = 2.0 * np.pi * ((n2 * k1 + FFT_N2 * n1 * k1) % seq) / seq
    t1 = np.concatenate([np.cos(ang), -np.sin(ang)], axis=1)
    k2 = np.arange(FFT_N2)[:, None]
    m2 = np.arange(FFT_N2)[None, :]
    ang2 = 2.0 * np.pi * ((k2 * m2) % FFT_N2) / FFT_N2
    cr, ci = np.cos(ang2), -np.sin(ang2)
    t2 = np.block([[cr, -ci], [ci, cr]])
    cc = np.arange(FOURIER_GROUP_DIM)
    ang3 = 2.0 * np.pi * ((cc[:, None] * cc[None, :]) % FOURIER_GROUP_DIM) / FOURIER_GROUP_DIM
    norm = 1.0 / math.sqrt(seq * FOURIER_GROUP_DIM)
    t3 = np.concatenate([np.cos(ang3), np.sin(ang3)], axis=0) * norm
    return (jnp.asarray(t1, BF16), jnp.asarray(t2, BF16), jnp.asarray(t3, BF16))


def _fft_kernel(u_ref, t1_ref, t2_ref, t3_ref, o_ref, uf_ref, yr_ref, yi_ref, gr_ref, gi_ref):
    def widen(n1, carry):
        src = pl.multiple_of(n1 * FFT_N2, FFT_N2)
        dst = pl.multiple_of(n1 * FFT_IN_PITCH, 8)
        uf_ref[pl.ds(dst, FFT_N2), :] = u_ref[0, pl.ds(src, FFT_N2), :].astype(F32)
        return carry
    lax.fori_loop(0, FFT_N1, widen, 0)

    def stage1(n2, carry):
        x = uf_ref[pl.ds(n2, FFT_N1, stride=FFT_IN_PITCH), :].astype(BF16)
        y = jnp.dot(t1_ref[n2], x, preferred_element_type=F32)
        dst = pl.multiple_of(n2 * FFT_MID_PITCH, 8)
        yr_ref[pl.ds(dst, FFT_N1), :] = y[:FFT_N1]
        yi_ref[pl.ds(dst, FFT_N1), :] = y[FFT_N1:]
        return carry
    lax.fori_loop(0, FFT_N2, stage1, 0)

    def stage2(k1, carry):
        rows = pl.ds(k1, FFT_N2, stride=FFT_MID_PITCH)
        y = jnp.concatenate([yr_ref[rows, :], yi_ref[rows, :]], axis=0).astype(BF16)
        gk = jnp.dot(t2_ref[...], y, preferred_element_type=F32)
        gr_ref[rows, :] = gk[:FFT_N2]
        gi_ref[rows, :] = gk[FFT_N2:]
        return carry
    lax.fori_loop(0, FFT_N1, stage2, 0)

    def stage3(k2, carry):
        src = pl.multiple_of(k2 * FFT_MID_PITCH, 8)
        gk = jnp.concatenate([gr_ref[pl.ds(src, FFT_N1), :], gi_ref[pl.ds(src, FFT_N1), :]],
                             axis=1).astype(BF16)
        out = jnp.dot(gk, t3_ref[...], preferred_element_type=F32)
        o_ref[0, pl.ds(pl.multiple_of(k2 * FFT_N1, FFT_N1), FFT_N1), :] = out.astype(BF16)
        return carry
    lax.fori_loop(0, FFT_N2, stage3, 0)


def _fft(u, *, batch, seq):
    t1, t2, t3 = _fft_tables(seq)
    blk = pl.BlockSpec((1, seq, FOURIER_GROUP_DIM), lambda b, g: (b, 0, g))
    return pl.pallas_call(
        _fft_kernel,
        out_shape=jax.ShapeDtypeStruct((batch, seq, FOURIER_WIDTH), BF16),
        grid=(batch, FOURIER_GROUPS),
        in_specs=[blk, _resident(t1.shape), _resident(t2.shape), _resident(t3.shape)],
        out_specs=blk,
        scratch_shapes=[pltpu.VMEM((FFT_N1 * FFT_IN_PITCH, LANES), F32)]
        + [pltpu.VMEM((FFT_N2 * FFT_MID_PITCH, LANES), F32)] * 4,
        compiler_params=_compiler_params(("parallel", "parallel")),
        name="fft",
    )(u, t1, t2, t3)


def _merge_ln_kernel(h_ref, a_ref, f_ref, wgate_ref, bgate_ref, wpa_ref, wpf_ref, wout_ref,
                     g_ref, b_ref, o_ref, *, alpha):
    h = h_ref[...]
    d = h.shape[1]
    hb = h.astype(BF16)
    pa = jnp.dot(a_ref[...], wpa_ref[...], preferred_element_type=F32)
    pf = jnp.dot(f_ref[...], wpf_ref[...], preferred_element_type=F32)
    ga = jax.nn.sigmoid(jnp.dot(hb, wgate_ref[:, :d], preferred_element_type=F32)
                        + bgate_ref[:, :d])
    gf = jax.nn.sigmoid(jnp.dot(hb, wgate_ref[:, d:], preferred_element_type=F32)
                        + bgate_ref[:, d:])
    merged = (ga * pa + gf * pf).astype(BF16)
    y = jnp.dot(merged, wout_ref[...], preferred_element_type=F32)
    o_ref[...] = _layer_norm(alpha * h + y, g_ref[...], b_ref[...])


def _merge_ln(h, attn, four, w_gate, b_gate, w_pa, w_pf, w_out, ln_g, ln_b, *, alpha, tm=512):
    n, d = h.shape
    row = pl.BlockSpec((tm, d), lambda i: (i, 0))
    half = pl.BlockSpec((tm, attn.shape[1]), lambda i: (i, 0))
    return pl.pallas_call(
        functools.partial(_merge_ln_kernel, alpha=alpha),
        out_shape=jax.ShapeDtypeStruct((n, d), F32),
        grid=(n // tm,),
        in_specs=[row, half, half, _resident(w_gate.shape), _resident(b_gate.shape),
                  _resident(w_pa.shape), _resident(w_pf.shape), _resident(w_out.shape),
                  _resident((1, d)), _resident((1, d))],
        out_specs=row,
        compiler_params=_compiler_params(("parallel",)),
        name="merge_ln",
    )(h, attn, four, w_gate, b_gate, w_pa, w_pf, w_out, ln_g.reshape(1, d), ln_b.reshape(1, d))


def _mixer_ln(h, w_in, b_in, rel_bias, w_proj_attn, w_proj_fourier, w_out, ln_g, ln_b,
              *, alpha, batch, seq):
    d = h.shape[1]
    a = ATTN_QKV_WIDTH
    gate0 = 3 * a + FOURIER_WIDTH
    starts = [part * a + g * GROUP_WIDTH for g in range(N_GROUPS) for part in range(3)]
    w_qkvu = jnp.concatenate([w_in[:, s:s + GROUP_WIDTH] for s in starts]
                             + [w_in[:, 3 * a:gate0]], axis=1).astype(BF16)
    b_qkvu = jnp.concatenate([b_in[s:s + GROUP_WIDTH] for s in starts]
                             + [b_in[3 * a:gate0]]).reshape(1, -1)
    w_gate = w_in[:, gate0:].astype(BF16)
    b_gate = b_in[gate0:].reshape(1, -1)

    qkv0, qkv1, qkv2, u = _in_proj(h, w_qkvu, b_qkvu, batch=batch, seq=seq)
    attn = _attention((qkv0, qkv1, qkv2), _attention_bias(rel_bias), batch=batch, seq=seq)
    four = _fft(u, batch=batch, seq=seq)
    return _merge_ln(h, attn.reshape(batch * seq, -1), four.reshape(batch * seq, -1),
                     w_gate, b_gate, w_proj_attn.astype(BF16), w_proj_fourier.astype(BF16),
                     w_out.astype(BF16), ln_g, ln_b, alpha=alpha)


def kernel(x, ln1_g, ln1_b, ffn1_w_gate, ffn1_w_up, ffn1_w_down, w_in, b_in, rel_bias,
           w_proj_attn, w_proj_fourier, w_out, ln2_g, ln2_b, ffn2_w_gate, ffn2_w_up,
           ffn2_w_down, ln3_g, ln3_b):
    batch, seq, d = x.shape
    depth = ln1_g.shape[0]
    alpha = (2 * depth) ** 0.25
    h = x.reshape(batch * seq, d)
    for l in range(depth):
        h = _ffn_ln(h, ffn1_w_gate[l], ffn1_w_up[l], ffn1_w_down[l], ln1_g[l], ln1_b[l],
                    alpha=alpha)
        h = _mixer_ln(h, w_in[l], b_in[l], rel_bias, w_proj_attn[l], w_proj_fourier[l],
                      w_out[l], ln2_g[l], ln2_b[l], alpha=alpha, batch=batch, seq=seq)
        h = _ffn_ln(h, ffn2_w_gate[l], ffn2_w_up[l], ffn2_w_down[l], ln3_g[l], ln3_b[l],
                    alpha=alpha)
    return h.reshape(batch, seq, d)
```

```python
import functools
import math

import numpy as np
import jax
import jax.numpy as jnp
from jax import lax
from jax.experimental import pallas as pl
from jax.experimental.pallas import tpu as pltpu

F32 = jnp.float32
BF16 = jnp.bfloat16

HEAD_DIM = 128
HEADS_PER_GROUP = 4
ATTN_PATTERNS = ((128, 1), (512, 4), (2048, 16))
N_GROUPS = len(ATTN_PATTERNS)
GROUP_WIDTH = HEADS_PER_GROUP * HEAD_DIM
ATTN_QKV_WIDTH = N_GROUPS * GROUP_WIDTH
FOURIER_GROUPS = 4
FOURIER_GROUP_DIM = 128
FOURIER_WIDTH = FOURIER_GROUPS * FOURIER_GROUP_DIM
NUM_BUCKETS = 32
MAX_EXACT = 8
MAX_DISTANCE = 1024
NEG_INF = -1e30
LN_EPS = 1e-5

LANES = 128
MXU_DIM = 256
V7X_VMEM_BYTES = 64 * 1024 * 1024
VMEM_LIMIT_BYTES = V7X_VMEM_BYTES * 7 // 8

HALF_WIN = 64
Q_TILE = 128
K_TILE = Q_TILE + 2 * HALF_WIN
ATTN_UNROLL = 16
FFT_UNROLL = 16

FFT_N1 = 128
FFT_N2 = 64
FFT_IN_PITCH = FFT_N2 + 8
FFT_MID_PITCH = FFT_N1 + 8


def _compiler_params(semantics):
    return pltpu.CompilerParams(dimension_semantics=semantics,
                                vmem_limit_bytes=VMEM_LIMIT_BYTES)


def _resident(shape):
    zeros = (0,) * len(shape)
    return pl.BlockSpec(shape, lambda *_: zeros, pipeline_mode=pl.Buffered(1))


def _layer_norm(z, g, b):
    mu = jnp.mean(z, axis=-1, keepdims=True)
    zc = z - mu
    var = jnp.mean(zc * zc, axis=-1, keepdims=True)
    return zc * lax.rsqrt(var + LN_EPS) * g + b


def _ffn_ln_kernel(x_ref, wg_ref, wu_ref, wd_ref, g_ref, b_ref, o_ref, *, alpha, f_chunk):
    x = x_ref[...]
    xb = x.astype(BF16)
    y = jnp.zeros(x.shape, F32)
    for c in range(wg_ref.shape[1] // f_chunk):
        sl = slice(c * f_chunk, (c + 1) * f_chunk)
        gate = jnp.dot(xb, wg_ref[:, sl], preferred_element_type=F32)
        up = jnp.dot(xb, wu_ref[:, sl], preferred_element_type=F32)
        act = (gate * jax.nn.sigmoid(gate) * up).astype(BF16)
        y = y + jnp.dot(act, wd_ref[sl, :], preferred_element_type=F32)
    o_ref[...] = _layer_norm(alpha * x + 0.5 * y, g_ref[...], b_ref[...])


def _ffn_ln(x, w_gate, w_up, w_down, ln_g, ln_b, *, alpha, tm=512, f_chunk=MXU_DIM):
    n, d = x.shape
    f = w_gate.shape[1]
    f_pad = -(-f // f_chunk) * f_chunk
    wg = jnp.pad(w_gate, ((0, 0), (0, f_pad - f))).astype(BF16)
    wu = jnp.pad(w_up, ((0, 0), (0, f_pad - f))).astype(BF16)
    wd = jnp.pad(w_down, ((0, f_pad - f), (0, 0))).astype(BF16)
    row = pl.BlockSpec((tm, d), lambda i: (i, 0))
    return pl.pallas_call(
        functools.partial(_ffn_ln_kernel, alpha=alpha, f_chunk=f_chunk),
        out_shape=jax.ShapeDtypeStruct((n, d), F32),
        grid=(n // tm,),
        in_specs=[row, _resident((d, f_pad)), _resident((d, f_pad)), _resident((f_pad, d)),
                  _resident((1, d)), _resident((1, d))],
        out_specs=row,
        compiler_params=_compiler_params(("parallel",)),
        name="ffn_ln",
    )(x, wg, wu, wd, ln_g.reshape(1, d), ln_b.reshape(1, d))


def _in_proj_kernel(x_ref, w_ref, b_ref, qkv0_ref, qkv1_ref, qkv2_ref, u_ref, stage_ref, *, tm):
    xb = x_ref[...].astype(BF16)
    gw = 3 * GROUP_WIDTH

    def proj(c0, width):
        return (jnp.dot(xb, w_ref[:, c0:c0 + width], preferred_element_type=F32)
                + b_ref[:, c0:c0 + width])

    for j in range(3):
        qkv0_ref[0, 0, :, j * GROUP_WIDTH:(j + 1) * GROUP_WIDTH] = (
            proj(j * GROUP_WIDTH, GROUP_WIDTH).astype(BF16))
    u_ref[0] = proj(N_GROUPS * gw, FOURIER_WIDTH).astype(BF16)

    for g, out_ref in ((1, qkv1_ref), (2, qkv2_ref)):
        dil = ATTN_PATTERNS[g][1]
        for j in range(3):
            res = proj(g * gw + j * GROUP_WIDTH, GROUP_WIDTH)
            for jj in range(GROUP_WIDTH // LANES):
                stage_ref[g - 1, j * 4 + jj] = res[:, jj * LANES:(jj + 1) * LANES]
        for r in range(dil):
            for s in range(gw // LANES):
                out_ref[0, r, :, s * LANES:(s + 1) * LANES] = (
                    stage_ref[g - 1, s, pl.ds(r, tm // dil, stride=dil), :].astype(BF16))


def _in_proj(h, w, b, *, batch, seq, tm=512):
    n, d = h.shape
    width = w.shape[1]
    gw = 3 * GROUP_WIDTH
    tiles_per_seq = seq // tm
    d1, d2 = ATTN_PATTERNS[1][1], ATTN_PATTERNS[2][1]
    out_shape = (
        jax.ShapeDtypeStruct((batch, 1, seq, gw), BF16),
        jax.ShapeDtypeStruct((batch, d1, seq // d1, gw), BF16),
        jax.ShapeDtypeStruct((batch, d2, seq // d2, gw), BF16),
        jax.ShapeDtypeStruct((batch, seq, FOURIER_WIDTH), BF16),
    )

    def bt(i):
        return i // tiles_per_seq, i % tiles_per_seq

    out_specs = (
        pl.BlockSpec((1, 1, tm, gw), lambda i: (bt(i)[0], 0, bt(i)[1], 0)),
        pl.BlockSpec((1, d1, tm // d1, gw), lambda i: (bt(i)[0], 0, bt(i)[1], 0)),
        pl.BlockSpec((1, d2, tm // d2, gw), lambda i: (bt(i)[0], 0, bt(i)[1], 0)),
        pl.BlockSpec((1, tm, FOURIER_WIDTH), lambda i: (bt(i)[0], bt(i)[1], 0)),
    )
    return pl.pallas_call(
        functools.partial(_in_proj_kernel, tm=tm),
        out_shape=out_shape,
        grid=(n // tm,),
        in_specs=[pl.BlockSpec((tm, d), lambda i: (i, 0)), _resident((d, width)),
                  _resident((1, width))],
        out_specs=out_specs,
        scratch_shapes=[pltpu.VMEM((2, gw // LANES, tm, LANES), F32)],
        compiler_params=_compiler_params(("parallel",)),
        name="in_proj",
    )(h, w, b)


def _t5_bucket_np(rel):
    half = NUM_BUCKETS // 2
    ret = (rel > 0).astype(np.int64) * half
    n = np.abs(rel)
    nf = np.maximum(n, 1).astype(np.float64)
    large = MAX_EXACT + (np.log(nf / MAX_EXACT) / math.log(MAX_DISTANCE / MAX_EXACT)
                         * (half - MAX_EXACT)).astype(np.int64)
    large = np.minimum(large, half - 1)
    return ret + np.where(n < MAX_EXACT, n, large)


def _attention_bias(rel_bias):
    offs = np.arange(-HALF_WIN, HALF_WIN + 1)
    band = []
    for g, (_, dil) in enumerate(ATTN_PATTERNS):
        buckets = _t5_bucket_np(offs * dil)
        heads = rel_bias[:, g * HEADS_PER_GROUP:(g + 1) * HEADS_PER_GROUP]
        band.append(jnp.take(heads, jnp.asarray(buckets, jnp.int32), axis=0).T)
    band = jnp.stack(band).astype(F32)
    period = Q_TILE + K_TILE
    base = jnp.concatenate(
        [band, jnp.full(band.shape[:2] + (period - band.shape[-1],), NEG_INF, F32)], axis=-1)
    variants = []
    for v in range(3):
        first_row = jnp.roll(base, HALF_WIN * (v - 1), axis=-1)
        seq = jnp.tile(first_row, (1, 1, Q_TILE))[..., :Q_TILE * (period - 1)]
        variants.append(seq.reshape(band.shape[:2] + (Q_TILE, period - 1))[..., :K_TILE])
    return jnp.stack(variants, axis=2)


def _attn_kernel(q0, k0, v0, q1, k1, v1, q2, k2, v2, bias_ref, o_ref, os_ref, ls_ref, *, chunk):
    c = pl.program_id(2)
    for g, (q_ref, k_ref, v_ref) in enumerate(((q0, k0, v0), (q1, k1, v1), (q2, k2, v2))):
        dil = ATTN_PATTERNS[g][1]
        sub_len = k_ref.shape[2]
        rows_per_sub = chunk // dil
        tiles_per_sub = rows_per_sub // Q_TILE

        def tile(i, carry, g=g, dil=dil, sub_len=sub_len, rows_per_sub=rows_per_sub,
                 tiles_per_sub=tiles_per_sub, q_ref=q_ref, k_ref=k_ref, v_ref=v_ref):
            r = i // tiles_per_sub
            t = i % tiles_per_sub
            q_start = c * rows_per_sub + t * Q_TILE
            k_start = jnp.clip(q_start - HALF_WIN, 0, sub_len - K_TILE)
            variant = (q_start - k_start) // HALF_WIN
            k_start = pl.multiple_of(k_start, HALF_WIN)
            q = q_ref[0, r, pl.ds(pl.multiple_of(t * Q_TILE, Q_TILE), Q_TILE), :]
            k = k_ref[0, r, pl.ds(k_start, K_TILE), :]
            v = v_ref[0, r, pl.ds(k_start, K_TILE), :]
            s = lax.dot_general(q, k, (((1,), (1,)), ((), ())), preferred_element_type=F32)
            s = s + bias_ref[g, 0, variant]
            m = jnp.max(s, axis=-1, keepdims=True)
            p = jnp.exp(s - m)
            den = jnp.sum(p, axis=-1, keepdims=True)
            o = jnp.dot(p.astype(BF16), v, preferred_element_type=F32) * (1.0 / den)
            lse = jnp.broadcast_to(m + jnp.log(den), (Q_TILE, LANES))
            row0 = t * Q_TILE * dil + r
            rows = pl.ds(row0, Q_TILE) if dil == 1 else pl.ds(row0, Q_TILE, stride=dil)
            os_ref[g, rows, :] = o
            ls_ref[g, rows, :] = lse
            return carry

        lax.fori_loop(0, dil * tiles_per_sub, tile, 0, unroll=ATTN_UNROLL)

    blk = 256
    for i in range(chunk // blk):
        rows = slice(i * blk, (i + 1) * blk)
        lses = [ls_ref[g, rows, :] for g in range(N_GROUPS)]
        mx = jnp.maximum(jnp.maximum(lses[0], lses[1]), lses[2])
        es = [jnp.exp(l - mx) for l in lses]
        num = es[0] * os_ref[0, rows, :] + es[1] * os_ref[1, rows, :] + es[2] * os_ref[2, rows, :]
        o_ref[0, rows, :] = (num / (es[0] + es[1] + es[2])).astype(BF16)


def _attention(qkv, bias, *, batch, seq, chunk=2048):
    in_specs = []
    for g, arr in enumerate(qkv):
        dil = ATTN_PATTERNS[g][1]
        sub_len = seq // dil
        in_specs += [
            pl.BlockSpec((1, dil, chunk // dil, HEAD_DIM), lambda b, h, c: (b, 0, c, h)),
            pl.BlockSpec((1, dil, sub_len, HEAD_DIM),
                         lambda b, h, c: (b, 0, 0, HEADS_PER_GROUP + h)),
            pl.BlockSpec((1, dil, sub_len, HEAD_DIM),
                         lambda b, h, c: (b, 0, 0, 2 * HEADS_PER_GROUP + h)),
        ]
    in_specs.append(pl.BlockSpec((N_GROUPS, 1, 3, Q_TILE, K_TILE),
                                 lambda b, h, c: (0, h, 0, 0, 0)))
    args = [a for arr in qkv for a in (arr, arr, arr)] + [bias]
    return pl.pallas_call(
        functools.partial(_attn_kernel, chunk=chunk),
        out_shape=jax.ShapeDtypeStruct((batch, seq, GROUP_WIDTH), BF16),
        grid=(batch, HEADS_PER_GROUP, seq // chunk),
        in_specs=in_specs,
        out_specs=pl.BlockSpec((1, chunk, HEAD_DIM), lambda b, h, c: (b, c, h)),
        scratch_shapes=[pltpu.VMEM((N_GROUPS, chunk, LANES), F32),
                        pltpu.VMEM((N_GROUPS, chunk, LANES), F32)],
        compiler_params=_compiler_params(("parallel", "parallel", "arbitrary")),
        name="attention",
    )(*args)


def _fft_tables(seq):
    assert seq == FFT_N1 * FFT_N2
    k1 = np.arange(FFT_N1)[None, :, None]
    n1 = np.arange(FFT_N1)[None, None, :]
    n2 = np.arange(FFT_N2)[:, None, None]
    ang = 2.0 * np.pi * ((n2 * k1 + FFT_N2 * n1 * k1) % seq) / seq
    t1 = np.concatenate([np.cos(ang), -np.sin(ang)], axis=1)
    k2 = np.arange(FFT_N2)[:, None]
    m2 = np.arange(FFT_N2)[None, :]
    ang2 = 2.0 * np.pi * ((k2 * m2) % FFT_N2) / FFT_N2
    cr, ci = np.cos(ang2), -np.sin(ang2)
    t2 = np.block([[cr, -ci], [ci, cr]])
    cc = np.arange(FOURIER_GROUP_DIM)
    ang3 = 2.0 * np.pi * ((cc[:, None] * cc[None, :]) % FOURIER_GROUP_DIM) / FOURIER_GROUP_DIM
    norm = 1.0 / math.sqrt(seq * FOURIER_GROUP_DIM)
    t3 = np.concatenate([np.cos(ang3), np.sin(ang3)], axis=0) * norm
    return (jnp.asarray(t1, BF16), jnp.asarray(t2, BF16), jnp.asarray(t3, BF16))


def _fft_kernel(u_ref, t1_ref, t2_ref, t3_ref, o_ref, uf_ref, yr_ref, yi_ref, gr_ref, gi_ref):
    def widen(n1, carry):
        src = pl.multiple_of(n1 * FFT_N2, FFT_N2)
        dst = pl.multiple_of(n1 * FFT_IN_PITCH, 8)
        uf_ref[pl.ds(dst, FFT_N2), :] = u_ref[0, pl.ds(src, FFT_N2), :].astype(F32)
        return carry
    lax.fori_loop(0, FFT_N1, widen, 0, unroll=2 * FFT_UNROLL)

    def stage1(n2, carry):
        x = uf_ref[pl.ds(n2, FFT_N1, stride=FFT_IN_PITCH), :].astype(BF16)
        y = jnp.dot(t1_ref[n2], x, preferred_element_type=F32)
        dst = pl.multiple_of(n2 * FFT_MID_PITCH, 8)
        yr_ref[pl.ds(dst, FFT_N1), :] = y[:FFT_N1]
        yi_ref[pl.ds(dst, FFT_N1), :] = y[FFT_N1:]
        return carry
    lax.fori_loop(0, FFT_N2, stage1, 0, unroll=FFT_UNROLL)

    def stage2(k1, carry):
        rows = pl.ds(k1, FFT_N2, stride=FFT_MID_PITCH)
        y = jnp.concatenate([yr_ref[rows, :], yi_ref[rows, :]], axis=0).astype(BF16)
        gk = jnp.dot(t2_ref[...], y, preferred_element_type=F32)
        gr_ref[rows, :] = gk[:FFT_N2]
        gi_ref[rows, :] = gk[FFT_N2:]
        return carry
    lax.fori_loop(0, FFT_N1, stage2, 0, unroll=FFT_UNROLL)

    def stage3(k2, carry):
        src = pl.multiple_of(k2 * FFT_MID_PITCH, 8)
        gk = jnp.concatenate([gr_ref[pl.ds(src, FFT_N1), :], gi_ref[pl.ds(src, FFT_N1), :]],
                             axis=1).astype(BF16)
        out = jnp.dot(gk, t3_ref[...], preferred_element_type=F32)
        o_ref[0, pl.ds(pl.multiple_of(k2 * FFT_N1, FFT_N1), FFT_N1), :] = out.astype(BF16)
        return carry
    lax.fori_loop(0, FFT_N2, stage3, 0, unroll=FFT_UNROLL)


def _fft(u, *, batch, seq):
    t1, t2, t3 = _fft_tables(seq)
    blk = pl.BlockSpec((1, seq, FOURIER_GROUP_DIM), lambda b, g: (b, 0, g))
    return pl.pallas_call(
        _fft_kernel,
        out_shape=jax.ShapeDtypeStruct((batch, seq, FOURIER_WIDTH), BF16),
        grid=(batch, FOURIER_GROUPS),
        in_specs=[blk, _resident(t1.shape), _resident(t2.shape), _resident(t3.shape)],
        out_specs=blk,
        scratch_shapes=[pltpu.VMEM((FFT_N1 * FFT_IN_PITCH, LANES), F32)]
        + [pltpu.VMEM((FFT_N2 * FFT_MID_PITCH, LANES), F32)] * 4,
        compiler_params=_compiler_params(("parallel", "parallel")),
        name="fft",
    )(u, t1, t2, t3)


def _merge_ln_kernel(h_ref, a_ref, f_ref, wgate_ref, bgate_ref, wpa_ref, wpf_ref, wout_ref,
                     g_ref, b_ref, o_ref, *, alpha):
    h = h_ref[...]
    d = h.shape[1]
    hb = h.astype(BF16)
    pa = jnp.dot(a_ref[...], wpa_ref[...], preferred_element_type=F32)
    pf = jnp.dot(f_ref[...], wpf_ref[...], preferred_element_type=F32)
    ga = jax.nn.sigmoid(jnp.dot(hb, wgate_ref[:, :d], preferred_element_type=F32)
                        + bgate_ref[:, :d])
    gf = jax.nn.sigmoid(jnp.dot(hb, wgate_ref[:, d:], preferred_element_type=F32)
                        + bgate_ref[:, d:])
    merged = (ga * pa + gf * pf).astype(BF16)
    y = jnp.dot(merged, wout_ref[...], preferred_element_type=F32)
    o_ref[...] = _layer_norm(alpha * h + y, g_ref[...], b_ref[...])


def _merge_ln(h, attn, four, w_gate, b_gate, w_pa, w_pf, w_out, ln_g, ln_b, *, alpha, tm=512):
    n, d = h.shape
    row = pl.BlockSpec((tm, d), lambda i: (i, 0))
    half = pl.BlockSpec((tm, attn.shape[1]), lambda i: (i, 0))
    return pl.pallas_call(
        functools.partial(_merge_ln_kernel, alpha=alpha),
        out_shape=jax.ShapeDtypeStruct((n, d), F32),
        grid=(n // tm,),
        in_specs=[row, half, half, _resident(w_gate.shape), _resident(b_gate.shape),
                  _resident(w_pa.shape), _resident(w_pf.shape), _resident(w_out.shape),
                  _resident((1, d)), _resident((1, d))],
        out_specs=row,
        compiler_params=_compiler_params(("parallel",)),
        name="merge_ln",
    )(h, attn, four, w_gate, b_gate, w_pa, w_pf, w_out, ln_g.reshape(1, d), ln_b.reshape(1, d))


def _mixer_ln(h, w_in, b_in, rel_bias, w_proj_attn, w_proj_fourier, w_out, ln_g, ln_b,
              *, alpha, batch, seq):
    d = h.shape[1]
    a = ATTN_QKV_WIDTH
    gate0 = 3 * a + FOURIER_WIDTH
    pieces = [(part * a + g * GROUP_WIDTH, HEAD_DIM ** -0.5 if part == 0 else 1.0)
              for g in range(N_GROUPS) for part in range(3)]
    w_qkvu = jnp.concatenate([w_in[:, s:s + GROUP_WIDTH] * f for s, f in pieces]
                             + [w_in[:, 3 * a:gate0]], axis=1).astype(BF16)
    b_qkvu = jnp.concatenate([b_in[s:s + GROUP_WIDTH] * f for s, f in pieces]
                             + [b_in[3 * a:gate0]]).reshape(1, -1)
    w_gate = w_in[:, gate0:].astype(BF16)
    b_gate = b_in[gate0:].reshape(1, -1)

    qkv0, qkv1, qkv2, u = _in_proj(h, w_qkvu, b_qkvu, batch=batch, seq=seq)
    attn = _attention((qkv0, qkv1, qkv2), _attention_bias(rel_bias), batch=batch, seq=seq)
    four = _fft(u, batch=batch, seq=seq)
    return _merge_ln(h, attn.reshape(batch * seq, -1), four.reshape(batch * seq, -1),
                     w_gate, b_gate, w_proj_attn.astype(BF16), w_proj_fourier.astype(BF16),
                     w_out.astype(BF16), ln_g, ln_b, alpha=alpha)


def kernel(x, ln1_g, ln1_b, ffn1_w_gate, ffn1_w_up, ffn1_w_down, w_in, b_in, rel_bias,
           w_proj_attn, w_proj_fourier, w_out, ln2_g, ln2_b, ffn2_w_gate, ffn2_w_up,
           ffn2_w_down, ln3_g, ln3_b):
    batch, seq, d = x.shape
    depth = ln1_g.shape[0]
    alpha = (2 * depth) ** 0.25
    h = x.reshape(batch * seq, d)
    for l in range(depth):
        h = _ffn_ln(h, ffn1_w_gate[l], ffn1_w_up[l], ffn1_w_down[l], ln1_g[l], ln1_b[l],
                    alpha=alpha)
        h = _mixer_ln(h, w_in[l], b_in[l], rel_bias, w_proj_attn[l], w_proj_fourier[l],
                      w_out[l], ln2_g[l], ln2_b[l], alpha=alpha, batch=batch, seq=seq)
        h = _ffn_ln(h, ffn2_w_gate[l], ffn2_w_up[l], ffn2_w_down[l], ln3_g[l], ln3_b[l],
                    alpha=alpha)
    return h.reshape(batch, seq, d)
```

```python
import functools
import math

import numpy as np
import jax
import jax.numpy as jnp
from jax import lax
from jax.experimental import pallas as pl
from jax.experimental.pallas import tpu as pltpu

F32 = jnp.float32
BF16 = jnp.bfloat16

HEAD_DIM = 128
HEADS_PER_GROUP = 4
ATTN_PATTERNS = ((128, 1), (512, 4), (2048, 16))
N_GROUPS = len(ATTN_PATTERNS)
GROUP_WIDTH = HEADS_PER_GROUP * HEAD_DIM
ATTN_QKV_WIDTH = N_GROUPS * GROUP_WIDTH
FOURIER_GROUPS = 4
FOURIER_GROUP_DIM = 128
FOURIER_WIDTH = FOURIER_GROUPS * FOURIER_GROUP_DIM
NUM_BUCKETS = 32
MAX_EXACT = 8
MAX_DISTANCE = 1024
NEG_INF = -1e30
LN_EPS = 1e-5
LOG2_E = math.log2(math.e)

LANES = 128
MXU_DIM = 256
V7X_VMEM_BYTES = 64 * 1024 * 1024
VMEM_LIMIT_BYTES = V7X_VMEM_BYTES * 7 // 8

ROW_TILE = 1024
SUB_ROWS = 512

HALF_WIN = 64
Q_TILE = 128
K_TILE = Q_TILE + 2 * HALF_WIN
ATTN_UNROLL = 16
FFT_UNROLL = 16

FFT_N1 = 128
FFT_N2 = 64
FFT_GROUPS_PER_STEP = 1
FFT_IN_PITCH = FFT_N2 + 8
FFT_MID_PITCH = FFT_N1 + 8


def _compiler_params(semantics):
    return pltpu.CompilerParams(dimension_semantics=semantics,
                                vmem_limit_bytes=VMEM_LIMIT_BYTES)


def _resident(shape):
    zeros = (0,) * len(shape)
    return pl.BlockSpec(shape, lambda *_: zeros, pipeline_mode=pl.Buffered(1))


def _layer_norm(z, g, b):
    mu = jnp.mean(z, axis=-1, keepdims=True)
    zc = z - mu
    var = jnp.mean(zc * zc, axis=-1, keepdims=True)
    return zc * lax.rsqrt(var + LN_EPS) * g + b


def _ffn_ln_kernel(x_ref, wgu_ref, wd_ref, g_ref, b_ref, o_ref, *, alpha, f_chunk):
    for s in range(x_ref.shape[0] // SUB_ROWS):
        rows = slice(s * SUB_ROWS, (s + 1) * SUB_ROWS)
        x = x_ref[rows, :]
        xb = x.astype(BF16)
        y = jnp.zeros(x.shape, F32)
        for c in range(wd_ref.shape[0] // f_chunk):
            gu = jnp.dot(xb, wgu_ref[:, 2 * c * f_chunk:2 * (c + 1) * f_chunk],
                         preferred_element_type=F32)
            gate, up = gu[:, :f_chunk], gu[:, f_chunk:]
            act = (gate * jax.nn.sigmoid(gate) * up).astype(BF16)
            y = y + jnp.dot(act, wd_ref[c * f_chunk:(c + 1) * f_chunk, :],
                            preferred_element_type=F32)
        o_ref[rows, :] = _layer_norm(alpha * x + 0.5 * y, g_ref[...], b_ref[...])


def _ffn_ln(x, w_gate, w_up, w_down, ln_g, ln_b, *, alpha, tm=ROW_TILE, f_chunk=MXU_DIM):
    n, d = x.shape
    f = w_gate.shape[1]
    n_chunks = -(-f // f_chunk)
    f_pad = n_chunks * f_chunk

    def chunked(w):
        return jnp.pad(w.astype(BF16), ((0, 0), (0, f_pad - f))).reshape(d, n_chunks, f_chunk)

    wgu = jnp.stack([chunked(w_gate), chunked(w_up)], axis=2).reshape(d, 2 * f_pad)
    wd = jnp.pad(w_down.astype(BF16), ((0, f_pad - f), (0, 0)))
    row = pl.BlockSpec((tm, d), lambda i: (i, 0))
    return pl.pallas_call(
        functools.partial(_ffn_ln_kernel, alpha=alpha, f_chunk=f_chunk),
        out_shape=jax.ShapeDtypeStruct((n, d), F32),
        grid=(n // tm,),
        in_specs=[row, _resident((d, 2 * f_pad)), _resident((f_pad, d)),
                  _resident((1, d)), _resident((1, d))],
        out_specs=row,
        compiler_params=_compiler_params(("parallel",)),
        name="ffn_ln",
    )(x, wgu, wd, ln_g.reshape(1, d), ln_b.reshape(1, d))


def _in_proj_kernel(x_ref, w_ref, b_ref, qkv0_ref, qkv1_ref, qkv2_ref, u_ref, stage_ref):
    gw = 3 * GROUP_WIDTH
    for t in range(x_ref.shape[0] // SUB_ROWS):
        rows = slice(t * SUB_ROWS, (t + 1) * SUB_ROWS)
        xb = x_ref[rows, :].astype(BF16)

        def proj(c0, xb=xb):
            return (jnp.dot(xb, w_ref[:, c0:c0 + MXU_DIM], preferred_element_type=F32)
                    + b_ref[:, c0:c0 + MXU_DIM])

        for g, out_ref in ((2, qkv2_ref), (1, qkv1_ref)):
            dil = ATTN_PATTERNS[g][1]
            n_sub = SUB_ROWS // dil
            for j in range(gw // MXU_DIM):
                res = proj(g * gw + j * MXU_DIM)
                for jj in range(MXU_DIM // LANES):
                    stage_ref[t, g - 1, j * 2 + jj] = res[:, jj * LANES:(jj + 1) * LANES]
            for r in range(dil):
                for s in range(gw // LANES):
                    out_ref[0, r, t * n_sub:(t + 1) * n_sub, s * LANES:(s + 1) * LANES] = (
                        stage_ref[t, g - 1, s, pl.ds(r, n_sub, stride=dil), :].astype(BF16))

        for j in range(gw // MXU_DIM):
            qkv0_ref[0, 0, rows, j * MXU_DIM:(j + 1) * MXU_DIM] = proj(j * MXU_DIM).astype(BF16)
        for j in range(FOURIER_WIDTH // MXU_DIM):
            u_ref[0, rows, j * MXU_DIM:(j + 1) * MXU_DIM] = (
                proj(N_GROUPS * gw + j * MXU_DIM).astype(BF16))


def _in_proj(h, w, b, *, batch, seq, tm=ROW_TILE):
    n, d = h.shape
    width = w.shape[1]
    gw = 3 * GROUP_WIDTH
    tiles_per_seq = seq // tm
    d1, d2 = ATTN_PATTERNS[1][1], ATTN_PATTERNS[2][1]
    out_shape = (
        jax.ShapeDtypeStruct((batch, 1, seq, gw), BF16),
        jax.ShapeDtypeStruct((batch, d1, seq // d1, gw), BF16),
        jax.ShapeDtypeStruct((batch, d2, seq // d2, gw), BF16),
        jax.ShapeDtypeStruct((batch, seq, FOURIER_WIDTH), BF16),
    )

    def bt(i):
        return i // tiles_per_seq, i % tiles_per_seq

    out_specs = (
        pl.BlockSpec((1, 1, tm, gw), lambda i: (bt(i)[0], 0, bt(i)[1], 0)),
        pl.BlockSpec((1, d1, tm // d1, gw), lambda i: (bt(i)[0], 0, bt(i)[1], 0)),
        pl.BlockSpec((1, d2, tm // d2, gw), lambda i: (bt(i)[0], 0, bt(i)[1], 0)),
        pl.BlockSpec((1, tm, FOURIER_WIDTH), lambda i: (bt(i)[0], bt(i)[1], 0)),
    )
    return pl.pallas_call(
        _in_proj_kernel,
        out_shape=out_shape,
        grid=(n // tm,),
        in_specs=[pl.BlockSpec((tm, d), lambda i: (i, 0)), _resident((d, width)),
                  _resident((1, width))],
        out_specs=out_specs,
        scratch_shapes=[pltpu.VMEM((tm // SUB_ROWS, 2, gw // LANES, SUB_ROWS, LANES), F32)],
        compiler_params=_compiler_params(("parallel",)),
        name="in_proj",
    )(h, w, b)


def _t5_bucket_np(rel):
    half = NUM_BUCKETS // 2
    ret = (rel > 0).astype(np.int64) * half
    n = np.abs(rel)
    nf = np.maximum(n, 1).astype(np.float64)
    large = MAX_EXACT + (np.log(nf / MAX_EXACT) / math.log(MAX_DISTANCE / MAX_EXACT)
                         * (half - MAX_EXACT)).astype(np.int64)
    large = np.minimum(large, half - 1)
    return ret + np.where(n < MAX_EXACT, n, large)


def _attention_bias(rel_bias):
    offs = np.arange(-HALF_WIN, HALF_WIN + 1)
    band = []
    for g, (_, dil) in enumerate(ATTN_PATTERNS):
        buckets = _t5_bucket_np(offs * dil)
        heads = rel_bias[:, g * HEADS_PER_GROUP:(g + 1) * HEADS_PER_GROUP]
        band.append(jnp.take(heads, jnp.asarray(buckets, jnp.int32), axis=0).T)
    band = jnp.stack(band).astype(F32) * LOG2_E
    period = Q_TILE + K_TILE
    base = jnp.concatenate(
        [band, jnp.full(band.shape[:2] + (period - band.shape[-1],), NEG_INF, F32)], axis=-1)
    variants = []
    for v in range(3):
        first_row = jnp.roll(base, HALF_WIN * (v - 1), axis=-1)
        seq = jnp.tile(first_row, (1, 1, Q_TILE))[..., :Q_TILE * (period - 1)]
        variants.append(seq.reshape(band.shape[:2] + (Q_TILE, period - 1))[..., :K_TILE])
    return jnp.stack(variants, axis=2)


def _attn_kernel(q0, k0, v0, q1, k1, v1, q2, k2, v2, bias_ref, o_ref, acc_ref, max_ref, den_ref,
                 *, chunk):
    c = pl.program_id(2)
    for g, (q_ref, k_ref, v_ref) in enumerate(((q0, k0, v0), (q1, k1, v1), (q2, k2, v2))):
        dil = ATTN_PATTERNS[g][1]
        sub_len = k_ref.shape[2]
        rows_per_sub = chunk // dil
        tiles_per_sub = rows_per_sub // Q_TILE

        def tile(i, carry, g=g, dil=dil, sub_len=sub_len, rows_per_sub=rows_per_sub,
                 tiles_per_sub=tiles_per_sub, q_ref=q_ref, k_ref=k_ref, v_ref=v_ref):
            r = i // tiles_per_sub
            t = i % tiles_per_sub
            q_start = c * rows_per_sub + t * Q_TILE
            k_start = jnp.clip(q_start - HALF_WIN, 0, sub_len - K_TILE)
            variant = (q_start - k_start) // HALF_WIN
            k_start = pl.multiple_of(k_start, HALF_WIN)
            q = q_ref[0, r, pl.ds(pl.multiple_of(t * Q_TILE, Q_TILE), Q_TILE), :]
            k = k_ref[0, r, pl.ds(k_start, K_TILE), :]
            v = v_ref[0, r, pl.ds(k_start, K_TILE), :]
            s = lax.dot_general(q, k, (((1,), (1,)), ((), ())), preferred_element_type=F32)
            s = s + bias_ref[g, 0, variant]
            m = jnp.max(s, axis=-1, keepdims=True)
            p = jnp.exp2(s - m)
            den = jnp.sum(p, axis=-1, keepdims=True)
            row0 = t * Q_TILE * dil + r
            rows = pl.ds(row0, Q_TILE) if dil == 1 else pl.ds(row0, Q_TILE, stride=dil)
            acc_ref[g, rows, :] = jnp.dot(p.astype(BF16), v, preferred_element_type=F32)
            max_ref[g, rows, :] = jnp.broadcast_to(m, (Q_TILE, LANES))
            den_ref[g, rows, :] = jnp.broadcast_to(den, (Q_TILE, LANES))
            return carry

        lax.fori_loop(0, dil * tiles_per_sub, tile, 0, unroll=ATTN_UNROLL)

    blk = 256
    for i in range(chunk // blk):
        rows = slice(i * blk, (i + 1) * blk)
        ms = [max_ref[g, rows, :] for g in range(N_GROUPS)]
        mx = jnp.maximum(jnp.maximum(ms[0], ms[1]), ms[2])
        ws = [jnp.exp2(m - mx) for m in ms]
        num = ws[0] * acc_ref[0, rows, :] + ws[1] * acc_ref[1, rows, :] + ws[2] * acc_ref[2, rows, :]
        z = ws[0] * den_ref[0, rows, :] + ws[1] * den_ref[1, rows, :] + ws[2] * den_ref[2, rows, :]
        o_ref[0, rows, :] = (num / z).astype(BF16)


def _attention(qkv, bias, *, batch, seq, chunk=2048):
    in_specs = []
    for g, arr in enumerate(qkv):
        dil = ATTN_PATTERNS[g][1]
        sub_len = seq // dil
        in_specs += [
            pl.BlockSpec((1, dil, chunk // dil, HEAD_DIM), lambda b, h, c: (b, 0, c, h)),
            pl.BlockSpec((1, dil, sub_len, HEAD_DIM),
                         lambda b, h, c: (b, 0, 0, HEADS_PER_GROUP + h)),
            pl.BlockSpec((1, dil, sub_len, HEAD_DIM),
                         lambda b, h, c: (b, 0, 0, 2 * HEADS_PER_GROUP + h)),
        ]
    in_specs.append(pl.BlockSpec((N_GROUPS, 1, 3, Q_TILE, K_TILE),
                                 lambda b, h, c: (0, h, 0, 0, 0)))
    args = [a for arr in qkv for a in (arr, arr, arr)] + [bias]
    return pl.pallas_call(
        functools.partial(_attn_kernel, chunk=chunk),
        out_shape=jax.ShapeDtypeStruct((batch, seq, GROUP_WIDTH), BF16),
        grid=(batch, HEADS_PER_GROUP, seq // chunk),
        in_specs=in_specs,
        out_specs=pl.BlockSpec((1, chunk, HEAD_DIM), lambda b, h, c: (b, c, h)),
        scratch_shapes=[pltpu.VMEM((N_GROUPS, chunk, LANES), F32)] * 3,
        compiler_params=_compiler_params(("parallel", "parallel", "arbitrary")),
        name="attention",
    )(*args)


def _fft_tables(seq):
    assert seq == FFT_N1 * FFT_N2
    k1 = np.arange(FFT_N1)[None, :, None]
    n1 = np.arange(FFT_N1)[None, None, :]
    n2 = np.arange(FFT_N2)[:, None, None]
    ang = 2.0 * np.pi * ((n2 * k1 + FFT_N2 * n1 * k1) % seq) / seq
    t1 = np.concatenate([np.cos(ang), -np.sin(ang)], axis=1)
    k2 = np.arange(FFT_N2)[:, None]
    m2 = np.arange(FFT_N2)[None, :]
    ang2 = 2.0 * np.pi * ((k2 * m2) % FFT_N2) / FFT_N2
    cr, ci = np.cos(ang2), -np.sin(ang2)
    t2 = np.block([[cr, -ci], [ci, cr]])
    cc = np.arange(FOURIER_GROUP_DIM)
    ang3 = 2.0 * np.pi * ((cc[:, None] * cc[None, :]) % FOURIER_GROUP_DIM) / FOURIER_GROUP_DIM
    norm = 1.0 / math.sqrt(seq * FOURIER_GROUP_DIM)
    t3 = np.concatenate([np.cos(ang3), np.sin(ang3)], axis=0) * norm
    return (jnp.asarray(t1, BF16), jnp.asarray(t2, BF16), jnp.asarray(t3, BF16))


def _fft_kernel(u_ref, t1_ref, t2_ref, t3_ref, o_ref, uf_ref, yr_ref, yi_ref, gi_ref):
    slabs = range(FFT_GROUPS_PER_STEP)
    gr_ref = uf_ref

    def lanes(h):
        return slice(h * LANES, (h + 1) * LANES)

    def widen(n1, carry):
        src = pl.multiple_of(n1 * FFT_N2, FFT_N2)
        dst = pl.multiple_of(n1 * FFT_IN_PITCH, 8)
        u = u_ref[0, pl.ds(src, FFT_N2), :].astype(F32)
        for h in slabs:
            uf_ref[h, pl.ds(dst, FFT_N2), :] = u[:, lanes(h)]
        return carry
    lax.fori_loop(0, FFT_N1, widen, 0, unroll=FFT_UNROLL)

    def stage1(n2, carry):
        rows = pl.ds(n2, FFT_N1, stride=FFT_IN_PITCH)
        x = jnp.concatenate([uf_ref[h, rows, :] for h in slabs], axis=1).astype(BF16)
        y = jnp.dot(t1_ref[n2], x, preferred_element_type=F32)
        dst = pl.multiple_of(n2 * FFT_MID_PITCH, 8)
        for h in slabs:
            yr_ref[h, pl.ds(dst, FFT_N1), :] = y[:FFT_N1, lanes(h)]
            yi_ref[h, pl.ds(dst, FFT_N1), :] = y[FFT_N1:, lanes(h)]
        return carry
    lax.fori_loop(0, FFT_N2, stage1, 0, unroll=FFT_UNROLL)

    def stage2(k1, carry):
        rows = pl.ds(k1, FFT_N2, stride=FFT_MID_PITCH)
        y = jnp.concatenate(
            [jnp.concatenate([yr_ref[h, rows, :], yi_ref[h, rows, :]], axis=0) for h in slabs],
            axis=1).astype(BF16)
        gk = jnp.dot(t2_ref[...], y, preferred_element_type=F32)
        for h in slabs:
            gr_ref[h, rows, :] = gk[:FFT_N2, lanes(h)]
            gi_ref[h, rows, :] = gk[FFT_N2:, lanes(h)]
        return carry
    lax.fori_loop(0, FFT_N1, stage2, 0, unroll=FFT_UNROLL)

    def stage3(k2, carry):
        src = pl.ds(pl.multiple_of(k2 * FFT_MID_PITCH, 8), FFT_N1)
        dst = pl.ds(pl.multiple_of(k2 * FFT_N1, FFT_N1), FFT_N1)
        for h in slabs:
            gk = jnp.concatenate([gr_ref[h, src, :], gi_ref[h, src, :]], axis=1).astype(BF16)
            out = jnp.dot(gk, t3_ref[...], preferred_element_type=F32)
            o_ref[0, dst, lanes(h)] = out.astype(BF16)
        return carry
    lax.fori_loop(0, FFT_N2, stage3, 0, unroll=FFT_UNROLL)


def _fft(u, *, batch, seq):
    t1, t2, t3 = _fft_tables(seq)
    width = FFT_GROUPS_PER_STEP * FOURIER_GROUP_DIM
    blk = pl.BlockSpec((1, seq, width), lambda b, g: (b, 0, g))
    return pl.pallas_call(
        _fft_kernel,
        out_shape=jax.ShapeDtypeStruct((batch, seq, FOURIER_WIDTH), BF16),
        grid=(batch, FOURIER_WIDTH // width),
        in_specs=[blk, _resident(t1.shape), _resident(t2.shape), _resident(t3.shape)],
        out_specs=blk,
        scratch_shapes=[pltpu.VMEM((FFT_GROUPS_PER_STEP, FFT_N1 * FFT_IN_PITCH, LANES), F32)]
        + [pltpu.VMEM((FFT_GROUPS_PER_STEP, FFT_N2 * FFT_MID_PITCH, LANES), F32)] * 3,
        compiler_params=_compiler_params(("parallel", "parallel")),
        name="fft",
    )(u, t1, t2, t3)


def _merge_ln_kernel(h_ref, a_ref, f_ref, wgate_ref, bgate_ref, wpa_ref, wpf_ref, wout_ref,
                     g_ref, b_ref, o_ref, *, alpha):
    d = h_ref.shape[1]
    for s in range(h_ref.shape[0] // SUB_ROWS):
        rows = slice(s * SUB_ROWS, (s + 1) * SUB_ROWS)
        h = h_ref[rows, :]
        hb = h.astype(BF16)
        a = a_ref[rows, :]
        f = f_ref[rows, :]
        y = jnp.zeros(h.shape, F32)
        for c in range(d // MXU_DIM):
            ca = slice(c * MXU_DIM, (c + 1) * MXU_DIM)
            cf = slice(d + c * MXU_DIM, d + (c + 1) * MXU_DIM)
            ga = jax.nn.sigmoid(jnp.dot(hb, wgate_ref[:, ca], preferred_element_type=F32)
                                + bgate_ref[:, ca])
            gf = jax.nn.sigmoid(jnp.dot(hb, wgate_ref[:, cf], preferred_element_type=F32)
                                + bgate_ref[:, cf])
            pa = jnp.dot(a, wpa_ref[:, ca], preferred_element_type=F32)
            pf = jnp.dot(f, wpf_ref[:, ca], preferred_element_type=F32)
            merged = (ga * pa + gf * pf).astype(BF16)
            y = y + jnp.dot(merged, wout_ref[ca, :], preferred_element_type=F32)
        o_ref[rows, :] = _layer_norm(alpha * h + y, g_ref[...], b_ref[...])


def _merge_ln(h, attn, four, w_gate, b_gate, w_pa, w_pf, w_out, ln_g, ln_b, *, alpha,
              tm=ROW_TILE):
    n, d = h.shape
    row = pl.BlockSpec((tm, d), lambda i: (i, 0))
    half = pl.BlockSpec((tm, attn.shape[1]), lambda i: (i, 0))
    return pl.pallas_call(
        functools.partial(_merge_ln_kernel, alpha=alpha),
        out_shape=jax.ShapeDtypeStruct((n, d), F32),
        grid=(n // tm,),
        in_specs=[row, half, half, _resident(w_gate.shape), _resident(b_gate.shape),
                  _resident(w_pa.shape), _resident(w_pf.shape), _resident(w_out.shape),
                  _resident((1, d)), _resident((1, d))],
        out_specs=row,
        compiler_params=_compiler_params(("parallel",)),
        name="merge_ln",
    )(h, attn, four, w_gate, b_gate, w_pa, w_pf, w_out, ln_g.reshape(1, d), ln_b.reshape(1, d))


def _mixer_ln(h, w_in, b_in, rel_bias, w_proj_attn, w_proj_fourier, w_out, ln_g, ln_b,
              *, alpha, batch, seq):
    d = h.shape[1]
    a = ATTN_QKV_WIDTH
    gate0 = 3 * a + FOURIER_WIDTH
    pieces = [(part * a + g * GROUP_WIDTH, HEAD_DIM ** -0.5 * LOG2_E if part == 0 else 1.0)
              for g in range(N_GROUPS) for part in range(3)]
    w_qkvu = jnp.concatenate([w_in[:, s:s + GROUP_WIDTH] * f for s, f in pieces]
                             + [w_in[:, 3 * a:gate0]], axis=1).astype(BF16)
    b_qkvu = jnp.concatenate([b_in[s:s + GROUP_WIDTH] * f for s, f in pieces]
                             + [b_in[3 * a:gate0]]).reshape(1, -1)
    w_gate = w_in[:, gate0:].astype(BF16)
    b_gate = b_in[gate0:].reshape(1, -1)

    qkv0, qkv1, qkv2, u = _in_proj(h, w_qkvu, b_qkvu, batch=batch, seq=seq)
    attn = _attention((qkv0, qkv1, qkv2), _attention_bias(rel_bias), batch=batch, seq=seq)
    four = _fft(u, batch=batch, seq=seq)
    return _merge_ln(h, attn.reshape(batch * seq, -1), four.reshape(batch * seq, -1),
                     w_gate, b_gate, w_proj_attn.astype(BF16), w_proj_fourier.astype(BF16),
                     w_out.astype(BF16), ln_g, ln_b, alpha=alpha)


def kernel(x, ln1_g, ln1_b, ffn1_w_gate, ffn1_w_up, ffn1_w_down, w_in, b_in, rel_bias,
           w_proj_attn, w_proj_fourier, w_out, ln2_g, ln2_b, ffn2_w_gate, ffn2_w_up,
           ffn2_w_down, ln3_g, ln3_b):
    batch, seq, d = x.shape
    depth = ln1_g.shape[0]
    alpha = (2 * depth) ** 0.25
    h = x.reshape(batch * seq, d)
    for l in range(depth):
        h = _ffn_ln(h, ffn1_w_gate[l], ffn1_w_up[l], ffn1_w_down[l], ln1_g[l], ln1_b[l],
                    alpha=alpha)
        h = _mixer_ln(h, w_in[l], b_in[l], rel_bias, w_proj_attn[l], w_proj_fourier[l],
                      w_out[l], ln2_g[l], ln2_b[l], alpha=alpha, batch=batch, seq=seq)
        h = _ffn_ln(h, ffn2_w_gate[l], ffn2_w_up[l], ffn2_w_down[l], ln3_g[l], ln3_b[l],
                    alpha=alpha)
    return h.reshape(batch, seq, d)
```

```python
import functools
import math

import numpy as np
import jax
import jax.numpy as jnp
from jax import lax
from jax.experimental import pallas as pl
from jax.experimental.pallas import tpu as pltpu

F32 = jnp.float32
BF16 = jnp.bfloat16

HEAD_DIM = 128
HEADS_PER_GROUP = 4
ATTN_PATTERNS = ((128, 1), (512, 4), (2048, 16))
N_GROUPS = len(ATTN_PATTERNS)
GROUP_WIDTH = HEADS_PER_GROUP * HEAD_DIM
ATTN_QKV_WIDTH = N_GROUPS * GROUP_WIDTH
FOURIER_GROUPS = 4
FOURIER_GROUP_DIM = 128
FOURIER_WIDTH = FOURIER_GROUPS * FOURIER_GROUP_DIM
NUM_BUCKETS = 32
MAX_EXACT = 8
MAX_DISTANCE = 1024
NEG_INF = -1e30
LN_EPS = 1e-5
LOG2_E = math.log2(math.e)

LANES = 128
MXU_DIM = 256
V7X_VMEM_BYTES = 64 * 1024 * 1024
VMEM_LIMIT_BYTES = V7X_VMEM_BYTES * 7 // 8

ROW_TILE = 1024
SUB_ROWS = 512

HALF_WIN = 64
Q_TILE = 128
K_TILE = Q_TILE + 2 * HALF_WIN
ATTN_UNROLL = 16
FFT_UNROLL = 16

FFT_N1 = 128
FFT_N2 = 64
FFT_GROUPS_PER_STEP = 1
FFT_IN_PITCH = FFT_N2 + 8
FFT_MID_PITCH = FFT_N1 + 8


def _compiler_params(semantics):
    return pltpu.CompilerParams(dimension_semantics=semantics,
                                vmem_limit_bytes=VMEM_LIMIT_BYTES)


def _resident(shape):
    zeros = (0,) * len(shape)
    return pl.BlockSpec(shape, lambda *_: zeros, pipeline_mode=pl.Buffered(1))


def _layer_norm(z, g, b):
    mu = jnp.mean(z, axis=-1, keepdims=True)
    zc = z - mu
    var = jnp.mean(zc * zc, axis=-1, keepdims=True)
    return zc * lax.rsqrt(var + LN_EPS) * g + b


def _ffn_ln_kernel(x_ref, wg_ref, wu_ref, wd_ref, g_ref, b_ref, o_ref, *, alpha, f_chunk):
    f = wd_ref.shape[0]
    for s in range(x_ref.shape[0] // SUB_ROWS):
        rows = slice(s * SUB_ROWS, (s + 1) * SUB_ROWS)
        x = x_ref[rows, :]
        xb = x.astype(BF16)
        y = jnp.zeros(x.shape, F32)
        for c0 in range(0, f, f_chunk):
            sl = slice(c0, min(c0 + f_chunk, f))
            gate = jnp.dot(xb, wg_ref[:, sl], preferred_element_type=F32)
            up = jnp.dot(xb, wu_ref[:, sl], preferred_element_type=F32)
            act = (gate * jax.nn.sigmoid(gate) * up).astype(BF16)
            y = y + jnp.dot(act, wd_ref[sl, :], preferred_element_type=F32)
        o_ref[rows, :] = _layer_norm(alpha * x + 0.5 * y, g_ref[...], b_ref[...])


def _ffn_ln(x, w_gate, w_up, w_down, ln_g, ln_b, *, alpha, tm=ROW_TILE, f_chunk=MXU_DIM):
    n, d = x.shape
    f = w_gate.shape[1]
    row = pl.BlockSpec((tm, d), lambda i: (i, 0))
    return pl.pallas_call(
        functools.partial(_ffn_ln_kernel, alpha=alpha, f_chunk=f_chunk),
        out_shape=jax.ShapeDtypeStruct((n, d), F32),
        grid=(n // tm,),
        in_specs=[row, _resident((d, f)), _resident((d, f)), _resident((f, d)),
                  _resident((1, d)), _resident((1, d))],
        out_specs=row,
        compiler_params=_compiler_params(("parallel",)),
        name="ffn_ln",
    )(x, w_gate.astype(BF16), w_up.astype(BF16), w_down.astype(BF16),
      ln_g.reshape(1, d), ln_b.reshape(1, d))


def _in_proj_kernel(x_ref, w_ref, b_ref, qkv0_ref, qkv1_ref, qkv2_ref, u_ref, stage_ref):
    gw = 3 * GROUP_WIDTH
    for t in range(x_ref.shape[0] // SUB_ROWS):
        rows = slice(t * SUB_ROWS, (t + 1) * SUB_ROWS)
        xb = x_ref[rows, :].astype(BF16)

        def proj(c0, xb=xb):
            return (jnp.dot(xb, w_ref[:, c0:c0 + MXU_DIM], preferred_element_type=F32)
                    + b_ref[:, c0:c0 + MXU_DIM])

        for g, out_ref in ((2, qkv2_ref), (1, qkv1_ref)):
            dil = ATTN_PATTERNS[g][1]
            n_sub = SUB_ROWS // dil
            for j in range(gw // MXU_DIM):
                res = proj(g * gw + j * MXU_DIM)
                for jj in range(MXU_DIM // LANES):
                    stage_ref[t, g - 1, j * 2 + jj] = res[:, jj * LANES:(jj + 1) * LANES]
            for r in range(dil):
                for s in range(gw // LANES):
                    out_ref[0, r, t * n_sub:(t + 1) * n_sub, s * LANES:(s + 1) * LANES] = (
                        stage_ref[t, g - 1, s, pl.ds(r, n_sub, stride=dil), :].astype(BF16))

        for j in range(gw // MXU_DIM):
            qkv0_ref[0, 0, rows, j * MXU_DIM:(j + 1) * MXU_DIM] = proj(j * MXU_DIM).astype(BF16)
        for j in range(FOURIER_WIDTH // MXU_DIM):
            u_ref[0, rows, j * MXU_DIM:(j + 1) * MXU_DIM] = (
                proj(N_GROUPS * gw + j * MXU_DIM).astype(BF16))


def _in_proj(h, w, b, *, batch, seq, tm=ROW_TILE):
    n, d = h.shape
    width = w.shape[1]
    gw = 3 * GROUP_WIDTH
    tiles_per_seq = seq // tm
    d1, d2 = ATTN_PATTERNS[1][1], ATTN_PATTERNS[2][1]
    out_shape = (
        jax.ShapeDtypeStruct((batch, 1, seq, gw), BF16),
        jax.ShapeDtypeStruct((batch, d1, seq // d1, gw), BF16),
        jax.ShapeDtypeStruct((batch, d2, seq // d2, gw), BF16),
        jax.ShapeDtypeStruct((batch, seq, FOURIER_WIDTH), BF16),
    )

    def bt(i):
        return i // tiles_per_seq, i % tiles_per_seq

    out_specs = (
        pl.BlockSpec((1, 1, tm, gw), lambda i: (bt(i)[0], 0, bt(i)[1], 0)),
        pl.BlockSpec((1, d1, tm // d1, gw), lambda i: (bt(i)[0], 0, bt(i)[1], 0)),
        pl.BlockSpec((1, d2, tm // d2, gw), lambda i: (bt(i)[0], 0, bt(i)[1], 0)),
        pl.BlockSpec((1, tm, FOURIER_WIDTH), lambda i: (bt(i)[0], bt(i)[1], 0)),
    )
    return pl.pallas_call(
        _in_proj_kernel,
        out_shape=out_shape,
        grid=(n // tm,),
        in_specs=[pl.BlockSpec((tm, d), lambda i: (i, 0)), _resident((d, width)),
                  _resident((1, width))],
        out_specs=out_specs,
        scratch_shapes=[pltpu.VMEM((tm // SUB_ROWS, 2, gw // LANES, SUB_ROWS, LANES), F32)],
        compiler_params=_compiler_params(("parallel",)),
        name="in_proj",
    )(h, w, b)


def _t5_bucket_np(rel):
    half = NUM_BUCKETS // 2
    ret = (rel > 0).astype(np.int64) * half
    n = np.abs(rel)
    nf = np.maximum(n, 1).astype(np.float64)
    large = MAX_EXACT + (np.log(nf / MAX_EXACT) / math.log(MAX_DISTANCE / MAX_EXACT)
                         * (half - MAX_EXACT)).astype(np.int64)
    large = np.minimum(large, half - 1)
    return ret + np.where(n < MAX_EXACT, n, large)


def _attention_bias(rel_bias):
    offs = np.arange(-HALF_WIN, HALF_WIN + 1)
    band = []
    for g, (_, dil) in enumerate(ATTN_PATTERNS):
        buckets = _t5_bucket_np(offs * dil)
        heads = rel_bias[:, g * HEADS_PER_GROUP:(g + 1) * HEADS_PER_GROUP]
        band.append(jnp.take(heads, jnp.asarray(buckets, jnp.int32), axis=0).T)
    band = jnp.stack(band).astype(F32) * LOG2_E
    period = Q_TILE + K_TILE
    base = jnp.concatenate(
        [band, jnp.full(band.shape[:2] + (period - band.shape[-1],), NEG_INF, F32)], axis=-1)
    variants = []
    for v in range(3):
        first_row = jnp.roll(base, HALF_WIN * (v - 1), axis=-1)
        seq = jnp.tile(first_row, (1, 1, Q_TILE))[..., :Q_TILE * (period - 1)]
        variants.append(seq.reshape(band.shape[:2] + (Q_TILE, period - 1))[..., :K_TILE])
    return jnp.stack(variants, axis=2)


def _attn_kernel(q0, k0, v0, q1, k1, v1, q2, k2, v2, bias_ref, o_ref, out_ref, lse_ref, *, chunk):
    c = pl.program_id(2)
    for g, (q_ref, k_ref, v_ref) in enumerate(((q0, k0, v0), (q1, k1, v1), (q2, k2, v2))):
        dil = ATTN_PATTERNS[g][1]
        sub_len = k_ref.shape[2]
        rows_per_sub = chunk // dil
        tiles_per_sub = rows_per_sub // Q_TILE

        def tile(i, carry, g=g, dil=dil, sub_len=sub_len, rows_per_sub=rows_per_sub,
                 tiles_per_sub=tiles_per_sub, q_ref=q_ref, k_ref=k_ref, v_ref=v_ref):
            r = i // tiles_per_sub
            t = i % tiles_per_sub
            q_start = c * rows_per_sub + t * Q_TILE
            k_start = jnp.clip(q_start - HALF_WIN, 0, sub_len - K_TILE)
            variant = (q_start - k_start) // HALF_WIN
            k_start = pl.multiple_of(k_start, HALF_WIN)
            q = q_ref[0, r, pl.ds(pl.multiple_of(t * Q_TILE, Q_TILE), Q_TILE), :]
            k = k_ref[0, r, pl.ds(k_start, K_TILE), :]
            v = v_ref[0, r, pl.ds(k_start, K_TILE), :]
            s = lax.dot_general(q, k, (((1,), (1,)), ((), ())), preferred_element_type=F32)
            s = s + bias_ref[g, 0, variant]
            m = jnp.max(s, axis=-1, keepdims=True)
            p = jnp.exp2(s - m)
            den = jnp.sum(p, axis=-1, keepdims=True)
            row0 = t * Q_TILE * dil + r
            rows = pl.ds(row0, Q_TILE) if dil == 1 else pl.ds(row0, Q_TILE, stride=dil)
            out_ref[g, rows, :] = (jnp.dot(p.astype(BF16), v, preferred_element_type=F32)
                                   * (1.0 / den))
            lse_ref[g, rows, :] = jnp.broadcast_to(m + jnp.log2(den), (Q_TILE, LANES))
            return carry

        lax.fori_loop(0, dil * tiles_per_sub, tile, 0, unroll=ATTN_UNROLL)

    blk = 256
    for i in range(chunk // blk):
        rows = slice(i * blk, (i + 1) * blk)
        lses = [lse_ref[g, rows, :] for g in range(N_GROUPS)]
        mx = jnp.maximum(jnp.maximum(lses[0], lses[1]), lses[2])
        ws = [jnp.exp2(l - mx) for l in lses]
        num = ws[0] * out_ref[0, rows, :] + ws[1] * out_ref[1, rows, :] + ws[2] * out_ref[2, rows, :]
        o_ref[0, rows, :] = (num / (ws[0] + ws[1] + ws[2])).astype(BF16)


def _attention(qkv, bias, *, batch, seq, chunk=2048):
    in_specs = []
    for g, arr in enumerate(qkv):
        dil = ATTN_PATTERNS[g][1]
        sub_len = seq // dil
        in_specs += [
            pl.BlockSpec((1, dil, chunk // dil, HEAD_DIM), lambda b, h, c: (b, 0, c, h)),
            pl.BlockSpec((1, dil, sub_len, HEAD_DIM),
                         lambda b, h, c: (b, 0, 0, HEADS_PER_GROUP + h)),
            pl.BlockSpec((1, dil, sub_len, HEAD_DIM),
                         lambda b, h, c: (b, 0, 0, 2 * HEADS_PER_GROUP + h)),
        ]
    in_specs.append(pl.BlockSpec((N_GROUPS, 1, 3, Q_TILE, K_TILE),
                                 lambda b, h, c: (0, h, 0, 0, 0)))
    args = [a for arr in qkv for a in (arr, arr, arr)] + [bias]
    return pl.pallas_call(
        functools.partial(_attn_kernel, chunk=chunk),
        out_shape=jax.ShapeDtypeStruct((batch, seq, GROUP_WIDTH), BF16),
        grid=(batch, HEADS_PER_GROUP, seq // chunk),
        in_specs=in_specs,
        out_specs=pl.BlockSpec((1, chunk, HEAD_DIM), lambda b, h, c: (b, c, h)),
        scratch_shapes=[pltpu.VMEM((N_GROUPS, chunk, LANES), F32)] * 2,
        compiler_params=_compiler_params(("parallel", "parallel", "arbitrary")),
        name="attention",
    )(*args)


def _fft_tables(seq):
    assert seq == FFT_N1 * FFT_N2
    k1 = np.arange(FFT_N1)[None, :, None]
    n1 = np.arange(FFT_N1)[None, None, :]
    n2 = np.arange(FFT_N2)[:, None, None]
    ang = 2.0 * np.pi * ((n2 * k1 + FFT_N2 * n1 * k1) % seq) / seq
    t1 = np.concatenate([np.cos(ang), -np.sin(ang)], axis=1)
    k2 = np.arange(FFT_N2)[:, None]
    m2 = np.arange(FFT_N2)[None, :]
    ang2 = 2.0 * np.pi * ((k2 * m2) % FFT_N2) / FFT_N2
    cr, ci = np.cos(ang2), -np.sin(ang2)
    t2 = np.block([[cr, -ci], [ci, cr]])
    cc = np.arange(FOURIER_GROUP_DIM)
    ang3 = 2.0 * np.pi * ((cc[:, None] * cc[None, :]) % FOURIER_GROUP_DIM) / FOURIER_GROUP_DIM
    norm = 1.0 / math.sqrt(seq * FOURIER_GROUP_DIM)
    t3 = np.concatenate([np.cos(ang3), np.sin(ang3)], axis=0) * norm
    return (jnp.asarray(t1, BF16), jnp.asarray(t2, BF16), jnp.asarray(t3, BF16))


def _fft_kernel(u_ref, t1_ref, t2_ref, t3_ref, o_ref, uf_ref, yr_ref, yi_ref, gi_ref):
    slabs = range(FFT_GROUPS_PER_STEP)
    gr_ref = uf_ref

    def lanes(h):
        return slice(h * LANES, (h + 1) * LANES)

    def widen(n1, carry):
        src = pl.multiple_of(n1 * FFT_N2, FFT_N2)
        dst = pl.multiple_of(n1 * FFT_IN_PITCH, 8)
        u = u_ref[0, pl.ds(src, FFT_N2), :].astype(F32)
        for h in slabs:
            uf_ref[h, pl.ds(dst, FFT_N2), :] = u[:, lanes(h)]
        return carry
    lax.fori_loop(0, FFT_N1, widen, 0, unroll=FFT_UNROLL)

    def stage1(n2, carry):
        rows = pl.ds(n2, FFT_N1, stride=FFT_IN_PITCH)
        x = jnp.concatenate([uf_ref[h, rows, :] for h in slabs], axis=1).astype(BF16)
        y = jnp.dot(t1_ref[n2], x, preferred_element_type=F32)
        dst = pl.multiple_of(n2 * FFT_MID_PITCH, 8)
        for h in slabs:
            yr_ref[h, pl.ds(dst, FFT_N1), :] = y[:FFT_N1, lanes(h)]
            yi_ref[h, pl.ds(dst, FFT_N1), :] = y[FFT_N1:, lanes(h)]
        return carry
    lax.fori_loop(0, FFT_N2, stage1, 0, unroll=FFT_UNROLL)

    def stage2(k1, carry):
        rows = pl.ds(k1, FFT_N2, stride=FFT_MID_PITCH)
        y = jnp.concatenate(
            [jnp.concatenate([yr_ref[h, rows, :], yi_ref[h, rows, :]], axis=0) for h in slabs],
            axis=1).astype(BF16)
        gk = jnp.dot(t2_ref[...], y, preferred_element_type=F32)
        for h in slabs:
            gr_ref[h, rows, :] = gk[:FFT_N2, lanes(h)]
            gi_ref[h, rows, :] = gk[FFT_N2:, lanes(h)]
        return carry
    lax.fori_loop(0, FFT_N1, stage2, 0, unroll=FFT_UNROLL)

    def stage3(k2, carry):
        src = pl.ds(pl.multiple_of(k2 * FFT_MID_PITCH, 8), FFT_N1)
        dst = pl.ds(pl.multiple_of(k2 * FFT_N1, FFT_N1), FFT_N1)
        for h in slabs:
            gk = jnp.concatenate([gr_ref[h, src, :], gi_ref[h, src, :]], axis=1).astype(BF16)
            out = jnp.dot(gk, t3_ref[...], preferred_element_type=F32)
            o_ref[0, dst, lanes(h)] = out.astype(BF16)
        return carry
    lax.fori_loop(0, FFT_N2, stage3, 0, unroll=FFT_UNROLL)


def _fft(u, *, batch, seq):
    t1, t2, t3 = _fft_tables(seq)
    width = FFT_GROUPS_PER_STEP * FOURIER_GROUP_DIM
    blk = pl.BlockSpec((1, seq, width), lambda b, g: (b, 0, g))
    return pl.pallas_call(
        _fft_kernel,
        out_shape=jax.ShapeDtypeStruct((batch, seq, FOURIER_WIDTH), BF16),
        grid=(batch, FOURIER_WIDTH // width),
        in_specs=[blk, _resident(t1.shape), _resident(t2.shape), _resident(t3.shape)],
        out_specs=blk,
        scratch_shapes=[pltpu.VMEM((FFT_GROUPS_PER_STEP, FFT_N1 * FFT_IN_PITCH, LANES), F32)]
        + [pltpu.VMEM((FFT_GROUPS_PER_STEP, FFT_N2 * FFT_MID_PITCH, LANES), F32)] * 3,
        compiler_params=_compiler_params(("parallel", "parallel")),
        name="fft",
    )(u, t1, t2, t3)


def _merge_ln_kernel(h_ref, a_ref, f_ref, wgate_ref, bgate_ref, wpa_ref, wpf_ref, wout_ref,
                     g_ref, b_ref, o_ref, *, alpha):
    d = h_ref.shape[1]
    for s in range(h_ref.shape[0] // SUB_ROWS):
        rows = slice(s * SUB_ROWS, (s + 1) * SUB_ROWS)
        h = h_ref[rows, :]
        hb = h.astype(BF16)
        a = a_ref[rows, :]
        f = f_ref[rows, :]
        y = jnp.zeros(h.shape, F32)
        for c in range(d // MXU_DIM):
            ca = slice(c * MXU_DIM, (c + 1) * MXU_DIM)
            cf = slice(d + c * MXU_DIM, d + (c + 1) * MXU_DIM)
            ga = jax.nn.sigmoid(jnp.dot(hb, wgate_ref[:, ca], preferred_element_type=F32)
                                + bgate_ref[:, ca])
            gf = jax.nn.sigmoid(jnp.dot(hb, wgate_ref[:, cf], preferred_element_type=F32)
                                + bgate_ref[:, cf])
            pa = jnp.dot(a, wpa_ref[:, ca], preferred_element_type=F32)
            pf = jnp.dot(f, wpf_ref[:, ca], preferred_element_type=F32)
            merged = (ga * pa + gf * pf).astype(BF16)
            y = y + jnp.dot(merged, wout_ref[ca, :], preferred_element_type=F32)
        o_ref[rows, :] = _layer_norm(alpha * h + y, g_ref[...], b_ref[...])


def _merge_ln(h, attn, four, w_gate, b_gate, w_pa, w_pf, w_out, ln_g, ln_b, *, alpha,
              tm=ROW_TILE):
    n, d = h.shape
    row = pl.BlockSpec((tm, d), lambda i: (i, 0))
    half = pl.BlockSpec((tm, attn.shape[1]), lambda i: (i, 0))
    return pl.pallas_call(
        functools.partial(_merge_ln_kernel, alpha=alpha),
        out_shape=jax.ShapeDtypeStruct((n, d), F32),
        grid=(n // tm,),
        in_specs=[row, half, half, _resident(w_gate.shape), _resident(b_gate.shape),
                  _resident(w_pa.shape), _resident(w_pf.shape), _resident(w_out.shape),
                  _resident((1, d)), _resident((1, d))],
        out_specs=row,
        compiler_params=_compiler_params(("parallel",)),
        name="merge_ln",
    )(h, attn, four, w_gate, b_gate, w_pa, w_pf, w_out, ln_g.reshape(1, d), ln_b.reshape(1, d))


def _mixer_ln(h, w_in, b_in, rel_bias, w_proj_attn, w_proj_fourier, w_out, ln_g, ln_b,
              *, alpha, batch, seq):
    d = h.shape[1]
    a = ATTN_QKV_WIDTH
    gate0 = 3 * a + FOURIER_WIDTH
    pieces = [(part * a + g * GROUP_WIDTH, HEAD_DIM ** -0.5 * LOG2_E if part == 0 else 1.0)
              for g in range(N_GROUPS) for part in range(3)]
    w_qkvu = jnp.concatenate([w_in[:, s:s + GROUP_WIDTH] * f for s, f in pieces]
                             + [w_in[:, 3 * a:gate0]], axis=1).astype(BF16)
    b_qkvu = jnp.concatenate([b_in[s:s + GROUP_WIDTH] * f for s, f in pieces]
                             + [b_in[3 * a:gate0]]).reshape(1, -1)
    w_gate = w_in[:, gate0:].astype(BF16)
    b_gate = b_in[gate0:].reshape(1, -1)

    qkv0, qkv1, qkv2, u = _in_proj(h, w_qkvu, b_qkvu, batch=batch, seq=seq)
    attn = _attention((qkv0, qkv1, qkv2), _attention_bias(rel_bias), batch=batch, seq=seq)
    four = _fft(u, batch=batch, seq=seq)
    return _merge_ln(h, attn.reshape(batch * seq, -1), four.reshape(batch * seq, -1),
                     w_gate, b_gate, w_proj_attn.astype(BF16), w_proj_fourier.astype(BF16),
                     w_out.astype(BF16), ln_g, ln_b, alpha=alpha)


def kernel(x, ln1_g, ln1_b, ffn1_w_gate, ffn1_w_up, ffn1_w_down, w_in, b_in, rel_bias,
           w_proj_attn, w_proj_fourier, w_out, ln2_g, ln2_b, ffn2_w_gate, ffn2_w_up,
           ffn2_w_down, ln3_g, ln3_b):
    batch, seq, d = x.shape
    depth = ln1_g.shape[0]
    alpha = (2 * depth) ** 0.25
    h = x.reshape(batch * seq, d)
    for l in range(depth):
        h = _ffn_ln(h, ffn1_w_gate[l], ffn1_w_up[l], ffn1_w_down[l], ln1_g[l], ln1_b[l],
                    alpha=alpha)
        h = _mixer_ln(h, w_in[l], b_in[l], rel_bias, w_proj_attn[l], w_proj_fourier[l],
                      w_out[l], ln2_g[l], ln2_b[l], alpha=alpha, batch=batch, seq=seq)
        h = _ffn_ln(h, ffn2_w_gate[l], ffn2_w_up[l], ffn2_w_down[l], ln3_g[l], ln3_b[l],
                    alpha=alpha)
    return h.reshape(batch, seq, d)
```

```python
import functools
import math

import numpy as np
import jax
import jax.numpy as jnp
from jax import lax
from jax.experimental import pallas as pl
from jax.experimental.pallas import tpu as pltpu

F32 = jnp.float32
BF16 = jnp.bfloat16

HEAD_DIM = 128
HEADS_PER_GROUP = 4
ATTN_PATTERNS = ((128, 1), (512, 4), (2048, 16))
N_GROUPS = len(ATTN_PATTERNS)
GROUP_WIDTH = HEADS_PER_GROUP * HEAD_DIM
ATTN_QKV_WIDTH = N_GROUPS * GROUP_WIDTH
FOURIER_GROUPS = 4
FOURIER_GROUP_DIM = 128
FOURIER_WIDTH = FOURIER_GROUPS * FOURIER_GROUP_DIM
NUM_BUCKETS = 32
MAX_EXACT = 8
MAX_DISTANCE = 1024
NEG_INF = -1e30
LN_EPS = 1e-5
LOG2_E = math.log2(math.e)

LANES = 128
MXU_DIM = 256
V7X_VMEM_BYTES = 64 * 1024 * 1024
VMEM_LIMIT_BYTES = V7X_VMEM_BYTES * 7 // 8

ROW_TILE = 1024
SUB_ROWS = 512
SINGLE_OP_STRIDE = 4
STAGE_SLOTS = 3

HALF_WIN = 64
Q_TILE = 128
K_TILE = Q_TILE + 2 * HALF_WIN
ATTN_UNROLL = 16
FFT_UNROLL = 16

FFT_N1 = 128
FFT_N2 = 64
FFT_GROUPS_PER_STEP = 1
FFT_IN_PITCH = FFT_N2 + 8
FFT_MID_PITCH = FFT_N1 + 8


def _compiler_params(semantics):
    return pltpu.CompilerParams(dimension_semantics=semantics,
                                vmem_limit_bytes=VMEM_LIMIT_BYTES)


def _resident(shape):
    zeros = (0,) * len(shape)
    return pl.BlockSpec(shape, lambda *_: zeros, pipeline_mode=pl.Buffered(1))


def _layer_norm(z, g, b):
    mu = jnp.mean(z, axis=-1, keepdims=True)
    zc = z - mu
    var = jnp.mean(zc * zc, axis=-1, keepdims=True)
    return zc * lax.rsqrt(var + LN_EPS) * g + b


def _ffn_ln_kernel(x_ref, wg_ref, wu_ref, wd_ref, g_ref, b_ref, o_ref, *, alpha, f_chunk):
    f = wd_ref.shape[0]
    for s in range(x_ref.shape[0] // SUB_ROWS):
        rows = slice(s * SUB_ROWS, (s + 1) * SUB_ROWS)
        x = x_ref[rows, :]
        xb = x.astype(BF16)
        y = jnp.zeros(x.shape, F32)
        for c0 in range(0, f, f_chunk):
            sl = slice(c0, min(c0 + f_chunk, f))
            gate = jnp.dot(xb, wg_ref[:, sl], preferred_element_type=F32)
            up = jnp.dot(xb, wu_ref[:, sl], preferred_element_type=F32)
            act = (gate * jax.nn.sigmoid(gate) * up).astype(BF16)
            y = y + jnp.dot(act, wd_ref[sl, :], preferred_element_type=F32)
        o_ref[rows, :] = _layer_norm(alpha * x + 0.5 * y, g_ref[...], b_ref[...])


def _ffn_ln(x, w_gate, w_up, w_down, ln_g, ln_b, *, alpha, tm=ROW_TILE, f_chunk=MXU_DIM):
    n, d = x.shape
    f = w_gate.shape[1]
    row = pl.BlockSpec((tm, d), lambda i: (i, 0))
    return pl.pallas_call(
        functools.partial(_ffn_ln_kernel, alpha=alpha, f_chunk=f_chunk),
        out_shape=jax.ShapeDtypeStruct((n, d), F32),
        grid=(n // tm,),
        in_specs=[row, _resident((d, f)), _resident((d, f)), _resident((f, d)),
                  _resident((1, d)), _resident((1, d))],
        out_specs=row,
        compiler_params=_compiler_params(("parallel",)),
        name="ffn_ln",
    )(x, w_gate.astype(BF16), w_up.astype(BF16), w_down.astype(BF16),
      ln_g.reshape(1, d), ln_b.reshape(1, d))


def _store_deinterleaved(res, out_ref, dil, t, col0, stage_ref, mid_ref, slot):
    n = res.shape[0]
    n_sub = n // dil
    out_rows = slice(t * n_sub, (t + 1) * n_sub)
    inner = min(dil, SINGLE_OP_STRIDE)
    outer = dil // inner
    part = n // inner
    for h in range(res.shape[1] // LANES):
        cols = slice(col0 + h * LANES, col0 + (h + 1) * LANES)
        stage_ref[slot, h] = res[:, h * LANES:(h + 1) * LANES]
        if outer == 1:
            for r in range(dil):
                out_ref[0, r, out_rows, cols] = (
                    stage_ref[slot, h, pl.ds(r, n_sub, stride=dil), :].astype(BF16))
            continue
        for ra in range(inner):
            mid_ref[slot, h, ra * part:(ra + 1) * part] = (
                stage_ref[slot, h, pl.ds(ra, part, stride=inner), :])
        for ra in range(inner):
            for rb in range(outer):
                out_ref[0, ra + inner * rb, out_rows, cols] = (
                    mid_ref[slot, h, pl.ds(ra * part + rb, n_sub, stride=outer), :].astype(BF16))


def _in_proj_kernel(x_ref, w_ref, b_ref, qkv0_ref, qkv1_ref, qkv2_ref, u_ref, stage_ref, mid_ref):
    gw = 3 * GROUP_WIDTH
    piece = 0
    for t in range(x_ref.shape[0] // SUB_ROWS):
        rows = slice(t * SUB_ROWS, (t + 1) * SUB_ROWS)
        xb = x_ref[rows, :].astype(BF16)

        def proj(c0, xb=xb):
            return (jnp.dot(xb, w_ref[:, c0:c0 + MXU_DIM], preferred_element_type=F32)
                    + b_ref[:, c0:c0 + MXU_DIM])

        for g, out_ref in ((2, qkv2_ref), (1, qkv1_ref)):
            for j in range(gw // MXU_DIM):
                _store_deinterleaved(proj(g * gw + j * MXU_DIM), out_ref, ATTN_PATTERNS[g][1], t,
                                     j * MXU_DIM, stage_ref, mid_ref, piece % STAGE_SLOTS)
                piece += 1

        for j in range(gw // MXU_DIM):
            qkv0_ref[0, 0, rows, j * MXU_DIM:(j + 1) * MXU_DIM] = proj(j * MXU_DIM).astype(BF16)
        for j in range(FOURIER_WIDTH // MXU_DIM):
            u_ref[0, rows, j * MXU_DIM:(j + 1) * MXU_DIM] = (
                proj(N_GROUPS * gw + j * MXU_DIM).astype(BF16))


def _in_proj(h, w, b, *, batch, seq, tm=ROW_TILE):
    n, d = h.shape
    width = w.shape[1]
    gw = 3 * GROUP_WIDTH
    tiles_per_seq = seq // tm
    d1, d2 = ATTN_PATTERNS[1][1], ATTN_PATTERNS[2][1]
    out_shape = (
        jax.ShapeDtypeStruct((batch, 1, seq, gw), BF16),
        jax.ShapeDtypeStruct((batch, d1, seq // d1, gw), BF16),
        jax.ShapeDtypeStruct((batch, d2, seq // d2, gw), BF16),
        jax.ShapeDtypeStruct((batch, seq, FOURIER_WIDTH), BF16),
    )

    def bt(i):
        return i // tiles_per_seq, i % tiles_per_seq

    out_specs = (
        pl.BlockSpec((1, 1, tm, gw), lambda i: (bt(i)[0], 0, bt(i)[1], 0)),
        pl.BlockSpec((1, d1, tm // d1, gw), lambda i: (bt(i)[0], 0, bt(i)[1], 0)),
        pl.BlockSpec((1, d2, tm // d2, gw), lambda i: (bt(i)[0], 0, bt(i)[1], 0)),
        pl.BlockSpec((1, tm, FOURIER_WIDTH), lambda i: (bt(i)[0], bt(i)[1], 0)),
    )
    return pl.pallas_call(
        _in_proj_kernel,
        out_shape=out_shape,
        grid=(n // tm,),
        in_specs=[pl.BlockSpec((tm, d), lambda i: (i, 0)), _resident((d, width)),
                  _resident((1, width))],
        out_specs=out_specs,
        scratch_shapes=[pltpu.VMEM((STAGE_SLOTS, MXU_DIM // LANES, SUB_ROWS, LANES), F32)] * 2,
        compiler_params=_compiler_params(("parallel",)),
        name="in_proj",
    )(h, w, b)


def _t5_bucket_np(rel):
    half = NUM_BUCKETS // 2
    ret = (rel > 0).astype(np.int64) * half
    n = np.abs(rel)
    nf = np.maximum(n, 1).astype(np.float64)
    large = MAX_EXACT + (np.log(nf / MAX_EXACT) / math.log(MAX_DISTANCE / MAX_EXACT)
                         * (half - MAX_EXACT)).astype(np.int64)
    large = np.minimum(large, half - 1)
    return ret + np.where(n < MAX_EXACT, n, large)


def _attention_bias(rel_bias):
    offs = np.arange(-HALF_WIN, HALF_WIN + 1)
    band = []
    for g, (_, dil) in enumerate(ATTN_PATTERNS):
        buckets = _t5_bucket_np(offs * dil)
        heads = rel_bias[:, g * HEADS_PER_GROUP:(g + 1) * HEADS_PER_GROUP]
        band.append(jnp.take(heads, jnp.asarray(buckets, jnp.int32), axis=0).T)
    band = jnp.stack(band).astype(F32) * LOG2_E
    period = Q_TILE + K_TILE
    base = jnp.concatenate(
        [band, jnp.full(band.shape[:2] + (period - band.shape[-1],), NEG_INF, F32)], axis=-1)
    variants = []
    for v in range(3):
        first_row = jnp.roll(base, HALF_WIN * (v - 1), axis=-1)
        seq = jnp.tile(first_row, (1, 1, Q_TILE))[..., :Q_TILE * (period - 1)]
        variants.append(seq.reshape(band.shape[:2] + (Q_TILE, period - 1))[..., :K_TILE])
    return jnp.stack(variants, axis=2)


def _attn_kernel(q0, k0, v0, q1, k1, v1, q2, k2, v2, bias_ref, o_ref, out_ref, lse_ref, *, chunk):
    c = pl.program_id(2)
    for g, (q_ref, k_ref, v_ref) in enumerate(((q0, k0, v0), (q1, k1, v1), (q2, k2, v2))):
        dil = ATTN_PATTERNS[g][1]
        sub_len = k_ref.shape[2]
        rows_per_sub = chunk // dil
        tiles_per_sub = rows_per_sub // Q_TILE

        def tile(i, carry, g=g, dil=dil, sub_len=sub_len, rows_per_sub=rows_per_sub,
                 tiles_per_sub=tiles_per_sub, q_ref=q_ref, k_ref=k_ref, v_ref=v_ref):
            r = i // tiles_per_sub
            t = i % tiles_per_sub
            q_start = c * rows_per_sub + t * Q_TILE
            k_start = jnp.clip(q_start - HALF_WIN, 0, sub_len - K_TILE)
            variant = (q_start - k_start) // HALF_WIN
            k_start = pl.multiple_of(k_start, HALF_WIN)
            q = q_ref[0, r, pl.ds(pl.multiple_of(t * Q_TILE, Q_TILE), Q_TILE), :]
            k = k_ref[0, r, pl.ds(k_start, K_TILE), :]
            v = v_ref[0, r, pl.ds(k_start, K_TILE), :]
            s = lax.dot_general(q, k, (((1,), (1,)), ((), ())), preferred_element_type=F32)
            s = s + bias_ref[g, 0, variant]
            m = jnp.max(s, axis=-1, keepdims=True)
            p = jnp.exp2(s - m)
            den = jnp.sum(p, axis=-1, keepdims=True)
            row0 = t * Q_TILE * dil + r
            rows = pl.ds(row0, Q_TILE) if dil == 1 else pl.ds(row0, Q_TILE, stride=dil)
            out_ref[g, rows, :] = (jnp.dot(p.astype(BF16), v, preferred_element_type=F32)
                                   * (1.0 / den))
            lse_ref[g, rows, :] = jnp.broadcast_to(m + jnp.log2(den), (Q_TILE, LANES))
            return carry

        lax.fori_loop(0, dil * tiles_per_sub, tile, 0, unroll=ATTN_UNROLL)

    blk = 256
    for i in range(chunk // blk):
        rows = slice(i * blk, (i + 1) * blk)
        lses = [lse_ref[g, rows, :] for g in range(N_GROUPS)]
        mx = jnp.maximum(jnp.maximum(lses[0], lses[1]), lses[2])
        ws = [jnp.exp2(l - mx) for l in lses]
        num = ws[0] * out_ref[0, rows, :] + ws[1] * out_ref[1, rows, :] + ws[2] * out_ref[2, rows, :]
        o_ref[0, rows, :] = (num / (ws[0] + ws[1] + ws[2])).astype(BF16)


def _attention(qkv, bias, *, batch, seq, chunk=2048):
    in_specs = []
    for g, arr in enumerate(qkv):
        dil = ATTN_PATTERNS[g][1]
        sub_len = seq // dil
        in_specs += [
            pl.BlockSpec((1, dil, chunk // dil, HEAD_DIM), lambda b, h, c: (b, 0, c, h)),
            pl.BlockSpec((1, dil, sub_len, HEAD_DIM),
                         lambda b, h, c: (b, 0, 0, HEADS_PER_GROUP + h)),
            pl.BlockSpec((1, dil, sub_len, HEAD_DIM),
                         lambda b, h, c: (b, 0, 0, 2 * HEADS_PER_GROUP + h)),
        ]
    in_specs.append(pl.BlockSpec((N_GROUPS, 1, 3, Q_TILE, K_TILE),
                                 lambda b, h, c: (0, h, 0, 0, 0)))
    args = [a for arr in qkv for a in (arr, arr, arr)] + [bias]
    return pl.pallas_call(
        functools.partial(_attn_kernel, chunk=chunk),
        out_shape=jax.ShapeDtypeStruct((batch, seq, GROUP_WIDTH), BF16),
        grid=(batch, HEADS_PER_GROUP, seq // chunk),
        in_specs=in_specs,
        out_specs=pl.BlockSpec((1, chunk, HEAD_DIM), lambda b, h, c: (b, c, h)),
        scratch_shapes=[pltpu.VMEM((N_GROUPS, chunk, LANES), F32)] * 2,
        compiler_params=_compiler_params(("parallel", "parallel", "arbitrary")),
        name="attention",
    )(*args)


def _fft_tables(seq):
    assert seq == FFT_N1 * FFT_N2
    k1 = np.arange(FFT_N1)[None, :, None]
    n1 = np.arange(FFT_N1)[None, None, :]
    n2 = np.arange(FFT_N2)[:, None, None]
    ang = 2.0 * np.pi * ((n2 * k1 + FFT_N2 * n1 * k1) % seq) / seq
    t1 = np.concatenate([np.cos(ang), -np.sin(ang)], axis=1)
    k2 = np.arange(FFT_N2)[:, None]
    m2 = np.arange(FFT_N2)[None, :]
    ang2 = 2.0 * np.pi * ((k2 * m2) % FFT_N2) / FFT_N2
    cr, ci = np.cos(ang2), -np.sin(ang2)
    t2 = np.block([[cr, -ci], [ci, cr]])
    cc = np.arange(FOURIER_GROUP_DIM)
    ang3 = 2.0 * np.pi * ((cc[:, None] * cc[None, :]) % FOURIER_GROUP_DIM) / FOURIER_GROUP_DIM
    norm = 1.0 / math.sqrt(seq * FOURIER_GROUP_DIM)
    t3 = np.concatenate([np.cos(ang3), np.sin(ang3)], axis=0) * norm
    return (jnp.asarray(t1, BF16), jnp.asarray(t2, BF16), jnp.asarray(t3, BF16))


def _fft_kernel(u_ref, t1_ref, t2_ref, t3_ref, o_ref, uf_ref, yr_ref, yi_ref, gi_ref):
    slabs = range(FFT_GROUPS_PER_STEP)
    gr_ref = uf_ref

    def lanes(h):
        return slice(h * LANES, (h + 1) * LANES)

    def widen(n1, carry):
        src = pl.multiple_of(n1 * FFT_N2, FFT_N2)
        dst = pl.multiple_of(n1 * FFT_IN_PITCH, 8)
        u = u_ref[0, pl.ds(src, FFT_N2), :].astype(F32)
        for h in slabs:
            uf_ref[h, pl.ds(dst, FFT_N2), :] = u[:, lanes(h)]
        return carry
    lax.fori_loop(0, FFT_N1, widen, 0, unroll=FFT_UNROLL)

    def stage1(n2, carry):
        rows = pl.ds(n2, FFT_N1, stride=FFT_IN_PITCH)
        x = jnp.concatenate([uf_ref[h, rows, :] for h in slabs], axis=1).astype(BF16)
        y = jnp.dot(t1_ref[n2], x, preferred_element_type=F32)
        dst = pl.multiple_of(n2 * FFT_MID_PITCH, 8)
        for h in slabs:
            yr_ref[h, pl.ds(dst, FFT_N1), :] = y[:FFT_N1, lanes(h)]
            yi_ref[h, pl.ds(dst, FFT_N1), :] = y[FFT_N1:, lanes(h)]
        return carry
    lax.fori_loop(0, FFT_N2, stage1, 0, unroll=FFT_UNROLL)

    def stage2(k1, carry):
        rows = pl.ds(k1, FFT_N2, stride=FFT_MID_PITCH)
        y = jnp.concatenate(
            [jnp.concatenate([yr_ref[h, rows, :], yi_ref[h, rows, :]], axis=0) for h in slabs],
            axis=1).astype(BF16)
        gk = jnp.dot(t2_ref[...], y, preferred_element_type=F32)
        for h in slabs:
            gr_ref[h, rows, :] = gk[:FFT_N2, lanes(h)]
            gi_ref[h, rows, :] = gk[FFT_N2:, lanes(h)]
        return carry
    lax.fori_loop(0, FFT_N1, stage2, 0, unroll=FFT_UNROLL)

    def stage3(k2, carry):
        src = pl.ds(pl.multiple_of(k2 * FFT_MID_PITCH, 8), FFT_N1)
        dst = pl.ds(pl.multiple_of(k2 * FFT_N1, FFT_N1), FFT_N1)
        for h in slabs:
            gk = jnp.concatenate([gr_ref[h, src, :], gi_ref[h, src, :]], axis=1).astype(BF16)
            out = jnp.dot(gk, t3_ref[...], preferred_element_type=F32)
            o_ref[0, dst, lanes(h)] = out.astype(BF16)
        return carry
    lax.fori_loop(0, FFT_N2, stage3, 0, unroll=FFT_UNROLL)


def _fft(u, *, batch, seq):
    t1, t2, t3 = _fft_tables(seq)
    width = FFT_GROUPS_PER_STEP * FOURIER_GROUP_DIM
    blk = pl.BlockSpec((1, seq, width), lambda b, g: (b, 0, g))
    return pl.pallas_call(
        _fft_kernel,
        out_shape=jax.ShapeDtypeStruct((batch, seq, FOURIER_WIDTH), BF16),
        grid=(batch, FOURIER_WIDTH // width),
        in_specs=[blk, _resident(t1.shape), _resident(t2.shape), _resident(t3.shape)],
        out_specs=blk,
        scratch_shapes=[pltpu.VMEM((FFT_GROUPS_PER_STEP, FFT_N1 * FFT_IN_PITCH, LANES), F32)]
        + [pltpu.VMEM((FFT_GROUPS_PER_STEP, FFT_N2 * FFT_MID_PITCH, LANES), F32)] * 3,
        compiler_params=_compiler_params(("parallel", "parallel")),
        name="fft",
    )(u, t1, t2, t3)


def _merge_ln_kernel(h_ref, a_ref, f_ref, wgate_ref, bgate_ref, wpa_ref, wpf_ref, wout_ref,
                     g_ref, b_ref, o_ref, *, alpha):
    d = h_ref.shape[1]
    for s in range(h_ref.shape[0] // SUB_ROWS):
        rows = slice(s * SUB_ROWS, (s + 1) * SUB_ROWS)
        h = h_ref[rows, :]
        hb = h.astype(BF16)
        a = a_ref[rows, :]
        f = f_ref[rows, :]
        y = jnp.zeros(h.shape, F32)
        for c in range(d // MXU_DIM):
            ca = slice(c * MXU_DIM, (c + 1) * MXU_DIM)
            cf = slice(d + c * MXU_DIM, d + (c + 1) * MXU_DIM)
            ga = jax.nn.sigmoid(jnp.dot(hb, wgate_ref[:, ca], preferred_element_type=F32)
                                + bgate_ref[:, ca])
            gf = jax.nn.sigmoid(jnp.dot(hb, wgate_ref[:, cf], preferred_element_type=F32)
                                + bgate_ref[:, cf])
            pa = jnp.dot(a, wpa_ref[:, ca], preferred_element_type=F32)
            pf = jnp.dot(f, wpf_ref[:, ca], preferred_element_type=F32)
            merged = (ga * pa + gf * pf).astype(BF16)
            y = y + jnp.dot(merged, wout_ref[ca, :], preferred_element_type=F32)
        o_ref[rows, :] = _layer_norm(alpha * h + y, g_ref[...], b_ref[...])


def _merge_ln(h, attn, four, w_gate, b_gate, w_pa, w_pf, w_out, ln_g, ln_b, *, alpha,
              tm=ROW_TILE):
    n, d = h.shape
    row = pl.BlockSpec((tm, d), lambda i: (i, 0))
    half = pl.BlockSpec((tm, attn.shape[1]), lambda i: (i, 0))
    return pl.pallas_call(
        functools.partial(_merge_ln_kernel, alpha=alpha),
        out_shape=jax.ShapeDtypeStruct((n, d), F32),
        grid=(n // tm,),
        in_specs=[row, half, half, _resident(w_gate.shape), _resident(b_gate.shape),
                  _resident(w_pa.shape), _resident(w_pf.shape), _resident(w_out.shape),
                  _resident((1, d)), _resident((1, d))],
        out_specs=row,
        compiler_params=_compiler_params(("parallel",)),
        name="merge_ln",
    )(h, attn, four, w_gate, b_gate, w_pa, w_pf, w_out, ln_g.reshape(1, d), ln_b.reshape(1, d))


def _mixer_ln(h, w_in, b_in, rel_bias, w_proj_attn, w_proj_fourier, w_out, ln_g, ln_b,
              *, alpha, batch, seq):
    d = h.shape[1]
    a = ATTN_QKV_WIDTH
    gate0 = 3 * a + FOURIER_WIDTH
    pieces = [(part * a + g * GROUP_WIDTH, HEAD_DIM ** -0.5 * LOG2_E if part == 0 else 1.0)
              for g in range(N_GROUPS) for part in range(3)]
    w_qkvu = jnp.concatenate([w_in[:, s:s + GROUP_WIDTH] * f for s, f in pieces]
                             + [w_in[:, 3 * a:gate0]], axis=1).astype(BF16)
    b_qkvu = jnp.concatenate([b_in[s:s + GROUP_WIDTH] * f for s, f in pieces]
                             + [b_in[3 * a:gate0]]).reshape(1, -1)
    w_gate = w_in[:, gate0:].astype(BF16)
    b_gate = b_in[gate0:].reshape(1, -1)

    qkv0, qkv1, qkv2, u = _in_proj(h, w_qkvu, b_qkvu, batch=batch, seq=seq)
    attn = _attention((qkv0, qkv1, qkv2), _attention_bias(rel_bias), batch=batch, seq=seq)
    four = _fft(u, batch=batch, seq=seq)
    return _merge_ln(h, attn.reshape(batch * seq, -1), four.reshape(batch * seq, -1),
                     w_gate, b_gate, w_proj_attn.astype(BF16), w_proj_fourier.astype(BF16),
                     w_out.astype(BF16), ln_g, ln_b, alpha=alpha)


def kernel(x, ln1_g, ln1_b, ffn1_w_gate, ffn1_w_up, ffn1_w_down, w_in, b_in, rel_bias,
           w_proj_attn, w_proj_fourier, w_out, ln2_g, ln2_b, ffn2_w_gate, ffn2_w_up,
           ffn2_w_down, ln3_g, ln3_b):
    batch, seq, d = x.shape
    depth = ln1_g.shape[0]
    alpha = (2 * depth) ** 0.25
    h = x.reshape(batch * seq, d)
    for l in range(depth):
        h = _ffn_ln(h, ffn1_w_gate[l], ffn1_w_up[l], ffn1_w_down[l], ln1_g[l], ln1_b[l],
                    alpha=alpha)
        h = _mixer_ln(h, w_in[l], b_in[l], rel_bias, w_proj_attn[l], w_proj_fourier[l],
                      w_out[l], ln2_g[l], ln2_b[l], alpha=alpha, batch=batch, seq=seq)
        h = _ffn_ln(h, ffn2_w_gate[l], ffn2_w_up[l], ffn2_w_down[l], ln3_g[l], ln3_b[l],
                    alpha=alpha)
    return h.reshape(batch, seq, d)
```

```python
import functools
import math

import numpy as np
import jax
import jax.numpy as jnp
from jax import lax
from jax.experimental import pallas as pl
from jax.experimental.pallas import tpu as pltpu

F32 = jnp.float32
BF16 = jnp.bfloat16

HEAD_DIM = 128
HEADS_PER_GROUP = 4
ATTN_PATTERNS = ((128, 1), (512, 4), (2048, 16))
N_GROUPS = len(ATTN_PATTERNS)
GROUP_WIDTH = HEADS_PER_GROUP * HEAD_DIM
ATTN_QKV_WIDTH = N_GROUPS * GROUP_WIDTH
FOURIER_GROUPS = 4
FOURIER_GROUP_DIM = 128
FOURIER_WIDTH = FOURIER_GROUPS * FOURIER_GROUP_DIM
NUM_BUCKETS = 32
MAX_EXACT = 8
MAX_DISTANCE = 1024
NEG_INF = -1e30
LN_EPS = 1e-5
LOG2_E = math.log2(math.e)

LANES = 128
MXU_DIM = 256
V7X_VMEM_BYTES = 64 * 1024 * 1024
VMEM_LIMIT_BYTES = V7X_VMEM_BYTES * 7 // 8

ROW_TILE = 1024
WIDE_ROW_TILE = 2048
SUB_ROWS = 512
SINGLE_OP_STRIDE = 4
STAGE_SLOTS = 3

HALF_WIN = 64
Q_TILE = 128
K_TILE = Q_TILE + 2 * HALF_WIN
ATTN_UNROLL = 16
FFT_UNROLL = 16

FFT_N1 = 128
FFT_N2 = 64
FFT_GROUPS_PER_STEP = 1
FFT_IN_PITCH = FFT_N2 + 8
FFT_MID_PITCH = FFT_N1 + 8


def _compiler_params(semantics):
    return pltpu.CompilerParams(dimension_semantics=semantics,
                                vmem_limit_bytes=VMEM_LIMIT_BYTES)


def _resident(shape):
    zeros = (0,) * len(shape)
    return pl.BlockSpec(shape, lambda *_: zeros, pipeline_mode=pl.Buffered(1))


def _layer_norm(z, g, b):
    mu = jnp.mean(z, axis=-1, keepdims=True)
    zc = z - mu
    var = jnp.mean(zc * zc, axis=-1, keepdims=True)
    return zc * lax.rsqrt(var + LN_EPS) * g + b


def _cast_kernel(*refs):
    n = len(refs) // 2
    for src, dst in zip(refs[:n], refs[n:]):
        dst[...] = src[...].astype(dst.dtype)


def _cast_bf16(arrays, *, row_blocks=4):
    rows, cols = arrays[0].shape
    block_rows = rows // row_blocks
    assert block_rows * row_blocks == rows and block_rows % 16 == 0
    spec = pl.BlockSpec((block_rows, cols), lambda i: (i, 0))
    return pl.pallas_call(
        _cast_kernel,
        out_shape=tuple(jax.ShapeDtypeStruct(a.shape, BF16) for a in arrays),
        grid=(row_blocks,),
        in_specs=[spec] * len(arrays),
        out_specs=tuple([spec] * len(arrays)),
        compiler_params=_compiler_params(("parallel",)),
        name="cast_bf16",
    )(*arrays)


def _ffn_ln_kernel(x_ref, wg_ref, wu_ref, wd_ref, g_ref, b_ref, o_ref, *, alpha, f_chunk):
    f = wd_ref.shape[0]
    for s in range(x_ref.shape[0] // SUB_ROWS):
        rows = slice(s * SUB_ROWS, (s + 1) * SUB_ROWS)
        x = x_ref[rows, :]
        xb = x.astype(BF16)
        y = jnp.zeros(x.shape, F32)
        for c0 in range(0, f, f_chunk):
            sl = slice(c0, min(c0 + f_chunk, f))
            gate = jnp.dot(xb, wg_ref[:, sl], preferred_element_type=F32)
            up = jnp.dot(xb, wu_ref[:, sl], preferred_element_type=F32)
            act = (gate * jax.nn.sigmoid(gate) * up).astype(BF16)
            y = y + jnp.dot(act, wd_ref[sl, :], preferred_element_type=F32)
        o_ref[rows, :] = _layer_norm(alpha * x + 0.5 * y, g_ref[...], b_ref[...])


def _ffn_ln(x, w_gate, w_up, w_down, ln_g, ln_b, *, alpha, tm=WIDE_ROW_TILE, f_chunk=MXU_DIM):
    n, d = x.shape
    f = w_gate.shape[1]
    row = pl.BlockSpec((tm, d), lambda i: (i, 0))
    return pl.pallas_call(
        functools.partial(_ffn_ln_kernel, alpha=alpha, f_chunk=f_chunk),
        out_shape=jax.ShapeDtypeStruct((n, d), F32),
        grid=(n // tm,),
        in_specs=[row, _resident((d, f)), _resident((d, f)), _resident((f, d)),
                  _resident((1, d)), _resident((1, d))],
        out_specs=row,
        compiler_params=_compiler_params(("parallel",)),
        name="ffn_ln",
    )(x, w_gate, w_up, w_down, ln_g.reshape(1, d), ln_b.reshape(1, d))


def _store_deinterleaved(res, out_ref, dil, t, col0, stage_ref, mid_ref, slot):
    n = res.shape[0]
    n_sub = n // dil
    out_rows = slice(t * n_sub, (t + 1) * n_sub)
    inner = min(dil, SINGLE_OP_STRIDE)
    outer = dil // inner
    part = n // inner
    for h in range(res.shape[1] // LANES):
        cols = slice(col0 + h * LANES, col0 + (h + 1) * LANES)
        stage_ref[slot, h] = res[:, h * LANES:(h + 1) * LANES]
        if outer == 1:
            for r in range(dil):
                out_ref[0, r, out_rows, cols] = (
                    stage_ref[slot, h, pl.ds(r, n_sub, stride=dil), :].astype(BF16))
            continue
        for ra in range(inner):
            mid_ref[slot, h, ra * part:(ra + 1) * part] = (
                stage_ref[slot, h, pl.ds(ra, part, stride=inner), :])
        for ra in range(inner):
            for rb in range(outer):
                out_ref[0, ra + inner * rb, out_rows, cols] = (
                    mid_ref[slot, h, pl.ds(ra * part + rb, n_sub, stride=outer), :].astype(BF16))


def _in_proj_kernel(x_ref, w_ref, b_ref, qkv0_ref, qkv1_ref, qkv2_ref, u_ref, stage_ref, mid_ref):
    gw = 3 * GROUP_WIDTH
    piece = 0
    for t in range(x_ref.shape[0] // SUB_ROWS):
        rows = slice(t * SUB_ROWS, (t + 1) * SUB_ROWS)
        xb = x_ref[rows, :].astype(BF16)

        def proj(c0, xb=xb):
            return (jnp.dot(xb, w_ref[:, c0:c0 + MXU_DIM], preferred_element_type=F32)
                    + b_ref[:, c0:c0 + MXU_DIM])

        for g, out_ref in ((2, qkv2_ref), (1, qkv1_ref)):
            for j in range(gw // MXU_DIM):
                _store_deinterleaved(proj(g * gw + j * MXU_DIM), out_ref, ATTN_PATTERNS[g][1], t,
                                     j * MXU_DIM, stage_ref, mid_ref, piece % STAGE_SLOTS)
                piece += 1

        for j in range(gw // MXU_DIM):
            qkv0_ref[0, 0, rows, j * MXU_DIM:(j + 1) * MXU_DIM] = proj(j * MXU_DIM).astype(BF16)
        for j in range(FOURIER_WIDTH // MXU_DIM):
            u_ref[0, rows, j * MXU_DIM:(j + 1) * MXU_DIM] = (
                proj(N_GROUPS * gw + j * MXU_DIM).astype(BF16))


def _in_proj(h, w, b, *, batch, seq, tm=ROW_TILE):
    n, d = h.shape
    width = w.shape[1]
    gw = 3 * GROUP_WIDTH
    tiles_per_seq = seq // tm
    d1, d2 = ATTN_PATTERNS[1][1], ATTN_PATTERNS[2][1]
    out_shape = (
        jax.ShapeDtypeStruct((batch, 1, seq, gw), BF16),
        jax.ShapeDtypeStruct((batch, d1, seq // d1, gw), BF16),
        jax.ShapeDtypeStruct((batch, d2, seq // d2, gw), BF16),
        jax.ShapeDtypeStruct((batch, seq, FOURIER_WIDTH), BF16),
    )

    def bt(i):
        return i // tiles_per_seq, i % tiles_per_seq

    out_specs = (
        pl.BlockSpec((1, 1, tm, gw), lambda i: (bt(i)[0], 0, bt(i)[1], 0)),
        pl.BlockSpec((1, d1, tm // d1, gw), lambda i: (bt(i)[0], 0, bt(i)[1], 0)),
        pl.BlockSpec((1, d2, tm // d2, gw), lambda i: (bt(i)[0], 0, bt(i)[1], 0)),
        pl.BlockSpec((1, tm, FOURIER_WIDTH), lambda i: (bt(i)[0], bt(i)[1], 0)),
    )
    return pl.pallas_call(
        _in_proj_kernel,
        out_shape=out_shape,
        grid=(n // tm,),
        in_specs=[pl.BlockSpec((tm, d), lambda i: (i, 0)), _resident((d, width)),
                  _resident((1, width))],
        out_specs=out_specs,
        scratch_shapes=[pltpu.VMEM((STAGE_SLOTS, MXU_DIM // LANES, SUB_ROWS, LANES), F32)] * 2,
        compiler_params=_compiler_params(("parallel",)),
        name="in_proj",
    )(h, w, b)


def _t5_bucket_np(rel):
    half = NUM_BUCKETS // 2
    ret = (rel > 0).astype(np.int64) * half
    n = np.abs(rel)
    nf = np.maximum(n, 1).astype(np.float64)
    large = MAX_EXACT + (np.log(nf / MAX_EXACT) / math.log(MAX_DISTANCE / MAX_EXACT)
                         * (half - MAX_EXACT)).astype(np.int64)
    large = np.minimum(large, half - 1)
    return ret + np.where(n < MAX_EXACT, n, large)


def _attention_bias(rel_bias):
    offs = np.arange(-HALF_WIN, HALF_WIN + 1)
    band = []
    for g, (_, dil) in enumerate(ATTN_PATTERNS):
        buckets = _t5_bucket_np(offs * dil)
        heads = rel_bias[:, g * HEADS_PER_GROUP:(g + 1) * HEADS_PER_GROUP]
        band.append(jnp.take(heads, jnp.asarray(buckets, jnp.int32), axis=0).T)
    band = jnp.stack(band).astype(F32) * LOG2_E
    period = Q_TILE + K_TILE
    base = jnp.concatenate(
        [band, jnp.full(band.shape[:2] + (period - band.shape[-1],), NEG_INF, F32)], axis=-1)
    variants = []
    for v in range(3):
        first_row = jnp.roll(base, HALF_WIN * (v - 1), axis=-1)
        seq = jnp.tile(first_row, (1, 1, Q_TILE))[..., :Q_TILE * (period - 1)]
        variants.append(seq.reshape(band.shape[:2] + (Q_TILE, period - 1))[..., :K_TILE])
    return jnp.stack(variants, axis=2)


def _attn_kernel(q0, k0, v0, q1, k1, v1, q2, k2, v2, bias_ref, o_ref, out_ref, lse_ref, *, chunk):
    c = pl.program_id(2)
    for g, (q_ref, k_ref, v_ref) in enumerate(((q0, k0, v0), (q1, k1, v1), (q2, k2, v2))):
        dil = ATTN_PATTERNS[g][1]
        sub_len = k_ref.shape[2]
        rows_per_sub = chunk // dil
        tiles_per_sub = rows_per_sub // Q_TILE

        def tile(i, carry, g=g, dil=dil, sub_len=sub_len, rows_per_sub=rows_per_sub,
                 tiles_per_sub=tiles_per_sub, q_ref=q_ref, k_ref=k_ref, v_ref=v_ref):
            r = i // tiles_per_sub
            t = i % tiles_per_sub
            q_start = c * rows_per_sub + t * Q_TILE
            k_start = jnp.clip(q_start - HALF_WIN, 0, sub_len - K_TILE)
            variant = (q_start - k_start) // HALF_WIN
            k_start = pl.multiple_of(k_start, HALF_WIN)
            q = q_ref[0, r, pl.ds(pl.multiple_of(t * Q_TILE, Q_TILE), Q_TILE), :]
            k = k_ref[0, r, pl.ds(k_start, K_TILE), :]
            v = v_ref[0, r, pl.ds(k_start, K_TILE), :]
            s = lax.dot_general(q, k, (((1,), (1,)), ((), ())), preferred_element_type=F32)
            s = s + bias_ref[g, 0, variant]
            m = jnp.max(s, axis=-1, keepdims=True)
            p = jnp.exp2(s - m)
            den = jnp.sum(p, axis=-1, keepdims=True)
            row0 = t * Q_TILE * dil + r
            rows = pl.ds(row0, Q_TILE) if dil == 1 else pl.ds(row0, Q_TILE, stride=dil)
            out_ref[g, rows, :] = (jnp.dot(p.astype(BF16), v, preferred_element_type=F32)
                                   * (1.0 / den))
            lse_ref[g, rows, :] = jnp.broadcast_to(m + jnp.log2(den), (Q_TILE, LANES))
            return carry

        lax.fori_loop(0, dil * tiles_per_sub, tile, 0, unroll=ATTN_UNROLL)

    blk = 256
    for i in range(chunk // blk):
        rows = slice(i * blk, (i + 1) * blk)
        lses = [lse_ref[g, rows, :] for g in range(N_GROUPS)]
        mx = jnp.maximum(jnp.maximum(lses[0], lses[1]), lses[2])
        ws = [jnp.exp2(l - mx) for l in lses]
        num = ws[0] * out_ref[0, rows, :] + ws[1] * out_ref[1, rows, :] + ws[2] * out_ref[2, rows, :]
        o_ref[0, rows, :] = (num / (ws[0] + ws[1] + ws[2])).astype(BF16)


def _attention(qkv, bias, *, batch, seq, chunk=2048):
    in_specs = []
    for g, arr in enumerate(qkv):
        dil = ATTN_PATTERNS[g][1]
        sub_len = seq // dil
        in_specs += [
            pl.BlockSpec((1, dil, chunk // dil, HEAD_DIM), lambda b, h, c: (b, 0, c, h)),
            pl.BlockSpec((1, dil, sub_len, HEAD_DIM),
                         lambda b, h, c: (b, 0, 0, HEADS_PER_GROUP + h)),
            pl.BlockSpec((1, dil, sub_len, HEAD_DIM),
                         lambda b, h, c: (b, 0, 0, 2 * HEADS_PER_GROUP + h)),
        ]
    in_specs.append(pl.BlockSpec((N_GROUPS, 1, 3, Q_TILE, K_TILE),
                                 lambda b, h, c: (0, h, 0, 0, 0)))
    args = [a for arr in qkv for a in (arr, arr, arr)] + [bias]
    return pl.pallas_call(
        functools.partial(_attn_kernel, chunk=chunk),
        out_shape=jax.ShapeDtypeStruct((batch, seq, GROUP_WIDTH), BF16),
        grid=(batch, HEADS_PER_GROUP, seq // chunk),
        in_specs=in_specs,
        out_specs=pl.BlockSpec((1, chunk, HEAD_DIM), lambda b, h, c: (b, c, h)),
        scratch_shapes=[pltpu.VMEM((N_GROUPS, chunk, LANES), F32)] * 2,
        compiler_params=_compiler_params(("parallel", "parallel", "arbitrary")),
        name="attention",
    )(*args)


def _fft_tables(seq):
    assert seq == FFT_N1 * FFT_N2
    k1 = np.arange(FFT_N1)[None, :, None]
    n1 = np.arange(FFT_N1)[None, None, :]
    n2 = np.arange(FFT_N2)[:, None, None]
    ang = 2.0 * np.pi * ((n2 * k1 + FFT_N2 * n1 * k1) % seq) / seq
    t1 = np.concatenate([np.cos(ang), -np.sin(ang)], axis=1)
    k2 = np.arange(FFT_N2)[:, None]
    m2 = np.arange(FFT_N2)[None, :]
    ang2 = 2.0 * np.pi * ((k2 * m2) % FFT_N2) / FFT_N2
    cr, ci = np.cos(ang2), -np.sin(ang2)
    t2 = np.block([[cr, -ci], [ci, cr]])
    cc = np.arange(FOURIER_GROUP_DIM)
    ang3 = 2.0 * np.pi * ((cc[:, None] * cc[None, :]) % FOURIER_GROUP_DIM) / FOURIER_GROUP_DIM
    norm = 1.0 / math.sqrt(seq * FOURIER_GROUP_DIM)
    t3 = np.concatenate([np.cos(ang3), np.sin(ang3)], axis=0) * norm
    return (jnp.asarray(t1, BF16), jnp.asarray(t2, BF16), jnp.asarray(t3, BF16))


def _fft_kernel(u_ref, t1_ref, t2_ref, t3_ref, o_ref, uf_ref, yr_ref, yi_ref, gi_ref):
    slabs = range(FFT_GROUPS_PER_STEP)
    gr_ref = uf_ref

    def lanes(h):
        return slice(h * LANES, (h + 1) * LANES)

    def widen(n1, carry):
        src = pl.multiple_of(n1 * FFT_N2, FFT_N2)
        dst = pl.multiple_of(n1 * FFT_IN_PITCH, 8)
        u = u_ref[0, pl.ds(src, FFT_N2), :].astype(F32)
        for h in slabs:
            uf_ref[h, pl.ds(dst, FFT_N2), :] = u[:, lanes(h)]
        return carry
    lax.fori_loop(0, FFT_N1, widen, 0, unroll=FFT_UNROLL)

    def stage1(n2, carry):
        rows = pl.ds(n2, FFT_N1, stride=FFT_IN_PITCH)
        x = jnp.concatenate([uf_ref[h, rows, :] for h in slabs], axis=1).astype(BF16)
        y = jnp.dot(t1_ref[n2], x, preferred_element_type=F32)
        dst = pl.multiple_of(n2 * FFT_MID_PITCH, 8)
        for h in slabs:
            yr_ref[h, pl.ds(dst, FFT_N1), :] = y[:FFT_N1, lanes(h)]
            yi_ref[h, pl.ds(dst, FFT_N1), :] = y[FFT_N1:, lanes(h)]
        return carry
    lax.fori_loop(0, FFT_N2, stage1, 0, unroll=FFT_UNROLL)

    def stage2(k1, carry):
        rows = pl.ds(k1, FFT_N2, stride=FFT_MID_PITCH)
        y = jnp.concatenate(
            [jnp.concatenate([yr_ref[h, rows, :], yi_ref[h, rows, :]], axis=0) for h in slabs],
            axis=1).astype(BF16)
        gk = jnp.dot(t2_ref[...], y, preferred_element_type=F32)
        for h in slabs:
            gr_ref[h, rows, :] = gk[:FFT_N2, lanes(h)]
            gi_ref[h, rows, :] = gk[FFT_N2:, lanes(h)]
        return carry
    lax.fori_loop(0, FFT_N1, stage2, 0, unroll=FFT_UNROLL)

    def stage3(k2, carry):
        src = pl.ds(pl.multiple_of(k2 * FFT_MID_PITCH, 8), FFT_N1)
        dst = pl.ds(pl.multiple_of(k2 * FFT_N1, FFT_N1), FFT_N1)
        for h in slabs:
            gk = jnp.concatenate([gr_ref[h, src, :], gi_ref[h, src, :]], axis=1).astype(BF16)
            out = jnp.dot(gk, t3_ref[...], preferred_element_type=F32)
            o_ref[0, dst, lanes(h)] = out.astype(BF16)
        return carry
    lax.fori_loop(0, FFT_N2, stage3, 0, unroll=FFT_UNROLL)


def _fft(u, *, batch, seq):
    t1, t2, t3 = _fft_tables(seq)
    width = FFT_GROUPS_PER_STEP * FOURIER_GROUP_DIM
    blk = pl.BlockSpec((1, seq, width), lambda b, g: (b, 0, g))
    return pl.pallas_call(
        _fft_kernel,
        out_shape=jax.ShapeDtypeStruct((batch, seq, FOURIER_WIDTH), BF16),
        grid=(batch, FOURIER_WIDTH // width),
        in_specs=[blk, _resident(t1.shape), _resident(t2.shape), _resident(t3.shape)],
        out_specs=blk,
        scratch_shapes=[pltpu.VMEM((FFT_GROUPS_PER_STEP, FFT_N1 * FFT_IN_PITCH, LANES), F32)]
        + [pltpu.VMEM((FFT_GROUPS_PER_STEP, FFT_N2 * FFT_MID_PITCH, LANES), F32)] * 3,
        compiler_params=_compiler_params(("parallel", "parallel")),
        name="fft",
    )(u, t1, t2, t3)


def _merge_ln_kernel(h_ref, a_ref, f_ref, wgate_ref, bgate_ref, wpa_ref, wpf_ref, wout_ref,
                     g_ref, b_ref, o_ref, *, alpha):
    d = h_ref.shape[1]
    for s in range(h_ref.shape[0] // SUB_ROWS):
        rows = slice(s * SUB_ROWS, (s + 1) * SUB_ROWS)
        h = h_ref[rows, :]
        hb = h.astype(BF16)
        a = a_ref[rows, :]
        f = f_ref[rows, :]
        y = jnp.zeros(h.shape, F32)
        for c in range(d // MXU_DIM):
            ca = slice(c * MXU_DIM, (c + 1) * MXU_DIM)
            cf = slice(d + c * MXU_DIM, d + (c + 1) * MXU_DIM)
            ga = jax.nn.sigmoid(jnp.dot(hb, wgate_ref[:, ca], preferred_element_type=F32)
                                + bgate_ref[:, ca])
            gf = jax.nn.sigmoid(jnp.dot(hb, wgate_ref[:, cf], preferred_element_type=F32)
                                + bgate_ref[:, cf])
            pa = jnp.dot(a, wpa_ref[:, ca], preferred_element_type=F32)
            pf = jnp.dot(f, wpf_ref[:, ca], preferred_element_type=F32)
            merged = (ga * pa + gf * pf).astype(BF16)
            y = y + jnp.dot(merged, wout_ref[ca, :], preferred_element_type=F32)
        o_ref[rows, :] = _layer_norm(alpha * h + y, g_ref[...], b_ref[...])


def _merge_ln(h, attn, four, w_gate, b_gate, w_pa, w_pf, w_out, ln_g, ln_b, *, alpha,
              tm=WIDE_ROW_TILE):
    n, d = h.shape
    row = pl.BlockSpec((tm, d), lambda i: (i, 0))
    half = pl.BlockSpec((tm, attn.shape[1]), lambda i: (i, 0))
    return pl.pallas_call(
        functools.partial(_merge_ln_kernel, alpha=alpha),
        out_shape=jax.ShapeDtypeStruct((n, d), F32),
        grid=(n // tm,),
        in_specs=[row, half, half, _resident(w_gate.shape), _resident(b_gate.shape),
                  _resident(w_pa.shape), _resident(w_pf.shape), _resident(w_out.shape),
                  _resident((1, d)), _resident((1, d))],
        out_specs=row,
        compiler_params=_compiler_params(("parallel",)),
        name="merge_ln",
    )(h, attn, four, w_gate, b_gate, w_pa, w_pf, w_out, ln_g.reshape(1, d), ln_b.reshape(1, d))


def _mixer_ln(h, w_in, b_in, rel_bias, w_proj_attn, w_proj_fourier, w_out, ln_g, ln_b,
              *, alpha, batch, seq):
    d = h.shape[1]
    a = ATTN_QKV_WIDTH
    gate0 = 3 * a + FOURIER_WIDTH
    pieces = [(part * a + g * GROUP_WIDTH, HEAD_DIM ** -0.5 * LOG2_E if part == 0 else 1.0)
              for g in range(N_GROUPS) for part in range(3)]
    w_qkvu = jnp.concatenate([w_in[:, s:s + GROUP_WIDTH] * f for s, f in pieces]
                             + [w_in[:, 3 * a:gate0]], axis=1).astype(BF16)
    b_qkvu = jnp.concatenate([b_in[s:s + GROUP_WIDTH] * f for s, f in pieces]
                             + [b_in[3 * a:gate0]]).reshape(1, -1)
    w_gate = w_in[:, gate0:].astype(BF16)
    b_gate = b_in[gate0:].reshape(1, -1)

    qkv0, qkv1, qkv2, u = _in_proj(h, w_qkvu, b_qkvu, batch=batch, seq=seq)
    attn = _attention((qkv0, qkv1, qkv2), _attention_bias(rel_bias), batch=batch, seq=seq)
    four = _fft(u, batch=batch, seq=seq)
    return _merge_ln(h, attn.reshape(batch * seq, -1), four.reshape(batch * seq, -1),
                     w_gate, b_gate, w_proj_attn.astype(BF16), w_proj_fourier.astype(BF16),
                     w_out.astype(BF16), ln_g, ln_b, alpha=alpha)


def kernel(x, ln1_g, ln1_b, ffn1_w_gate, ffn1_w_up, ffn1_w_down, w_in, b_in, rel_bias,
           w_proj_attn, w_proj_fourier, w_out, ln2_g, ln2_b, ffn2_w_gate, ffn2_w_up,
           ffn2_w_down, ln3_g, ln3_b):
    batch, seq, d = x.shape
    depth = ln1_g.shape[0]
    alpha = (2 * depth) ** 0.25
    h = x.reshape(batch * seq, d)
    for l in range(depth):
        wg1, wu1, wg2, wu2 = _cast_bf16(
            (ffn1_w_gate[l], ffn1_w_up[l], ffn2_w_gate[l], ffn2_w_up[l]))
        wd1, wd2 = _cast_bf16((ffn1_w_down[l], ffn2_w_down[l]))
        h = _ffn_ln(h, wg1, wu1, wd1, ln1_g[l], ln1_b[l], alpha=alpha)
        h = _mixer_ln(h, w_in[l], b_in[l], rel_bias, w_proj_attn[l], w_proj_fourier[l],
                      w_out[l], ln2_g[l], ln2_b[l], alpha=alpha, batch=batch, seq=seq)
        h = _ffn_ln(h, wg2, wu2, wd2, ln3_g[l], ln3_b[l], alpha=alpha)
    return h.reshape(batch, seq, d)
```

```python
import functools
import math

import numpy as np
import jax
import jax.numpy as jnp
from jax import lax
from jax.experimental import pallas as pl
from jax.experimental.pallas import tpu as pltpu

F32 = jnp.float32
BF16 = jnp.bfloat16

HEAD_DIM = 128
HEADS_PER_GROUP = 4
ATTN_PATTERNS = ((128, 1), (512, 4), (2048, 16))
N_GROUPS = len(ATTN_PATTERNS)
GROUP_WIDTH = HEADS_PER_GROUP * HEAD_DIM
ATTN_QKV_WIDTH = N_GROUPS * GROUP_WIDTH
FOURIER_GROUPS = 4
FOURIER_GROUP_DIM = 128
FOURIER_WIDTH = FOURIER_GROUPS * FOURIER_GROUP_DIM
NUM_BUCKETS = 32
MAX_EXACT = 8
MAX_DISTANCE = 1024
NEG_INF = -1e30
LN_EPS = 1e-5
LOG2_E = math.log2(math.e)

LANES = 128
MXU_DIM = 256
V7X_VMEM_BYTES = 64 * 1024 * 1024
VMEM_LIMIT_BYTES = V7X_VMEM_BYTES * 7 // 8

ROW_TILE = 1024
SUB_ROWS = 256
FFN_SUB_ROWS = 512
SINGLE_OP_STRIDE = 4
MERGE_CHUNK = 2 * MXU_DIM
STAGE_SLOTS = 3

HALF_WIN = 64
Q_TILE = 128
K_TILE = Q_TILE + 2 * HALF_WIN
BIAS_PERIOD = Q_TILE + K_TILE
ATTN_UNROLL = 16
FFT_UNROLL = 16

FFT_N1 = 128
FFT_N2 = 64
FFT_GROUPS_PER_STEP = 1
FFT_IN_PITCH = FFT_N2 + 8
FFT_MID_PITCH = FFT_N1 + 8


def _compiler_params(semantics):
    return pltpu.CompilerParams(dimension_semantics=semantics,
                                vmem_limit_bytes=VMEM_LIMIT_BYTES)


def _resident(shape):
    zeros = (0,) * len(shape)
    return pl.BlockSpec(shape, lambda *_: zeros, pipeline_mode=pl.Buffered(1))


def _layer_norm(z, g, b):
    mu = jnp.mean(z, axis=-1, keepdims=True)
    zc = z - mu
    var = jnp.mean(zc * zc, axis=-1, keepdims=True)
    return zc * lax.rsqrt(var + LN_EPS) * g + b


def _ffn_ln_kernel(x_ref, wg_ref, wu_ref, wd_ref, g_ref, b_ref, o_ref, *, alpha, f_chunk):
    f = wd_ref.shape[0]
    for s in range(x_ref.shape[0] // FFN_SUB_ROWS):
        rows = slice(s * FFN_SUB_ROWS, (s + 1) * FFN_SUB_ROWS)
        x = x_ref[rows, :]
        xb = x.astype(BF16)
        y = jnp.zeros(x.shape, F32)
        for c0 in range(0, f, f_chunk):
            sl = slice(c0, min(c0 + f_chunk, f))
            gate = jnp.dot(xb, wg_ref[:, sl], preferred_element_type=F32)
            up = jnp.dot(xb, wu_ref[:, sl], preferred_element_type=F32)
            act = (gate * jax.nn.sigmoid(gate) * up).astype(BF16)
            y = y + jnp.dot(act, wd_ref[sl, :], preferred_element_type=F32)
        o_ref[rows, :] = _layer_norm(alpha * x + 0.5 * y, g_ref[...], b_ref[...])


def _ffn_ln(x, w_gate, w_up, w_down, ln_g, ln_b, *, alpha, tm=ROW_TILE, f_chunk=MXU_DIM):
    n, d = x.shape
    f = w_gate.shape[1]
    row = pl.BlockSpec((tm, d), lambda i: (i, 0))
    return pl.pallas_call(
        functools.partial(_ffn_ln_kernel, alpha=alpha, f_chunk=f_chunk),
        out_shape=jax.ShapeDtypeStruct((n, d), F32),
        grid=(n // tm,),
        in_specs=[row, _resident((d, f)), _resident((d, f)), _resident((f, d)),
                  _resident((1, d)), _resident((1, d))],
        out_specs=row,
        compiler_params=_compiler_params(("parallel",)),
        name="ffn_ln",
    )(x, w_gate.astype(BF16), w_up.astype(BF16), w_down.astype(BF16),
      ln_g.reshape(1, d), ln_b.reshape(1, d))


def _store_deinterleaved(res, out_ref, dil, t, col0, stage_ref, mid_ref, slot):
    n = res.shape[0]
    n_sub = n // dil
    out_rows = slice(t * n_sub, (t + 1) * n_sub)
    inner = min(dil, SINGLE_OP_STRIDE)
    outer = dil // inner
    part = n // inner
    for h in range(res.shape[1] // LANES):
        cols = slice(col0 + h * LANES, col0 + (h + 1) * LANES)
        stage_ref[slot, h] = res[:, h * LANES:(h + 1) * LANES]
        if outer == 1:
            for r in range(dil):
                out_ref[0, r, out_rows, cols] = (
                    stage_ref[slot, h, pl.ds(r, n_sub, stride=dil), :].astype(BF16))
            continue
        for ra in range(inner):
            mid_ref[slot, h, ra * part:(ra + 1) * part] = (
                stage_ref[slot, h, pl.ds(ra, part, stride=inner), :])
        for ra in range(inner):
            for rb in range(outer):
                out_ref[0, ra + inner * rb, out_rows, cols] = (
                    mid_ref[slot, h, pl.ds(ra * part + rb, n_sub, stride=outer), :].astype(BF16))


def _in_proj_kernel(x_ref, w_ref, b_ref, qkv0_ref, qkv1_ref, qkv2_ref, u_ref, stage_ref, mid_ref):
    gw = 3 * GROUP_WIDTH
    piece = 0
    for t in range(x_ref.shape[0] // SUB_ROWS):
        rows = slice(t * SUB_ROWS, (t + 1) * SUB_ROWS)
        xb = x_ref[rows, :].astype(BF16)

        def proj(c0, xb=xb):
            return (jnp.dot(xb, w_ref[:, c0:c0 + MXU_DIM], preferred_element_type=F32)
                    + b_ref[:, c0:c0 + MXU_DIM])

        for g, out_ref in ((2, qkv2_ref), (1, qkv1_ref)):
            for j in range(gw // MXU_DIM):
                _store_deinterleaved(proj(g * gw + j * MXU_DIM), out_ref, ATTN_PATTERNS[g][1], t,
                                     j * MXU_DIM, stage_ref, mid_ref, piece % STAGE_SLOTS)
                piece += 1

        for j in range(gw // MXU_DIM):
            qkv0_ref[0, 0, rows, j * MXU_DIM:(j + 1) * MXU_DIM] = proj(j * MXU_DIM).astype(BF16)
        for j in range(FOURIER_WIDTH // MXU_DIM):
            u_ref[0, rows, j * MXU_DIM:(j + 1) * MXU_DIM] = (
                proj(N_GROUPS * gw + j * MXU_DIM).astype(BF16))


def _in_proj(h, w, b, *, batch, seq, tm=ROW_TILE):
    n, d = h.shape
    width = w.shape[1]
    gw = 3 * GROUP_WIDTH
    tiles_per_seq = seq // tm
    d1, d2 = ATTN_PATTERNS[1][1], ATTN_PATTERNS[2][1]
    out_shape = (
        jax.ShapeDtypeStruct((batch, 1, seq, gw), BF16),
        jax.ShapeDtypeStruct((batch, d1, seq // d1, gw), BF16),
        jax.ShapeDtypeStruct((batch, d2, seq // d2, gw), BF16),
        jax.ShapeDtypeStruct((batch, seq, FOURIER_WIDTH), BF16),
    )

    def bt(i):
        return i // tiles_per_seq, i % tiles_per_seq

    out_specs = (
        pl.BlockSpec((1, 1, tm, gw), lambda i: (bt(i)[0], 0, bt(i)[1], 0)),
        pl.BlockSpec((1, d1, tm // d1, gw), lambda i: (bt(i)[0], 0, bt(i)[1], 0)),
        pl.BlockSpec((1, d2, tm // d2, gw), lambda i: (bt(i)[0], 0, bt(i)[1], 0)),
        pl.BlockSpec((1, tm, FOURIER_WIDTH), lambda i: (bt(i)[0], bt(i)[1], 0)),
    )
    return pl.pallas_call(
        _in_proj_kernel,
        out_shape=out_shape,
        grid=(n // tm,),
        in_specs=[pl.BlockSpec((tm, d), lambda i: (i, 0)), _resident((d, width)),
                  _resident((1, width))],
        out_specs=out_specs,
        scratch_shapes=[pltpu.VMEM((STAGE_SLOTS, MXU_DIM // LANES, SUB_ROWS, LANES), F32)] * 2,
        compiler_params=_compiler_params(("parallel",)),
        name="in_proj",
    )(h, w, b)


def _t5_bucket_np(rel):
    half = NUM_BUCKETS // 2
    ret = (rel > 0).astype(np.int64) * half
    n = np.abs(rel)
    nf = np.maximum(n, 1).astype(np.float64)
    large = MAX_EXACT + (np.log(nf / MAX_EXACT) / math.log(MAX_DISTANCE / MAX_EXACT)
                         * (half - MAX_EXACT)).astype(np.int64)
    large = np.minimum(large, half - 1)
    return ret + np.where(n < MAX_EXACT, n, large)


def _attention_bias_rows(rel_bias):
    offs = np.arange(-HALF_WIN, HALF_WIN + 1)
    band = []
    for g, (_, dil) in enumerate(ATTN_PATTERNS):
        buckets = _t5_bucket_np(offs * dil)
        heads = rel_bias[:, g * HEADS_PER_GROUP:(g + 1) * HEADS_PER_GROUP]
        band.append(jnp.take(heads, jnp.asarray(buckets, jnp.int32), axis=0).T)
    band = jnp.stack(band).astype(F32) * LOG2_E
    base = jnp.concatenate(
        [band, jnp.full(band.shape[:2] + (BIAS_PERIOD - band.shape[-1],), NEG_INF, F32)], axis=-1)
    rows = [jnp.roll(base, HALF_WIN * (v - 1), axis=-1) for v in range(3)]
    return jnp.stack(rows, axis=2)[:, :, :, None, :]


def _attn_kernel(q0, k0, v0, q1, k1, v1, q2, k2, v2, rows_ref, o_ref, out_ref, lse_ref, bias_ref,
                 *, chunk):
    c = pl.program_id(2)

    @pl.when(c == 0)
    def _():
        for g in range(N_GROUPS):
            for v in range(3):
                row0 = jnp.broadcast_to(rows_ref[g, 0, v], (Q_TILE, BIAS_PERIOD))
                circulant = pltpu.roll(row0, 0, 1, stride=1, stride_axis=0)
                bias_ref[g, v] = circulant[:, :K_TILE]

    for g, (q_ref, k_ref, v_ref) in enumerate(((q0, k0, v0), (q1, k1, v1), (q2, k2, v2))):
        dil = ATTN_PATTERNS[g][1]
        sub_len = k_ref.shape[2]
        rows_per_sub = chunk // dil
        tiles_per_sub = rows_per_sub // Q_TILE

        def tile(i, carry, g=g, dil=dil, sub_len=sub_len, rows_per_sub=rows_per_sub,
                 tiles_per_sub=tiles_per_sub, q_ref=q_ref, k_ref=k_ref, v_ref=v_ref):
            r = i // tiles_per_sub
            t = i % tiles_per_sub
            q_start = c * rows_per_sub + t * Q_TILE
            k_start = jnp.clip(q_start - HALF_WIN, 0, sub_len - K_TILE)
            variant = (q_start - k_start) // HALF_WIN
            k_start = pl.multiple_of(k_start, HALF_WIN)
            q = q_ref[0, r, pl.ds(pl.multiple_of(t * Q_TILE, Q_TILE), Q_TILE), :]
            k = k_ref[0, r, pl.ds(k_start, K_TILE), :]
            v = v_ref[0, r, pl.ds(k_start, K_TILE), :]
            s = lax.dot_general(q, k, (((1,), (1,)), ((), ())), preferred_element_type=F32)
            s = s + bias_ref[g, variant]
            m = jnp.max(s, axis=-1, keepdims=True)
            p = jnp.exp2(s - m)
            den = jnp.sum(p, axis=-1, keepdims=True)
            row0 = t * Q_TILE * dil + r
            rows = pl.ds(row0, Q_TILE) if dil == 1 else pl.ds(row0, Q_TILE, stride=dil)
            out_ref[g, rows, :] = (jnp.dot(p.astype(BF16), v, preferred_element_type=F32)
                                   * (1.0 / den))
            lse_ref[g, rows, :] = jnp.broadcast_to(m + jnp.log2(den), (Q_TILE, LANES))
            return carry

        lax.fori_loop(0, dil * tiles_per_sub, tile, 0, unroll=ATTN_UNROLL)

    blk = 256
    for i in range(chunk // blk):
        rows = slice(i * blk, (i + 1) * blk)
        lses = [lse_ref[g, rows, :] for g in range(N_GROUPS)]
        mx = jnp.maximum(jnp.maximum(lses[0], lses[1]), lses[2])
        ws = [jnp.exp2(l - mx) for l in lses]
        num = ws[0] * out_ref[0, rows, :] + ws[1] * out_ref[1, rows, :] + ws[2] * out_ref[2, rows, :]
        o_ref[0, rows, :] = (num / (ws[0] + ws[1] + ws[2])).astype(BF16)


def _attention(qkv, bias_rows, *, batch, seq, chunk=2048):
    in_specs = []
    for g, arr in enumerate(qkv):
        dil = ATTN_PATTERNS[g][1]
        sub_len = seq // dil
        in_specs += [
            pl.BlockSpec((1, dil, chunk // dil, HEAD_DIM), lambda b, h, c: (b, 0, c, h)),
            pl.BlockSpec((1, dil, sub_len, HEAD_DIM),
                         lambda b, h, c: (b, 0, 0, HEADS_PER_GROUP + h)),
            pl.BlockSpec((1, dil, sub_len, HEAD_DIM),
                         lambda b, h, c: (b, 0, 0, 2 * HEADS_PER_GROUP + h)),
        ]
    in_specs.append(pl.BlockSpec((N_GROUPS, 1, 3, 1, BIAS_PERIOD),
                                 lambda b, h, c: (0, h, 0, 0, 0)))
    args = [a for arr in qkv for a in (arr, arr, arr)] + [bias_rows]
    return pl.pallas_call(
        functools.partial(_attn_kernel, chunk=chunk),
        out_shape=jax.ShapeDtypeStruct((batch, seq, GROUP_WIDTH), BF16),
        grid=(batch, HEADS_PER_GROUP, seq // chunk),
        in_specs=in_specs,
        out_specs=pl.BlockSpec((1, chunk, HEAD_DIM), lambda b, h, c: (b, c, h)),
        scratch_shapes=[pltpu.VMEM((N_GROUPS, chunk, LANES), F32)] * 2
        + [pltpu.VMEM((N_GROUPS, 3, Q_TILE, K_TILE), F32)],
        compiler_params=_compiler_params(("parallel", "parallel", "arbitrary")),
        name="attention",
    )(*args)


def _fft_tables(seq):
    assert seq == FFT_N1 * FFT_N2
    k1 = np.arange(FFT_N1)[None, :, None]
    n1 = np.arange(FFT_N1)[None, None, :]
    n2 = np.arange(FFT_N2)[:, None, None]
    ang = 2.0 * np.pi * ((n2 * k1 + FFT_N2 * n1 * k1) % seq) / seq
    t1 = np.concatenate([np.cos(ang), -np.sin(ang)], axis=1)
    k2 = np.arange(FFT_N2)[:, None]
    m2 = np.arange(FFT_N2)[None, :]
    ang2 = 2.0 * np.pi * ((k2 * m2) % FFT_N2) / FFT_N2
    cr, ci = np.cos(ang2), -np.sin(ang2)
    t2 = np.block([[cr, -ci], [ci, cr]])
    cc = np.arange(FOURIER_GROUP_DIM)
    ang3 = 2.0 * np.pi * ((cc[:, None] * cc[None, :]) % FOURIER_GROUP_DIM) / FOURIER_GROUP_DIM
    norm = 1.0 / math.sqrt(seq * FOURIER_GROUP_DIM)
    t3 = np.concatenate([np.cos(ang3), np.sin(ang3)], axis=0) * norm
    return (jnp.asarray(t1, BF16), jnp.asarray(t2, BF16), jnp.asarray(t3, BF16))


def _fft_kernel(u_ref, t1_ref, t2_ref, t3_ref, o_ref, uf_ref, yr_ref, yi_ref, gi_ref):
    slabs = range(FFT_GROUPS_PER_STEP)
    gr_ref = uf_ref

    def lanes(h):
        return slice(h * LANES, (h + 1) * LANES)

    def widen(n1, carry):
        src = pl.multiple_of(n1 * FFT_N2, FFT_N2)
        dst = pl.multiple_of(n1 * FFT_IN_PITCH, 8)
        u = u_ref[0, pl.ds(src, FFT_N2), :].astype(F32)
        for h in slabs:
            uf_ref[h, pl.ds(dst, FFT_N2), :] = u[:, lanes(h)]
        return carry
    lax.fori_loop(0, FFT_N1, widen, 0, unroll=FFT_UNROLL)

    def stage1(n2, carry):
        rows = pl.ds(n2, FFT_N1, stride=FFT_IN_PITCH)
        x = jnp.concatenate([uf_ref[h, rows, :] for h in slabs], axis=1).astype(BF16)
        y = jnp.dot(t1_ref[n2], x, preferred_element_type=F32)
        dst = pl.multiple_of(n2 * FFT_MID_PITCH, 8)
        for h in slabs:
            yr_ref[h, pl.ds(dst, FFT_N1), :] = y[:FFT_N1, lanes(h)]
            yi_ref[h, pl.ds(dst, FFT_N1), :] = y[FFT_N1:, lanes(h)]
        return carry
    lax.fori_loop(0, FFT_N2, stage1, 0, unroll=FFT_UNROLL)

    def stage2(k1, carry):
        rows = pl.ds(k1, FFT_N2, stride=FFT_MID_PITCH)
        y = jnp.concatenate(
            [jnp.concatenate([yr_ref[h, rows, :], yi_ref[h, rows, :]], axis=0) for h in slabs],
            axis=1).astype(BF16)
        gk = jnp.dot(t2_ref[...], y, preferred_element_type=F32)
        for h in slabs:
            gr_ref[h, rows, :] = gk[:FFT_N2, lanes(h)]
            gi_ref[h, rows, :] = gk[FFT_N2:, lanes(h)]
        return carry
    lax.fori_loop(0, FFT_N1, stage2, 0, unroll=FFT_UNROLL)

    def stage3(k2, carry):
        src = pl.ds(pl.multiple_of(k2 * FFT_MID_PITCH, 8), FFT_N1)
        dst = pl.ds(pl.multiple_of(k2 * FFT_N1, FFT_N1), FFT_N1)
        for h in slabs:
            gk = jnp.concatenate([gr_ref[h, src, :], gi_ref[h, src, :]], axis=1).astype(BF16)
            out = jnp.dot(gk, t3_ref[...], preferred_element_type=F32)
            o_ref[0, dst, lanes(h)] = out.astype(BF16)
        return carry
    lax.fori_loop(0, FFT_N2, stage3, 0, unroll=FFT_UNROLL)


def _fft(u, *, batch, seq):
    t1, t2, t3 = _fft_tables(seq)
    width = FFT_GROUPS_PER_STEP * FOURIER_GROUP_DIM
    blk = pl.BlockSpec((1, seq, width), lambda b, g: (b, 0, g))
    return pl.pallas_call(
        _fft_kernel,
        out_shape=jax.ShapeDtypeStruct((batch, seq, FOURIER_WIDTH), BF16),
        grid=(batch, FOURIER_WIDTH // width),
        in_specs=[blk, _resident(t1.shape), _resident(t2.shape), _resident(t3.shape)],
        out_specs=blk,
        scratch_shapes=[pltpu.VMEM((FFT_GROUPS_PER_STEP, FFT_N1 * FFT_IN_PITCH, LANES), F32)]
        + [pltpu.VMEM((FFT_GROUPS_PER_STEP, FFT_N2 * FFT_MID_PITCH, LANES), F32)] * 3,
        compiler_params=_compiler_params(("parallel", "parallel")),
        name="fft",
    )(u, t1, t2, t3)


def _merge_ln_kernel(h_ref, a_ref, f_ref, wgate_ref, bgate_ref, wpa_ref, wpf_ref, wout_ref,
                     g_ref, b_ref, o_ref, *, alpha):
    d = h_ref.shape[1]
    for s in range(h_ref.shape[0] // SUB_ROWS):
        rows = slice(s * SUB_ROWS, (s + 1) * SUB_ROWS)
        h = h_ref[rows, :]
        hb = h.astype(BF16)
        a = a_ref[rows, :]
        f = f_ref[rows, :]
        y = jnp.zeros(h.shape, F32)
        for c in range(d // MERGE_CHUNK):
            ca = slice(c * MERGE_CHUNK, (c + 1) * MERGE_CHUNK)
            cf = slice(d + c * MERGE_CHUNK, d + (c + 1) * MERGE_CHUNK)
            ga = jax.nn.sigmoid(jnp.dot(hb, wgate_ref[:, ca], preferred_element_type=F32)
                                + bgate_ref[:, ca])
            gf = jax.nn.sigmoid(jnp.dot(hb, wgate_ref[:, cf], preferred_element_type=F32)
                                + bgate_ref[:, cf])
            pa = jnp.dot(a, wpa_ref[:, ca], preferred_element_type=F32)
            pf = jnp.dot(f, wpf_ref[:, ca], preferred_element_type=F32)
            merged = (ga * pa + gf * pf).astype(BF16)
            y = y + jnp.dot(merged, wout_ref[ca, :], preferred_element_type=F32)
        o_ref[rows, :] = _layer_norm(alpha * h + y, g_ref[...], b_ref[...])


def _merge_ln(h, attn, four, w_gate, b_gate, w_pa, w_pf, w_out, ln_g, ln_b, *, alpha,
              tm=ROW_TILE):
    n, d = h.shape
    row = pl.BlockSpec((tm, d), lambda i: (i, 0))
    half = pl.BlockSpec((tm, attn.shape[1]), lambda i: (i, 0))
    return pl.pallas_call(
        functools.partial(_merge_ln_kernel, alpha=alpha),
        out_shape=jax.ShapeDtypeStruct((n, d), F32),
        grid=(n // tm,),
        in_specs=[row, half, half, _resident(w_gate.shape), _resident(b_gate.shape),
                  _resident(w_pa.shape), _resident(w_pf.shape), _resident(w_out.shape),
                  _resident((1, d)), _resident((1, d))],
        out_specs=row,
        compiler_params=_compiler_params(("parallel",)),
        name="merge_ln",
    )(h, attn, four, w_gate, b_gate, w_pa, w_pf, w_out, ln_g.reshape(1, d), ln_b.reshape(1, d))


def _mixer_ln(h, w_in, b_in, rel_bias, w_proj_attn, w_proj_fourier, w_out, ln_g, ln_b,
              *, alpha, batch, seq):
    d = h.shape[1]
    a = ATTN_QKV_WIDTH
    gate0 = 3 * a + FOURIER_WIDTH
    pieces = [(part * a + g * GROUP_WIDTH, HEAD_DIM ** -0.5 * LOG2_E if part == 0 else 1.0)
              for g in range(N_GROUPS) for part in range(3)]
    w_qkvu = jnp.concatenate([w_in[:, s:s + GROUP_WIDTH] * f for s, f in pieces]
                             + [w_in[:, 3 * a:gate0]], axis=1).astype(BF16)
    b_qkvu = jnp.concatenate([b_in[s:s + GROUP_WIDTH] * f for s, f in pieces]
                             + [b_in[3 * a:gate0]]).reshape(1, -1)
    w_gate = w_in[:, gate0:].astype(BF16)
    b_gate = b_in[gate0:].reshape(1, -1)

    qkv0, qkv1, qkv2, u = _in_proj(h, w_qkvu, b_qkvu, batch=batch, seq=seq)
    attn = _attention((qkv0, qkv1, qkv2), _attention_bias_rows(rel_bias), batch=batch, seq=seq)
    four = _fft(u, batch=batch, seq=seq)
    return _merge_ln(h, attn.reshape(batch * seq, -1), four.reshape(batch * seq, -1),
                     w_gate, b_gate, w_proj_attn.astype(BF16), w_proj_fourier.astype(BF16),
                     w_out.astype(BF16), ln_g, ln_b, alpha=alpha)


def kernel(x, ln1_g, ln1_b, ffn1_w_gate, ffn1_w_up, ffn1_w_down, w_in, b_in, rel_bias,
           w_proj_attn, w_proj_fourier, w_out, ln2_g, ln2_b, ffn2_w_gate, ffn2_w_up,
           ffn2_w_down, ln3_g, ln3_b):
    batch, seq, d = x.shape
    depth = ln1_g.shape[0]
    alpha = (2 * depth) ** 0.25
    h = x.reshape(batch * seq, d)
    for l in range(depth):
        h = _ffn_ln(h, ffn1_w_gate[l], ffn1_w_up[l], ffn1_w_down[l], ln1_g[l], ln1_b[l],
                    alpha=alpha)
        h = _mixer_ln(h, w_in[l], b_in[l], rel_bias, w_proj_attn[l], w_proj_fourier[l],
                      w_out[l], ln2_g[l], ln2_b[l], alpha=alpha, batch=batch, seq=seq)
        h = _ffn_ln(h, ffn2_w_gate[l], ffn2_w_up[l], ffn2_w_down[l], ln3_g[l], ln3_b[l],
                    alpha=alpha)
    return h.reshape(batch, seq, d)
```

```python
import functools
import math

import numpy as np
import jax
import jax.numpy as jnp
from jax import lax
from jax.experimental import pallas as pl
from jax.experimental.pallas import tpu as pltpu

F32 = jnp.float32
BF16 = jnp.bfloat16

HEAD_DIM = 128
HEADS_PER_GROUP = 4
ATTN_PATTERNS = ((128, 1), (512, 4), (2048, 16))
N_GROUPS = len(ATTN_PATTERNS)
GROUP_WIDTH = HEADS_PER_GROUP * HEAD_DIM
ATTN_QKV_WIDTH = N_GROUPS * GROUP_WIDTH
FOURIER_GROUPS = 4
FOURIER_GROUP_DIM = 128
FOURIER_WIDTH = FOURIER_GROUPS * FOURIER_GROUP_DIM
NUM_BUCKETS = 32
MAX_EXACT = 8
MAX_DISTANCE = 1024
NEG_INF = -1e30
LN_EPS = 1e-5
LOG2_E = math.log2(math.e)

LANES = 128
MXU_DIM = 256
V7X_VMEM_BYTES = 64 * 1024 * 1024
VMEM_LIMIT_BYTES = V7X_VMEM_BYTES * 7 // 8

ROW_TILE = 1024
SUB_ROWS = 512
MERGE_SUB_ROWS = 256
SINGLE_OP_STRIDE = 4
MERGE_CHUNK = 2 * MXU_DIM
STAGE_SLOTS = 3

HALF_WIN = 64
Q_TILE = 128
K_TILE = Q_TILE + 2 * HALF_WIN
BIAS_PERIOD = Q_TILE + K_TILE
ATTN_CHUNK = 2048
ATTN_UNROLL = 16
FFT_UNROLL = 16

FFT_N1 = 128
FFT_N2 = 64
FFT_GROUPS_PER_STEP = 1
FFT_IN_PITCH = FFT_N2 + 8
FFT_MID_PITCH = FFT_N1 + 8


def _compiler_params(semantics):
    return pltpu.CompilerParams(dimension_semantics=semantics,
                                vmem_limit_bytes=VMEM_LIMIT_BYTES)


def _resident(shape):
    zeros = (0,) * len(shape)
    return pl.BlockSpec(shape, lambda *_: zeros, pipeline_mode=pl.Buffered(1))


def _layer_norm(z, g, b):
    mu = jnp.mean(z, axis=-1, keepdims=True)
    zc = z - mu
    var = jnp.mean(zc * zc, axis=-1, keepdims=True)
    return zc * lax.rsqrt(var + LN_EPS) * g + b


def _ffn_ln_kernel(x_ref, wg_ref, wu_ref, wd_ref, g_ref, b_ref, o_ref, *, alpha, f_chunk):
    f = wd_ref.shape[0]
    for s in range(x_ref.shape[0] // SUB_ROWS):
        rows = slice(s * SUB_ROWS, (s + 1) * SUB_ROWS)
        x = x_ref[rows, :]
        xb = x.astype(BF16)
        y = jnp.zeros(x.shape, F32)
        for c0 in range(0, f, f_chunk):
            sl = slice(c0, min(c0 + f_chunk, f))
            gate = jnp.dot(xb, wg_ref[:, sl], preferred_element_type=F32)
            up = jnp.dot(xb, wu_ref[:, sl], preferred_element_type=F32)
            act = (gate * jax.nn.sigmoid(gate) * up).astype(BF16)
            y = y + jnp.dot(act, wd_ref[sl, :], preferred_element_type=F32)
        o_ref[rows, :] = _layer_norm(alpha * x + 0.5 * y, g_ref[...], b_ref[...])


def _ffn_ln(x, w_gate, w_up, w_down, ln_g, ln_b, *, alpha, tm=ROW_TILE, f_chunk=MXU_DIM):
    n, d = x.shape
    f = w_gate.shape[1]
    row = pl.BlockSpec((tm, d), lambda i: (i, 0))
    return pl.pallas_call(
        functools.partial(_ffn_ln_kernel, alpha=alpha, f_chunk=f_chunk),
        out_shape=jax.ShapeDtypeStruct((n, d), F32),
        grid=(n // tm,),
        in_specs=[row, _resident((d, f)), _resident((d, f)), _resident((f, d)),
                  _resident((1, d)), _resident((1, d))],
        out_specs=row,
        compiler_params=_compiler_params(("parallel",)),
        name="ffn_ln",
    )(x, w_gate.astype(BF16), w_up.astype(BF16), w_down.astype(BF16),
      ln_g.reshape(1, d), ln_b.reshape(1, d))


def _store_deinterleaved(res, out_ref, dil, t, col0, stage_ref, mid_ref, slot):
    n = res.shape[0]
    n_sub = n // dil
    out_rows = slice(t * n_sub, (t + 1) * n_sub)
    inner = min(dil, SINGLE_OP_STRIDE)
    outer = dil // inner
    part = n // inner
    for h in range(res.shape[1] // LANES):
        cols = slice(col0 + h * LANES, col0 + (h + 1) * LANES)
        stage_ref[slot, h] = res[:, h * LANES:(h + 1) * LANES]
        if outer == 1:
            for r in range(dil):
                out_ref[0, r, out_rows, cols] = (
                    stage_ref[slot, h, pl.ds(r, n_sub, stride=dil), :].astype(BF16))
            continue
        for ra in range(inner):
            mid_ref[slot, h, ra * part:(ra + 1) * part] = (
                stage_ref[slot, h, pl.ds(ra, part, stride=inner), :])
        for ra in range(inner):
            for rb in range(outer):
                out_ref[0, ra + inner * rb, out_rows, cols] = (
                    mid_ref[slot, h, pl.ds(ra * part + rb, n_sub, stride=outer), :].astype(BF16))


def _in_proj_kernel(x_ref, w_ref, b_ref, qkv0_ref, qkv1_ref, qkv2_ref, u_ref, stage_ref, mid_ref):
    gw = 3 * GROUP_WIDTH
    piece = 0
    for t in range(x_ref.shape[0] // SUB_ROWS):
        rows = slice(t * SUB_ROWS, (t + 1) * SUB_ROWS)
        xb = x_ref[rows, :].astype(BF16)

        def proj(c0, xb=xb):
            return (jnp.dot(xb, w_ref[:, c0:c0 + MXU_DIM], preferred_element_type=F32)
                    + b_ref[:, c0:c0 + MXU_DIM])

        for g, out_ref in ((2, qkv2_ref), (1, qkv1_ref)):
            for j in range(gw // MXU_DIM):
                _store_deinterleaved(proj(g * gw + j * MXU_DIM), out_ref, ATTN_PATTERNS[g][1], t,
                                     j * MXU_DIM, stage_ref, mid_ref, piece % STAGE_SLOTS)
                piece += 1

        for j in range(gw // MXU_DIM):
            qkv0_ref[0, 0, rows, j * MXU_DIM:(j + 1) * MXU_DIM] = proj(j * MXU_DIM).astype(BF16)
        for j in range(FOURIER_WIDTH // MXU_DIM):
            u_ref[0, rows, j * MXU_DIM:(j + 1) * MXU_DIM] = (
                proj(N_GROUPS * gw + j * MXU_DIM).astype(BF16))


def _in_proj(h, w, b, *, batch, seq, tm=ROW_TILE):
    n, d = h.shape
    width = w.shape[1]
    gw = 3 * GROUP_WIDTH
    tiles_per_seq = seq // tm
    d1, d2 = ATTN_PATTERNS[1][1], ATTN_PATTERNS[2][1]
    out_shape = (
        jax.ShapeDtypeStruct((batch, 1, seq, gw), BF16),
        jax.ShapeDtypeStruct((batch, d1, seq // d1, gw), BF16),
        jax.ShapeDtypeStruct((batch, d2, seq // d2, gw), BF16),
        jax.ShapeDtypeStruct((batch, seq, FOURIER_WIDTH), BF16),
    )

    def bt(i):
        return i // tiles_per_seq, i % tiles_per_seq

    out_specs = (
        pl.BlockSpec((1, 1, tm, gw), lambda i: (bt(i)[0], 0, bt(i)[1], 0)),
        pl.BlockSpec((1, d1, tm // d1, gw), lambda i: (bt(i)[0], 0, bt(i)[1], 0)),
        pl.BlockSpec((1, d2, tm // d2, gw), lambda i: (bt(i)[0], 0, bt(i)[1], 0)),
        pl.BlockSpec((1, tm, FOURIER_WIDTH), lambda i: (bt(i)[0], bt(i)[1], 0)),
    )
    return pl.pallas_call(
        _in_proj_kernel,
        out_shape=out_shape,
        grid=(n // tm,),
        in_specs=[pl.BlockSpec((tm, d), lambda i: (i, 0)), _resident((d, width)),
                  _resident((1, width))],
        out_specs=out_specs,
        scratch_shapes=[pltpu.VMEM((STAGE_SLOTS, MXU_DIM // LANES, SUB_ROWS, LANES), F32)] * 2,
        compiler_params=_compiler_params(("parallel",)),
        name="in_proj",
    )(h, w, b)


def _t5_bucket_np(rel):
    half = NUM_BUCKETS // 2
    ret = (rel > 0).astype(np.int64) * half
    n = np.abs(rel)
    nf = np.maximum(n, 1).astype(np.float64)
    large = MAX_EXACT + (np.log(nf / MAX_EXACT) / math.log(MAX_DISTANCE / MAX_EXACT)
                         * (half - MAX_EXACT)).astype(np.int64)
    large = np.minimum(large, half - 1)
    return ret + np.where(n < MAX_EXACT, n, large)


def _attention_bias_rows(rel_bias):
    offs = np.arange(-HALF_WIN, HALF_WIN + 1)
    band = []
    for g, (_, dil) in enumerate(ATTN_PATTERNS):
        buckets = _t5_bucket_np(offs * dil)
        heads = rel_bias[:, g * HEADS_PER_GROUP:(g + 1) * HEADS_PER_GROUP]
        band.append(jnp.take(heads, jnp.asarray(buckets, jnp.int32), axis=0).T)
    band = jnp.stack(band).astype(F32) * LOG2_E
    base = jnp.concatenate(
        [band, jnp.full(band.shape[:2] + (BIAS_PERIOD - band.shape[-1],), NEG_INF, F32)], axis=-1)
    rows = [jnp.roll(base, HALF_WIN * (v - 1), axis=-1) for v in range(3)]
    return jnp.stack(rows, axis=2)[:, :, :, None, :]


def _attn_kernel(q0, k0, v0, q1, k1, v1, q2, k2, v2, rows_ref, o_ref, out_ref, lse_ref, bias_ref,
                 *, chunk):
    c = pl.program_id(2)

    @pl.when(c == 0)
    def _():
        for g in range(N_GROUPS):
            for v in range(3):
                row0 = jnp.broadcast_to(rows_ref[g, 0, v], (Q_TILE, BIAS_PERIOD))
                circulant = pltpu.roll(row0, 0, 1, stride=1, stride_axis=0)
                bias_ref[g, v] = circulant[:, :K_TILE]

    for g, (q_ref, k_ref, v_ref) in enumerate(((q0, k0, v0), (q1, k1, v1), (q2, k2, v2))):
        dil = ATTN_PATTERNS[g][1]
        sub_len = k_ref.shape[2]
        rows_per_sub = chunk // dil
        tiles_per_sub = rows_per_sub // Q_TILE

        def tile(i, carry, g=g, dil=dil, sub_len=sub_len, rows_per_sub=rows_per_sub,
                 tiles_per_sub=tiles_per_sub, q_ref=q_ref, k_ref=k_ref, v_ref=v_ref):
            r = i // tiles_per_sub
            t = i % tiles_per_sub
            q_start = c * rows_per_sub + t * Q_TILE
            k_start = jnp.clip(q_start - HALF_WIN, 0, sub_len - K_TILE)
            variant = (q_start - k_start) // HALF_WIN
            k_start = pl.multiple_of(k_start, HALF_WIN)
            q = q_ref[0, r, pl.ds(pl.multiple_of(t * Q_TILE, Q_TILE), Q_TILE), :]
            k = k_ref[0, r, pl.ds(k_start, K_TILE), :]
            v = v_ref[0, r, pl.ds(k_start, K_TILE), :]
            s = lax.dot_general(q, k, (((1,), (1,)), ((), ())), preferred_element_type=F32)
            s = s + bias_ref[g, variant]
            m = jnp.max(s, axis=-1, keepdims=True)
            p = jnp.exp2(s - m)
            den = jnp.sum(p, axis=-1, keepdims=True)
            row0 = t * Q_TILE * dil + r
            rows = pl.ds(row0, Q_TILE) if dil == 1 else pl.ds(row0, Q_TILE, stride=dil)
            out_ref[g, rows, :] = (jnp.dot(p.astype(BF16), v, preferred_element_type=F32)
                                   * (1.0 / den))
            lse_ref[g, rows, :] = jnp.broadcast_to(m + jnp.log2(den), (Q_TILE, LANES))
            return carry

        lax.fori_loop(0, dil * tiles_per_sub, tile, 0, unroll=ATTN_UNROLL)

    blk = 256
    for i in range(chunk // blk):
        rows = slice(i * blk, (i + 1) * blk)
        lses = [lse_ref[g, rows, :] for g in range(N_GROUPS)]
        mx = jnp.maximum(jnp.maximum(lses[0], lses[1]), lses[2])
        ws = [jnp.exp2(l - mx) for l in lses]
        num = ws[0] * out_ref[0, rows, :] + ws[1] * out_ref[1, rows, :] + ws[2] * out_ref[2, rows, :]
        o_ref[0, rows, :] = (num / (ws[0] + ws[1] + ws[2])).astype(BF16)


def _attention(qkv, bias_rows, *, batch, seq, chunk=ATTN_CHUNK):
    in_specs = []
    for g, arr in enumerate(qkv):
        dil = ATTN_PATTERNS[g][1]
        sub_len = seq // dil
        in_specs += [
            pl.BlockSpec((1, dil, chunk // dil, HEAD_DIM), lambda b, h, c: (b, 0, c, h)),
            pl.BlockSpec((1, dil, sub_len, HEAD_DIM),
                         lambda b, h, c: (b, 0, 0, HEADS_PER_GROUP + h)),
            pl.BlockSpec((1, dil, sub_len, HEAD_DIM),
                         lambda b, h, c: (b, 0, 0, 2 * HEADS_PER_GROUP + h)),
        ]
    in_specs.append(pl.BlockSpec((N_GROUPS, 1, 3, 1, BIAS_PERIOD),
                                 lambda b, h, c: (0, h, 0, 0, 0)))
    args = [a for arr in qkv for a in (arr, arr, arr)] + [bias_rows]
    return pl.pallas_call(
        functools.partial(_attn_kernel, chunk=chunk),
        out_shape=jax.ShapeDtypeStruct((batch, seq, GROUP_WIDTH), BF16),
        grid=(batch, HEADS_PER_GROUP, seq // chunk),
        in_specs=in_specs,
        out_specs=pl.BlockSpec((1, chunk, HEAD_DIM), lambda b, h, c: (b, c, h)),
        scratch_shapes=[pltpu.VMEM((N_GROUPS, chunk, LANES), F32)] * 2
        + [pltpu.VMEM((N_GROUPS, 3, Q_TILE, K_TILE), F32)],
        compiler_params=_compiler_params(("parallel", "parallel", "arbitrary")),
        name="attention",
    )(*args)


def _fft_tables(seq):
    assert seq == FFT_N1 * FFT_N2
    k1 = np.arange(FFT_N1)[None, :, None]
    n1 = np.arange(FFT_N1)[None, None, :]
    n2 = np.arange(FFT_N2)[:, None, None]
    ang = 2.0 * np.pi * ((n2 * k1 + FFT_N2 * n1 * k1) % seq) / seq
    t1 = np.concatenate([np.cos(ang), -np.sin(ang)], axis=1)
    k2 = np.arange(FFT_N2)[:, None]
    m2 = np.arange(FFT_N2)[None, :]
    ang2 = 2.0 * np.pi * ((k2 * m2) % FFT_N2) / FFT_N2
    cr, ci = np.cos(ang2), -np.sin(ang2)
    t2 = np.block([[cr, -ci], [ci, cr]])
    cc = np.arange(FOURIER_GROUP_DIM)
    ang3 = 2.0 * np.pi * ((cc[:, None] * cc[None, :]) % FOURIER_GROUP_DIM) / FOURIER_GROUP_DIM
    norm = 1.0 / math.sqrt(seq * FOURIER_GROUP_DIM)
    t3 = np.concatenate([np.cos(ang3), np.sin(ang3)], axis=0) * norm
    return (jnp.asarray(t1, BF16), jnp.asarray(t2, BF16), jnp.asarray(t3, BF16))


def _fft_kernel(u_ref, t1_ref, t2_ref, t3_ref, o_ref, uf_ref, yr_ref, yi_ref, gi_ref):
    slabs = range(FFT_GROUPS_PER_STEP)
    gr_ref = uf_ref

    def lanes(h):
        return slice(h * LANES, (h + 1) * LANES)

    def widen(n1, carry):
        src = pl.multiple_of(n1 * FFT_N2, FFT_N2)
        dst = pl.multiple_of(n1 * FFT_IN_PITCH, 8)
        u = u_ref[0, pl.ds(src, FFT_N2), :].astype(F32)
        for h in slabs:
            uf_ref[h, pl.ds(dst, FFT_N2), :] = u[:, lanes(h)]
        return carry
    lax.fori_loop(0, FFT_N1, widen, 0, unroll=FFT_UNROLL)

    def stage1(n2, carry):
        rows = pl.ds(n2, FFT_N1, stride=FFT_IN_PITCH)
        x = jnp.concatenate([uf_ref[h, rows, :] for h in slabs], axis=1).astype(BF16)
        y = jnp.dot(t1_ref[n2], x, preferred_element_type=F32)
        dst = pl.multiple_of(n2 * FFT_MID_PITCH, 8)
        for h in slabs:
            yr_ref[h, pl.ds(dst, FFT_N1), :] = y[:FFT_N1, lanes(h)]
            yi_ref[h, pl.ds(dst, FFT_N1), :] = y[FFT_N1:, lanes(h)]
        return carry
    lax.fori_loop(0, FFT_N2, stage1, 0, unroll=FFT_UNROLL)

    def stage2(k1, carry):
        rows = pl.ds(k1, FFT_N2, stride=FFT_MID_PITCH)
        y = jnp.concatenate(
            [jnp.concatenate([yr_ref[h, rows, :], yi_ref[h, rows, :]], axis=0) for h in slabs],
            axis=1).astype(BF16)
        gk = jnp.dot(t2_ref[...], y, preferred_element_type=F32)
        for h in slabs:
            gr_ref[h, rows, :] = gk[:FFT_N2, lanes(h)]
            gi_ref[h, rows, :] = gk[FFT_N2:, lanes(h)]
        return carry
    lax.fori_loop(0, FFT_N1, stage2, 0, unroll=FFT_UNROLL)

    def stage3(k2, carry):
        src = pl.ds(pl.multiple_of(k2 * FFT_MID_PITCH, 8), FFT_N1)
        dst = pl.ds(pl.multiple_of(k2 * FFT_N1, FFT_N1), FFT_N1)
        for h in slabs:
            gk = jnp.concatenate([gr_ref[h, src, :], gi_ref[h, src, :]], axis=1).astype(BF16)
            out = jnp.dot(gk, t3_ref[...], preferred_element_type=F32)
            o_ref[0, dst, lanes(h)] = out.astype(BF16)
        return carry
    lax.fori_loop(0, FFT_N2, stage3, 0, unroll=FFT_UNROLL)


def _fft(u, *, batch, seq):
    t1, t2, t3 = _fft_tables(seq)
    width = FFT_GROUPS_PER_STEP * FOURIER_GROUP_DIM
    blk = pl.BlockSpec((1, seq, width), lambda b, g: (b, 0, g))
    return pl.pallas_call(
        _fft_kernel,
        out_shape=jax.ShapeDtypeStruct((batch, seq, FOURIER_WIDTH), BF16),
        grid=(batch, FOURIER_WIDTH // width),
        in_specs=[blk, _resident(t1.shape), _resident(t2.shape), _resident(t3.shape)],
        out_specs=blk,
        scratch_shapes=[pltpu.VMEM((FFT_GROUPS_PER_STEP, FFT_N1 * FFT_IN_PITCH, LANES), F32)]
        + [pltpu.VMEM((FFT_GROUPS_PER_STEP, FFT_N2 * FFT_MID_PITCH, LANES), F32)] * 3,
        compiler_params=_compiler_params(("parallel", "parallel")),
        name="fft",
    )(u, t1, t2, t3)


def _merge_ln_kernel(h_ref, a_ref, f_ref, wgate_ref, bgate_ref, wpa_ref, wpf_ref, wout_ref,
                     g_ref, b_ref, o_ref, *, alpha):
    d = h_ref.shape[1]
    for s in range(h_ref.shape[0] // MERGE_SUB_ROWS):
        rows = slice(s * MERGE_SUB_ROWS, (s + 1) * MERGE_SUB_ROWS)
        h = h_ref[rows, :]
        hb = h.astype(BF16)
        a = a_ref[rows, :]
        f = f_ref[rows, :]
        y = jnp.zeros(h.shape, F32)
        for c in range(d // MERGE_CHUNK):
            ca = slice(c * MERGE_CHUNK, (c + 1) * MERGE_CHUNK)
            cf = slice(d + c * MERGE_CHUNK, d + (c + 1) * MERGE_CHUNK)
            ga = jax.nn.sigmoid(jnp.dot(hb, wgate_ref[:, ca], preferred_element_type=F32)
                                + bgate_ref[:, ca])
            gf = jax.nn.sigmoid(jnp.dot(hb, wgate_ref[:, cf], preferred_element_type=F32)
                                + bgate_ref[:, cf])
            pa = jnp.dot(a, wpa_ref[:, ca], preferred_element_type=F32)
            pf = jnp.dot(f, wpf_ref[:, ca], preferred_element_type=F32)
            merged = (ga * pa + gf * pf).astype(BF16)
            y = y + jnp.dot(merged, wout_ref[ca, :], preferred_element_type=F32)
        o_ref[rows, :] = _layer_norm(alpha * h + y, g_ref[...], b_ref[...])


def _merge_ln(h, attn, four, w_gate, b_gate, w_pa, w_pf, w_out, ln_g, ln_b, *, alpha,
              tm=ROW_TILE):
    n, d = h.shape
    row = pl.BlockSpec((tm, d), lambda i: (i, 0))
    half = pl.BlockSpec((tm, attn.shape[1]), lambda i: (i, 0))
    return pl.pallas_call(
        functools.partial(_merge_ln_kernel, alpha=alpha),
        out_shape=jax.ShapeDtypeStruct((n, d), F32),
        grid=(n // tm,),
        in_specs=[row, half, half, _resident(w_gate.shape), _resident(b_gate.shape),
                  _resident(w_pa.shape), _resident(w_pf.shape), _resident(w_out.shape),
                  _resident((1, d)), _resident((1, d))],
        out_specs=row,
        compiler_params=_compiler_params(("parallel",)),
        name="merge_ln",
    )(h, attn, four, w_gate, b_gate, w_pa, w_pf, w_out, ln_g.reshape(1, d), ln_b.reshape(1, d))


def _mixer_ln(h, w_in, b_in, rel_bias, w_proj_attn, w_proj_fourier, w_out, ln_g, ln_b,
              *, alpha, batch, seq):
    d = h.shape[1]
    a = ATTN_QKV_WIDTH
    gate0 = 3 * a + FOURIER_WIDTH
    pieces = [(part * a + g * GROUP_WIDTH, HEAD_DIM ** -0.5 * LOG2_E if part == 0 else 1.0)
              for g in range(N_GROUPS) for part in range(3)]
    w_qkvu = jnp.concatenate([w_in[:, s:s + GROUP_WIDTH] * f for s, f in pieces]
                             + [w_in[:, 3 * a:gate0]], axis=1).astype(BF16)
    b_qkvu = jnp.concatenate([b_in[s:s + GROUP_WIDTH] * f for s, f in pieces]
                             + [b_in[3 * a:gate0]]).reshape(1, -1)
    w_gate = w_in[:, gate0:].astype(BF16)
    b_gate = b_in[gate0:].reshape(1, -1)

    qkv0, qkv1, qkv2, u = _in_proj(h, w_qkvu, b_qkvu, batch=batch, seq=seq)
    attn = _attention((qkv0, qkv1, qkv2), _attention_bias_rows(rel_bias), batch=batch, seq=seq)
    four = _fft(u, batch=batch, seq=seq)
    return _merge_ln(h, attn.reshape(batch * seq, -1), four.reshape(batch * seq, -1),
                     w_gate, b_gate, w_proj_attn.astype(BF16), w_proj_fourier.astype(BF16),
                     w_out.astype(BF16), ln_g, ln_b, alpha=alpha)


def kernel(x, ln1_g, ln1_b, ffn1_w_gate, ffn1_w_up, ffn1_w_down, w_in, b_in, rel_bias,
           w_proj_attn, w_proj_fourier, w_out, ln2_g, ln2_b, ffn2_w_gate, ffn2_w_up,
           ffn2_w_down, ln3_g, ln3_b):
    batch, seq, d = x.shape
    depth = ln1_g.shape[0]
    alpha = (2 * depth) ** 0.25
    h = x.reshape(batch * seq, d)
    for l in range(depth):
        h = _ffn_ln(h, ffn1_w_gate[l], ffn1_w_up[l], ffn1_w_down[l], ln1_g[l], ln1_b[l],
                    alpha=alpha)
        h = _mixer_ln(h, w_in[l], b_in[l], rel_bias, w_proj_attn[l], w_proj_fourier[l],
                      w_out[l], ln2_g[l], ln2_b[l], alpha=alpha, batch=batch, seq=seq)
        h = _ffn_ln(h, ffn2_w_gate[l], ffn2_w_up[l], ffn2_w_down[l], ln3_g[l], ln3_b[l],
                    alpha=alpha)
    return h.reshape(batch, seq, d)
```

```python
import functools
import math
from typing import NamedTuple

import numpy as np
import jax
import jax.numpy as jnp
from jax import lax
from jax.experimental import pallas as pl
from jax.experimental.pallas import tpu as pltpu

F32 = jnp.float32
BF16 = jnp.bfloat16

HEAD_DIM = 128
HEADS_PER_GROUP = 4
ATTN_PATTERNS = ((128, 1), (512, 4), (2048, 16))
N_GROUPS = len(ATTN_PATTERNS)
GROUP_WIDTH = HEADS_PER_GROUP * HEAD_DIM
ATTN_QKV_WIDTH = N_GROUPS * GROUP_WIDTH
FOURIER_GROUPS = 4
FOURIER_GROUP_DIM = 128
FOURIER_WIDTH = FOURIER_GROUPS * FOURIER_GROUP_DIM
NUM_BUCKETS = 32
MAX_EXACT = 8
MAX_DISTANCE = 1024
NEG_INF = -1e30
LN_EPS = 1e-5
LOG2_E = math.log2(math.e)

LANES = 128
BF16_SUBLANES = 16
MXU_DIM = 256
V7X_VMEM_BYTES = 64 * 1024 * 1024
VMEM_LIMIT_BYTES = V7X_VMEM_BYTES * 7 // 8

ROW_TILE = 1024
SUB_ROWS = 512
MERGE_SUB_ROWS = 256
SINGLE_OP_STRIDE = 4
MERGE_CHUNK = 2 * MXU_DIM
STAGE_SLOTS = 3

HALF_WIN = 64
Q_TILE = 128
K_TILE = Q_TILE + 2 * HALF_WIN
BIAS_PERIOD = Q_TILE + K_TILE
ATTN_CHUNK = 2048
ATTN_UNROLL = 16
FFT_UNROLL = 16

FFT_N1 = 128
FFT_N2 = 64
FFT_GROUPS_PER_STEP = 1
FFT_IN_PITCH = FFT_N2 + 8
FFT_MID_PITCH = FFT_N1 + 8


def _compiler_params(semantics):
    return pltpu.CompilerParams(dimension_semantics=semantics,
                                vmem_limit_bytes=VMEM_LIMIT_BYTES)


def _resident(shape):
    zeros = (0,) * len(shape)
    return pl.BlockSpec(shape, lambda *_: zeros, pipeline_mode=pl.Buffered(1))


def _layer_norm(z, g, b):
    mu = jnp.mean(z, axis=-1, keepdims=True)
    zc = z - mu
    var = jnp.mean(zc * zc, axis=-1, keepdims=True)
    return zc * lax.rsqrt(var + LN_EPS) * g + b


class CastJob(NamedTuple):
    src: jax.Array
    outputs: tuple


def _whole(w):
    return CastJob(w, (((0, w.shape[1], 1.0),),))


def _run_cast_jobs(job_outputs, src_refs, out_refs):
    out_refs = iter(out_refs)
    for outputs, src_ref in zip(job_outputs, src_refs):
        for pieces in outputs:
            out_ref = next(out_refs)
            dst = 0
            for col0, width, factor in pieces:
                blk = src_ref[:, col0:col0 + width]
                out_ref[:, dst:dst + width] = (blk if factor == 1.0 else blk * factor).astype(BF16)
                dst += width


def _ffn_ln_kernel(x_ref, wg_ref, wu_ref, wd_ref, g_ref, b_ref, *refs, alpha, f_chunk, jobs):
    o_ref = refs[len(jobs)]
    _run_cast_jobs(jobs, refs[:len(jobs)], refs[len(jobs) + 1:])
    f = wd_ref.shape[0]
    for s in range(x_ref.shape[0] // SUB_ROWS):
        rows = slice(s * SUB_ROWS, (s + 1) * SUB_ROWS)
        x = x_ref[rows, :]
        xb = x.astype(BF16)
        y = jnp.zeros(x.shape, F32)
        for c0 in range(0, f, f_chunk):
            sl = slice(c0, min(c0 + f_chunk, f))
            gate = jnp.dot(xb, wg_ref[:, sl], preferred_element_type=F32)
            up = jnp.dot(xb, wu_ref[:, sl], preferred_element_type=F32)
            act = (gate * jax.nn.sigmoid(gate) * up).astype(BF16)
            y = y + jnp.dot(act, wd_ref[sl, :], preferred_element_type=F32)
        o_ref[rows, :] = _layer_norm(alpha * x + 0.5 * y, g_ref[...], b_ref[...])


def _ffn_ln(x, w_gate, w_up, w_down, ln_g, ln_b, *, alpha, cast_jobs=(), tm=ROW_TILE,
            f_chunk=MXU_DIM):
    n, d = x.shape
    f = w_gate.shape[1]
    steps = n // tm
    row = pl.BlockSpec((tm, d), lambda i: (i, 0))

    def row_block(rows, cols):
        assert rows % steps == 0 and (rows // steps) % BF16_SUBLANES == 0
        return pl.BlockSpec((rows // steps, cols), lambda i: (i, 0))

    job_in_specs = [row_block(*job.src.shape) for job in cast_jobs]
    job_out_widths = [(job.src.shape[0], sum(width for _, width, _ in pieces))
                      for job in cast_jobs for pieces in job.outputs]
    outs = pl.pallas_call(
        functools.partial(_ffn_ln_kernel, alpha=alpha, f_chunk=f_chunk,
                          jobs=tuple(job.outputs for job in cast_jobs)),
        out_shape=(jax.ShapeDtypeStruct((n, d), F32),)
        + tuple(jax.ShapeDtypeStruct(shape, BF16) for shape in job_out_widths),
        grid=(steps,),
        in_specs=[row, _resident((d, f)), _resident((d, f)), _resident((f, d)),
                  _resident((1, d)), _resident((1, d))] + job_in_specs,
        out_specs=(row,) + tuple(row_block(*shape) for shape in job_out_widths),
        compiler_params=_compiler_params(("parallel",)),
        name="ffn_ln",
    )(x, w_gate, w_up, w_down, ln_g.reshape(1, d), ln_b.reshape(1, d),
      *[job.src for job in cast_jobs])
    return outs[0], outs[1:]


def _store_deinterleaved(res, out_ref, dil, t, col0, stage_ref, mid_ref, slot):
    n = res.shape[0]
    n_sub = n // dil
    out_rows = slice(t * n_sub, (t + 1) * n_sub)
    inner = min(dil, SINGLE_OP_STRIDE)
    outer = dil // inner
    part = n // inner
    for h in range(res.shape[1] // LANES):
        cols = slice(col0 + h * LANES, col0 + (h + 1) * LANES)
        stage_ref[slot, h] = res[:, h * LANES:(h + 1) * LANES]
        if outer == 1:
            for r in range(dil):
                out_ref[0, r, out_rows, cols] = (
                    stage_ref[slot, h, pl.ds(r, n_sub, stride=dil), :].astype(BF16))
            continue
        for ra in range(inner):
            mid_ref[slot, h, ra * part:(ra + 1) * part] = (
                stage_ref[slot, h, pl.ds(ra, part, stride=inner), :])
        for ra in range(inner):
            for rb in range(outer):
                out_ref[0, ra + inner * rb, out_rows, cols] = (
                    mid_ref[slot, h, pl.ds(ra * part + rb, n_sub, stride=outer), :].astype(BF16))


def _in_proj_kernel(x_ref, w_ref, b_ref, qkv0_ref, qkv1_ref, qkv2_ref, u_ref, stage_ref, mid_ref):
    gw = 3 * GROUP_WIDTH
    piece = 0
    for t in range(x_ref.shape[0] // SUB_ROWS):
        rows = slice(t * SUB_ROWS, (t + 1) * SUB_ROWS)
        xb = x_ref[rows, :].astype(BF16)

        def proj(c0, xb=xb):
            return (jnp.dot(xb, w_ref[:, c0:c0 + MXU_DIM], preferred_element_type=F32)
                    + b_ref[:, c0:c0 + MXU_DIM])

        for g, out_ref in ((2, qkv2_ref), (1, qkv1_ref)):
            for j in range(gw // MXU_DIM):
                _store_deinterleaved(proj(g * gw + j * MXU_DIM), out_ref, ATTN_PATTERNS[g][1], t,
                                     j * MXU_DIM, stage_ref, mid_ref, piece % STAGE_SLOTS)
                piece += 1

        for j in range(gw // MXU_DIM):
            qkv0_ref[0, 0, rows, j * MXU_DIM:(j + 1) * MXU_DIM] = proj(j * MXU_DIM).astype(BF16)
        for j in range(FOURIER_WIDTH // MXU_DIM):
            u_ref[0, rows, j * MXU_DIM:(j + 1) * MXU_DIM] = (
                proj(N_GROUPS * gw + j * MXU_DIM).astype(BF16))


def _in_proj(h, w, b, *, batch, seq, tm=ROW_TILE):
    n, d = h.shape
    width = w.shape[1]
    gw = 3 * GROUP_WIDTH
    tiles_per_seq = seq // tm
    d1, d2 = ATTN_PATTERNS[1][1], ATTN_PATTERNS[2][1]
    out_shape = (
        jax.ShapeDtypeStruct((batch, 1, seq, gw), BF16),
        jax.ShapeDtypeStruct((batch, d1, seq // d1, gw), BF16),
        jax.ShapeDtypeStruct((batch, d2, seq // d2, gw), BF16),
        jax.ShapeDtypeStruct((batch, seq, FOURIER_WIDTH), BF16),
    )

    def bt(i):
        return i // tiles_per_seq, i % tiles_per_seq

    out_specs = (
        pl.BlockSpec((1, 1, tm, gw), lambda i: (bt(i)[0], 0, bt(i)[1], 0)),
        pl.BlockSpec((1, d1, tm // d1, gw), lambda i: (bt(i)[0], 0, bt(i)[1], 0)),
        pl.BlockSpec((1, d2, tm // d2, gw), lambda i: (bt(i)[0], 0, bt(i)[1], 0)),
        pl.BlockSpec((1, tm, FOURIER_WIDTH), lambda i: (bt(i)[0], bt(i)[1], 0)),
    )
    return pl.pallas_call(
        _in_proj_kernel,
        out_shape=out_shape,
        grid=(n // tm,),
        in_specs=[pl.BlockSpec((tm, d), lambda i: (i, 0)), _resident((d, width)),
                  _resident((1, width))],
        out_specs=out_specs,
        scratch_shapes=[pltpu.VMEM((STAGE_SLOTS, MXU_DIM // LANES, SUB_ROWS, LANES), F32)] * 2,
        compiler_params=_compiler_params(("parallel",)),
        name="in_proj",
    )(h, w, b)


def _t5_bucket_np(rel):
    half = NUM_BUCKETS // 2
    ret = (rel > 0).astype(np.int64) * half
    n = np.abs(rel)
    nf = np.maximum(n, 1).astype(np.float64)
    large = MAX_EXACT + (np.log(nf / MAX_EXACT) / math.log(MAX_DISTANCE / MAX_EXACT)
                         * (half - MAX_EXACT)).astype(np.int64)
    large = np.minimum(large, half - 1)
    return ret + np.where(n < MAX_EXACT, n, large)


def _attention_bias_rows(rel_bias):
    offs = np.arange(-HALF_WIN, HALF_WIN + 1)
    band = []
    for g, (_, dil) in enumerate(ATTN_PATTERNS):
        buckets = _t5_bucket_np(offs * dil)
        heads = rel_bias[:, g * HEADS_PER_GROUP:(g + 1) * HEADS_PER_GROUP]
        band.append(jnp.take(heads, jnp.asarray(buckets, jnp.int32), axis=0).T)
    band = jnp.stack(band).astype(F32) * LOG2_E
    base = jnp.concatenate(
        [band, jnp.full(band.shape[:2] + (BIAS_PERIOD - band.shape[-1],), NEG_INF, F32)], axis=-1)
    rows = [jnp.roll(base, HALF_WIN * (v - 1), axis=-1) for v in range(3)]
    return jnp.stack(rows, axis=2)[:, :, :, None, :]


def _attn_kernel(q0, k0, v0, q1, k1, v1, q2, k2, v2, rows_ref, o_ref, out_ref, lse_ref, bias_ref,
                 *, chunk):
    c = pl.program_id(2)

    @pl.when(c == 0)
    def _():
        for g in range(N_GROUPS):
            for v in range(3):
                row0 = jnp.broadcast_to(rows_ref[g, 0, v], (Q_TILE, BIAS_PERIOD))
                circulant = pltpu.roll(row0, 0, 1, stride=1, stride_axis=0)
                bias_ref[g, v] = circulant[:, :K_TILE]

    for g, (q_ref, k_ref, v_ref) in enumerate(((q0, k0, v0), (q1, k1, v1), (q2, k2, v2))):
        dil = ATTN_PATTERNS[g][1]
        sub_len = k_ref.shape[2]
        rows_per_sub = chunk // dil
        tiles_per_sub = rows_per_sub // Q_TILE

        def tile(i, carry, g=g, dil=dil, sub_len=sub_len, rows_per_sub=rows_per_sub,
                 tiles_per_sub=tiles_per_sub, q_ref=q_ref, k_ref=k_ref, v_ref=v_ref):
            r = i // tiles_per_sub
            t = i % tiles_per_sub
            q_start = c * rows_per_sub + t * Q_TILE
            k_start = jnp.clip(q_start - HALF_WIN, 0, sub_len - K_TILE)
            variant = (q_start - k_start) // HALF_WIN
            k_start = pl.multiple_of(k_start, HALF_WIN)
            q = q_ref[0, r, pl.ds(pl.multiple_of(t * Q_TILE, Q_TILE), Q_TILE), :]
            k = k_ref[0, r, pl.ds(k_start, K_TILE), :]
            v = v_ref[0, r, pl.ds(k_start, K_TILE), :]
            s = lax.dot_general(q, k, (((1,), (1,)), ((), ())), preferred_element_type=F32)
            s = s + bias_ref[g, variant]
            m = jnp.max(s, axis=-1, keepdims=True)
            p = jnp.exp2(s - m)
            den = jnp.sum(p, axis=-1, keepdims=True)
            row0 = t * Q_TILE * dil + r
            rows = pl.ds(row0, Q_TILE) if dil == 1 else pl.ds(row0, Q_TILE, stride=dil)
            out_ref[g, rows, :] = (jnp.dot(p.astype(BF16), v, preferred_element_type=F32)
                                   * (1.0 / den))
            lse_ref[g, rows, :] = jnp.broadcast_to(m + jnp.log2(den), (Q_TILE, LANES))
            return carry

        lax.fori_loop(0, dil * tiles_per_sub, tile, 0, unroll=ATTN_UNROLL)

    blk = 256
    for i in range(chunk // blk):
        rows = slice(i * blk, (i + 1) * blk)
        lses = [lse_ref[g, rows, :] for g in range(N_GROUPS)]
        mx = jnp.maximum(jnp.maximum(lses[0], lses[1]), lses[2])
        ws = [jnp.exp2(l - mx) for l in lses]
        num = ws[0] * out_ref[0, rows, :] + ws[1] * out_ref[1, rows, :] + ws[2] * out_ref[2, rows, :]
        o_ref[0, rows, :] = (num / (ws[0] + ws[1] + ws[2])).astype(BF16)


def _attention(qkv, bias_rows, *, batch, seq, chunk=ATTN_CHUNK):
    in_specs = []
    for g, arr in enumerate(qkv):
        dil = ATTN_PATTERNS[g][1]
        sub_len = seq // dil
        in_specs += [
            pl.BlockSpec((1, dil, chunk // dil, HEAD_DIM), lambda b, h, c: (b, 0, c, h)),
            pl.BlockSpec((1, dil, sub_len, HEAD_DIM),
                         lambda b, h, c: (b, 0, 0, HEADS_PER_GROUP + h)),
            pl.BlockSpec((1, dil, sub_len, HEAD_DIM),
                         lambda b, h, c: (b, 0, 0, 2 * HEADS_PER_GROUP + h)),
        ]
    in_specs.append(pl.BlockSpec((N_GROUPS, 1, 3, 1, BIAS_PERIOD),
                                 lambda b, h, c: (0, h, 0, 0, 0)))
    args = [a for arr in qkv for a in (arr, arr, arr)] + [bias_rows]
    return pl.pallas_call(
        functools.partial(_attn_kernel, chunk=chunk),
        out_shape=jax.ShapeDtypeStruct((batch, seq, GROUP_WIDTH), BF16),
        grid=(batch, HEADS_PER_GROUP, seq // chunk),
        in_specs=in_specs,
        out_specs=pl.BlockSpec((1, chunk, HEAD_DIM), lambda b, h, c: (b, c, h)),
        scratch_shapes=[pltpu.VMEM((N_GROUPS, chunk, LANES), F32)] * 2
        + [pltpu.VMEM((N_GROUPS, 3, Q_TILE, K_TILE), F32)],
        compiler_params=_compiler_params(("parallel", "parallel", "arbitrary")),
        name="attention",
    )(*args)


def _fft_tables(seq):
    assert seq == FFT_N1 * FFT_N2
    k1 = np.arange(FFT_N1)[None, :, None]
    n1 = np.arange(FFT_N1)[None, None, :]
    n2 = np.arange(FFT_N2)[:, None, None]
    ang = 2.0 * np.pi * ((n2 * k1 + FFT_N2 * n1 * k1) % seq) / seq
    t1 = np.concatenate([np.cos(ang), -np.sin(ang)], axis=1)
    k2 = np.arange(FFT_N2)[:, None]
    m2 = np.arange(FFT_N2)[None, :]
    ang2 = 2.0 * np.pi * ((k2 * m2) % FFT_N2) / FFT_N2
    cr, ci = np.cos(ang2), -np.sin(ang2)
    t2 = np.block([[cr, -ci], [ci, cr]])
    cc = np.arange(FOURIER_GROUP_DIM)
    ang3 = 2.0 * np.pi * ((cc[:, None] * cc[None, :]) % FOURIER_GROUP_DIM) / FOURIER_GROUP_DIM
    norm = 1.0 / math.sqrt(seq * FOURIER_GROUP_DIM)
    t3 = np.concatenate([np.cos(ang3), np.sin(ang3)], axis=0) * norm
    return (jnp.asarray(t1, BF16), jnp.asarray(t2, BF16), jnp.asarray(t3, BF16))


def _fft_kernel(u_ref, t1_ref, t2_ref, t3_ref, o_ref, uf_ref, yr_ref, yi_ref, gi_ref):
    slabs = range(FFT_GROUPS_PER_STEP)
    gr_ref = uf_ref

    def lanes(h):
        return slice(h * LANES, (h + 1) * LANES)

    def widen(n1, carry):
        src = pl.multiple_of(n1 * FFT_N2, FFT_N2)
        dst = pl.multiple_of(n1 * FFT_IN_PITCH, 8)
        u = u_ref[0, pl.ds(src, FFT_N2), :].astype(F32)
        for h in slabs:
            uf_ref[h, pl.ds(dst, FFT_N2), :] = u[:, lanes(h)]
        return carry
    lax.fori_loop(0, FFT_N1, widen, 0, unroll=FFT_UNROLL)

    def stage1(n2, carry):
        rows = pl.ds(n2, FFT_N1, stride=FFT_IN_PITCH)
        x = jnp.concatenate([uf_ref[h, rows, :] for h in slabs], axis=1).astype(BF16)
        y = jnp.dot(t1_ref[n2], x, preferred_element_type=F32)
        dst = pl.multiple_of(n2 * FFT_MID_PITCH, 8)
        for h in slabs:
            yr_ref[h, pl.ds(dst, FFT_N1), :] = y[:FFT_N1, lanes(h)]
            yi_ref[h, pl.ds(dst, FFT_N1), :] = y[FFT_N1:, lanes(h)]
        return carry
    lax.fori_loop(0, FFT_N2, stage1, 0, unroll=FFT_UNROLL)

    def stage2(k1, carry):
        rows = pl.ds(k1, FFT_N2, stride=FFT_MID_PITCH)
        y = jnp.concatenate(
            [jnp.concatenate([yr_ref[h, rows, :], yi_ref[h, rows, :]], axis=0) for h in slabs],
            axis=1).astype(BF16)
        gk = jnp.dot(t2_ref[...], y, preferred_element_type=F32)
        for h in slabs:
            gr_ref[h, rows, :] = gk[:FFT_N2, lanes(h)]
            gi_ref[h, rows, :] = gk[FFT_N2:, lanes(h)]
        return carry
    lax.fori_loop(0, FFT_N1, stage2, 0, unroll=FFT_UNROLL)

    def stage3(k2, carry):
        src = pl.ds(pl.multiple_of(k2 * FFT_MID_PITCH, 8), FFT_N1)
        dst = pl.ds(pl.multiple_of(k2 * FFT_N1, FFT_N1), FFT_N1)
        for h in slabs:
            gk = jnp.concatenate([gr_ref[h, src, :], gi_ref[h, src, :]], axis=1).astype(BF16)
            out = jnp.dot(gk, t3_ref[...], preferred_element_type=F32)
            o_ref[0, dst, lanes(h)] = out.astype(BF16)
        return carry
    lax.fori_loop(0, FFT_N2, stage3, 0, unroll=FFT_UNROLL)


def _fft(u, *, batch, seq):
    t1, t2, t3 = _fft_tables(seq)
    width = FFT_GROUPS_PER_STEP * FOURIER_GROUP_DIM
    blk = pl.BlockSpec((1, seq, width), lambda b, g: (b, 0, g))
    return pl.pallas_call(
        _fft_kernel,
        out_shape=jax.ShapeDtypeStruct((batch, seq, FOURIER_WIDTH), BF16),
        grid=(batch, FOURIER_WIDTH // width),
        in_specs=[blk, _resident(t1.shape), _resident(t2.shape), _resident(t3.shape)],
        out_specs=blk,
        scratch_shapes=[pltpu.VMEM((FFT_GROUPS_PER_STEP, FFT_N1 * FFT_IN_PITCH, LANES), F32)]
        + [pltpu.VMEM((FFT_GROUPS_PER_STEP, FFT_N2 * FFT_MID_PITCH, LANES), F32)] * 3,
        compiler_params=_compiler_params(("parallel", "parallel")),
        name="fft",
    )(u, t1, t2, t3)


def _merge_ln_kernel(h_ref, a_ref, f_ref, wgate_ref, bgate_ref, wpa_ref, wpf_ref, wout_ref,
                     g_ref, b_ref, o_ref, *, alpha):
    d = h_ref.shape[1]
    for s in range(h_ref.shape[0] // MERGE_SUB_ROWS):
        rows = slice(s * MERGE_SUB_ROWS, (s + 1) * MERGE_SUB_ROWS)
        h = h_ref[rows, :]
        hb = h.astype(BF16)
        a = a_ref[rows, :]
        f = f_ref[rows, :]
        y = jnp.zeros(h.shape, F32)
        for c in range(d // MERGE_CHUNK):
            ca = slice(c * MERGE_CHUNK, (c + 1) * MERGE_CHUNK)
            cf = slice(d + c * MERGE_CHUNK, d + (c + 1) * MERGE_CHUNK)
            ga = jax.nn.sigmoid(jnp.dot(hb, wgate_ref[:, ca], preferred_element_type=F32)
                                + bgate_ref[:, ca])
            gf = jax.nn.sigmoid(jnp.dot(hb, wgate_ref[:, cf], preferred_element_type=F32)
                                + bgate_ref[:, cf])
            pa = jnp.dot(a, wpa_ref[:, ca], preferred_element_type=F32)
            pf = jnp.dot(f, wpf_ref[:, ca], preferred_element_type=F32)
            merged = (ga * pa + gf * pf).astype(BF16)
            y = y + jnp.dot(merged, wout_ref[ca, :], preferred_element_type=F32)
        o_ref[rows, :] = _layer_norm(alpha * h + y, g_ref[...], b_ref[...])


def _merge_ln(h, attn, four, w_gate, b_gate, w_pa, w_pf, w_out, ln_g, ln_b, *, alpha,
              tm=ROW_TILE):
    n, d = h.shape
    row = pl.BlockSpec((tm, d), lambda i: (i, 0))
    half = pl.BlockSpec((tm, attn.shape[1]), lambda i: (i, 0))
    return pl.pallas_call(
        functools.partial(_merge_ln_kernel, alpha=alpha),
        out_shape=jax.ShapeDtypeStruct((n, d), F32),
        grid=(n // tm,),
        in_specs=[row, half, half, _resident(w_gate.shape), _resident(b_gate.shape),
                  _resident(w_pa.shape), _resident(w_pf.shape), _resident(w_out.shape),
                  _resident((1, d)), _resident((1, d))],
        out_specs=row,
        compiler_params=_compiler_params(("parallel",)),
        name="merge_ln",
    )(h, attn, four, w_gate, b_gate, w_pa, w_pf, w_out, ln_g.reshape(1, d), ln_b.reshape(1, d))


GATE_COL0 = 3 * ATTN_QKV_WIDTH + FOURIER_WIDTH


def _in_proj_pieces():
    a = ATTN_QKV_WIDTH
    return tuple((part * a + g * GROUP_WIDTH, GROUP_WIDTH,
                  HEAD_DIM ** -0.5 * LOG2_E if part == 0 else 1.0)
                 for g in range(N_GROUPS) for part in range(3)) + ((3 * a, FOURIER_WIDTH, 1.0),)


def _mixer_ln(h, w_qkvu, w_gate, b_in, rel_bias, w_pa, w_pf, w_out, ln_g, ln_b,
              *, alpha, batch, seq):
    b_qkvu = jnp.concatenate([b_in[s:s + w] * f for s, w, f in _in_proj_pieces()]).reshape(1, -1)
    b_gate = b_in[GATE_COL0:].reshape(1, -1)
    qkv0, qkv1, qkv2, u = _in_proj(h, w_qkvu, b_qkvu, batch=batch, seq=seq)
    attn = _attention((qkv0, qkv1, qkv2), _attention_bias_rows(rel_bias), batch=batch, seq=seq)
    four = _fft(u, batch=batch, seq=seq)
    return _merge_ln(h, attn.reshape(batch * seq, -1), four.reshape(batch * seq, -1),
                     w_gate, b_gate, w_pa, w_pf, w_out, ln_g, ln_b, alpha=alpha)


def kernel(x, ln1_g, ln1_b, ffn1_w_gate, ffn1_w_up, ffn1_w_down, w_in, b_in, rel_bias,
           w_proj_attn, w_proj_fourier, w_out, ln2_g, ln2_b, ffn2_w_gate, ffn2_w_up,
           ffn2_w_down, ln3_g, ln3_b):
    batch, seq, d = x.shape
    depth = ln1_g.shape[0]
    alpha = (2 * depth) ** 0.25
    h = x.reshape(batch * seq, d)
    for l in range(depth):
        gates = ((GATE_COL0, w_in.shape[2] - GATE_COL0, 1.0),)
        jobs = (_whole(ffn2_w_gate[l]), _whole(ffn2_w_up[l]),
                CastJob(w_in[l], (_in_proj_pieces(), gates)),
                _whole(w_proj_attn[l]), _whole(w_proj_fourier[l]), _whole(w_out[l]))
        h, (wg2, wu2, w_qkvu, w_gate, w_pa, w_pf, w_o) = _ffn_ln(
            h, ffn1_w_gate[l].astype(BF16), ffn1_w_up[l].astype(BF16),
            ffn1_w_down[l].astype(BF16), ln1_g[l], ln1_b[l], alpha=alpha, cast_jobs=jobs)
        h = _mixer_ln(h, w_qkvu, w_gate, b_in[l], rel_bias, w_pa, w_pf, w_o,
                      ln2_g[l], ln2_b[l], alpha=alpha, batch=batch, seq=seq)
        h, _ = _ffn_ln(h, wg2, wu2, ffn2_w_down[l].astype(BF16), ln3_g[l], ln3_b[l], alpha=alpha)
    return h.reshape(batch, seq, d)
```

```python
import functools
import math
from typing import NamedTuple

import numpy as np
import jax
import jax.numpy as jnp
from jax import lax
from jax.experimental import pallas as pl
from jax.experimental.pallas import tpu as pltpu

F32 = jnp.float32
BF16 = jnp.bfloat16

HEAD_DIM = 128
HEADS_PER_GROUP = 4
ATTN_PATTERNS = ((128, 1), (512, 4), (2048, 16))
N_GROUPS = len(ATTN_PATTERNS)
GROUP_WIDTH = HEADS_PER_GROUP * HEAD_DIM
ATTN_QKV_WIDTH = N_GROUPS * GROUP_WIDTH
FOURIER_GROUPS = 4
FOURIER_GROUP_DIM = 128
FOURIER_WIDTH = FOURIER_GROUPS * FOURIER_GROUP_DIM
NUM_BUCKETS = 32
MAX_EXACT = 8
MAX_DISTANCE = 1024
NEG_INF = -1e30
LN_EPS = 1e-5
LOG2_E = math.log2(math.e)

LANES = 128
BF16_SUBLANES = 16
MXU_DIM = 256
V7X_VMEM_BYTES = 64 * 1024 * 1024
VMEM_LIMIT_BYTES = V7X_VMEM_BYTES * 7 // 8

ROW_TILE = 1024
SUB_ROWS = 512
MERGE_SUB_ROWS = 256
SINGLE_OP_STRIDE = 4
MERGE_CHUNK = 2 * MXU_DIM
STAGE_SLOTS = 3

HALF_WIN = 64
Q_TILE = 128
K_TILE = Q_TILE + 2 * HALF_WIN
BIAS_PERIOD = Q_TILE + K_TILE
ATTN_CHUNK = 2048
ATTN_UNROLL = 16
FFT_UNROLL = 16

FFT_N1 = 128
FFT_N2 = 64
FFT_GROUPS_PER_STEP = 1
FFT_IN_PITCH = FFT_N2 + 8
FFT_MID_PITCH = FFT_N1 + 8


def _compiler_params(semantics):
    return pltpu.CompilerParams(dimension_semantics=semantics,
                                vmem_limit_bytes=VMEM_LIMIT_BYTES)


def _resident(shape):
    zeros = (0,) * len(shape)
    return pl.BlockSpec(shape, lambda *_: zeros, pipeline_mode=pl.Buffered(1))


def _layer_norm(z, g, b):
    mu = jnp.mean(z, axis=-1, keepdims=True)
    zc = z - mu
    var = jnp.mean(zc * zc, axis=-1, keepdims=True)
    return zc * lax.rsqrt(var + LN_EPS) * g + b


class CastJob(NamedTuple):
    src: jax.Array
    outputs: tuple


def _whole(w):
    return CastJob(w, (((0, w.shape[1], 1.0),),))


def _run_cast_jobs(job_outputs, src_refs, out_refs):
    out_refs = iter(out_refs)
    for outputs, src_ref in zip(job_outputs, src_refs):
        for pieces in outputs:
            out_ref = next(out_refs)
            dst = 0
            for col0, width, factor in pieces:
                blk = src_ref[:, col0:col0 + width]
                out_ref[:, dst:dst + width] = (blk if factor == 1.0 else blk * factor).astype(BF16)
                dst += width


def _ffn_ln_kernel(x_ref, wg_ref, wu_ref, wd_ref, g_ref, b_ref, *refs, alpha, f_chunk, jobs):
    o_ref = refs[len(jobs)]
    _run_cast_jobs(jobs, refs[:len(jobs)], refs[len(jobs) + 1:])
    f = wd_ref.shape[0]
    for s in range(x_ref.shape[0] // SUB_ROWS):
        rows = slice(s * SUB_ROWS, (s + 1) * SUB_ROWS)
        x = x_ref[rows, :]
        xb = x.astype(BF16)
        acts = []
        for c0 in range(0, f, f_chunk):
            sl = slice(c0, min(c0 + f_chunk, f))
            gate = jnp.dot(xb, wg_ref[:, sl], preferred_element_type=F32)
            up = jnp.dot(xb, wu_ref[:, sl], preferred_element_type=F32)
            acts.append((gate * jax.nn.sigmoid(gate) * up).astype(BF16))
        y = jnp.dot(jnp.concatenate(acts, axis=1), wd_ref[...], preferred_element_type=F32)
        o_ref[rows, :] = _layer_norm(alpha * x + 0.5 * y, g_ref[...], b_ref[...])


def _ffn_ln(x, w_gate, w_up, w_down, ln_g, ln_b, *, alpha, cast_jobs=(), tm=ROW_TILE,
            f_chunk=MXU_DIM):
    n, d = x.shape
    f = w_gate.shape[1]
    steps = n // tm
    row = pl.BlockSpec((tm, d), lambda i: (i, 0))

    def row_block(rows, cols):
        assert rows % steps == 0 and (rows // steps) % BF16_SUBLANES == 0
        return pl.BlockSpec((rows // steps, cols), lambda i: (i, 0))

    job_in_specs = [row_block(*job.src.shape) for job in cast_jobs]
    job_out_widths = [(job.src.shape[0], sum(width for _, width, _ in pieces))
                      for job in cast_jobs for pieces in job.outputs]
    outs = pl.pallas_call(
        functools.partial(_ffn_ln_kernel, alpha=alpha, f_chunk=f_chunk,
                          jobs=tuple(job.outputs for job in cast_jobs)),
        out_shape=(jax.ShapeDtypeStruct((n, d), F32),)
        + tuple(jax.ShapeDtypeStruct(shape, BF16) for shape in job_out_widths),
        grid=(steps,),
        in_specs=[row, _resident((d, f)), _resident((d, f)), _resident((f, d)),
                  _resident((1, d)), _resident((1, d))] + job_in_specs,
        out_specs=(row,) + tuple(row_block(*shape) for shape in job_out_widths),
        compiler_params=_compiler_params(("parallel",)),
        name="ffn_ln",
    )(x, w_gate, w_up, w_down, ln_g.reshape(1, d), ln_b.reshape(1, d),
      *[job.src for job in cast_jobs])
    return outs[0], outs[1:]


def _store_deinterleaved(res, out_ref, dil, t, col0, stage_ref, mid_ref, slot):
    n = res.shape[0]
    n_sub = n // dil
    out_rows = slice(t * n_sub, (t + 1) * n_sub)
    inner = min(dil, SINGLE_OP_STRIDE)
    outer = dil // inner
    part = n // inner
    for h in range(res.shape[1] // LANES):
        cols = slice(col0 + h * LANES, col0 + (h + 1) * LANES)
        stage_ref[slot, h] = res[:, h * LANES:(h + 1) * LANES]
        if outer == 1:
            for r in range(dil):
                out_ref[0, r, out_rows, cols] = (
                    stage_ref[slot, h, pl.ds(r, n_sub, stride=dil), :].astype(BF16))
            continue
        for ra in range(inner):
            mid_ref[slot, h, ra * part:(ra + 1) * part] = (
                stage_ref[slot, h, pl.ds(ra, part, stride=inner), :])
        for ra in range(inner):
            for rb in range(outer):
                out_ref[0, ra + inner * rb, out_rows, cols] = (
                    mid_ref[slot, h, pl.ds(ra * part + rb, n_sub, stride=outer), :].astype(BF16))


def _in_proj_kernel(x_ref, w_ref, b_ref, qkv0_ref, qkv1_ref, qkv2_ref, u_ref, stage_ref, mid_ref):
    gw = 3 * GROUP_WIDTH
    piece = 0
    for t in range(x_ref.shape[0] // SUB_ROWS):
        rows = slice(t * SUB_ROWS, (t + 1) * SUB_ROWS)
        xb = x_ref[rows, :].astype(BF16)

        def proj(c0, xb=xb):
            return (jnp.dot(xb, w_ref[:, c0:c0 + MXU_DIM], preferred_element_type=F32)
                    + b_ref[:, c0:c0 + MXU_DIM])

        for g, out_ref in ((2, qkv2_ref), (1, qkv1_ref)):
            for j in range(gw // MXU_DIM):
                _store_deinterleaved(proj(g * gw + j * MXU_DIM), out_ref, ATTN_PATTERNS[g][1], t,
                                     j * MXU_DIM, stage_ref, mid_ref, piece % STAGE_SLOTS)
                piece += 1

        for j in range(gw // MXU_DIM):
            qkv0_ref[0, 0, rows, j * MXU_DIM:(j + 1) * MXU_DIM] = proj(j * MXU_DIM).astype(BF16)
        for j in range(FOURIER_WIDTH // MXU_DIM):
            u_ref[0, rows, j * MXU_DIM:(j + 1) * MXU_DIM] = (
                proj(N_GROUPS * gw + j * MXU_DIM).astype(BF16))


def _in_proj(h, w, b, *, batch, seq, tm=ROW_TILE):
    n, d = h.shape
    width = w.shape[1]
    gw = 3 * GROUP_WIDTH
    tiles_per_seq = seq // tm
    d1, d2 = ATTN_PATTERNS[1][1], ATTN_PATTERNS[2][1]
    out_shape = (
        jax.ShapeDtypeStruct((batch, 1, seq, gw), BF16),
        jax.ShapeDtypeStruct((batch, d1, seq // d1, gw), BF16),
        jax.ShapeDtypeStruct((batch, d2, seq // d2, gw), BF16),
        jax.ShapeDtypeStruct((batch, seq, FOURIER_WIDTH), BF16),
    )

    def bt(i):
        return i // tiles_per_seq, i % tiles_per_seq

    out_specs = (
        pl.BlockSpec((1, 1, tm, gw), lambda i: (bt(i)[0], 0, bt(i)[1], 0)),
        pl.BlockSpec((1, d1, tm // d1, gw), lambda i: (bt(i)[0], 0, bt(i)[1], 0)),
        pl.BlockSpec((1, d2, tm // d2, gw), lambda i: (bt(i)[0], 0, bt(i)[1], 0)),
        pl.BlockSpec((1, tm, FOURIER_WIDTH), lambda i: (bt(i)[0], bt(i)[1], 0)),
    )
    return pl.pallas_call(
        _in_proj_kernel,
        out_shape=out_shape,
        grid=(n // tm,),
        in_specs=[pl.BlockSpec((tm, d), lambda i: (i, 0)), _resident((d, width)),
                  _resident((1, width))],
        out_specs=out_specs,
        scratch_shapes=[pltpu.VMEM((STAGE_SLOTS, MXU_DIM // LANES, SUB_ROWS, LANES), F32)] * 2,
        compiler_params=_compiler_params(("parallel",)),
        name="in_proj",
    )(h, w, b)


def _t5_bucket_np(rel):
    half = NUM_BUCKETS // 2
    ret = (rel > 0).astype(np.int64) * half
    n = np.abs(rel)
    nf = np.maximum(n, 1).astype(np.float64)
    large = MAX_EXACT + (np.log(nf / MAX_EXACT) / math.log(MAX_DISTANCE / MAX_EXACT)
                         * (half - MAX_EXACT)).astype(np.int64)
    large = np.minimum(large, half - 1)
    return ret + np.where(n < MAX_EXACT, n, large)


def _attention_bias_rows(rel_bias):
    offs = np.arange(-HALF_WIN, HALF_WIN + 1)
    band = []
    for g, (_, dil) in enumerate(ATTN_PATTERNS):
        buckets = _t5_bucket_np(offs * dil)
        heads = rel_bias[:, g * HEADS_PER_GROUP:(g + 1) * HEADS_PER_GROUP]
        band.append(jnp.take(heads, jnp.asarray(buckets, jnp.int32), axis=0).T)
    band = jnp.stack(band).astype(F32) * LOG2_E
    base = jnp.concatenate(
        [band, jnp.full(band.shape[:2] + (BIAS_PERIOD - band.shape[-1],), NEG_INF, F32)], axis=-1)
    rows = [jnp.roll(base, HALF_WIN * (v - 1), axis=-1) for v in range(3)]
    return jnp.stack(rows, axis=2)[:, :, :, None, :]


def _attn_kernel(q0, k0, v0, q1, k1, v1, q2, k2, v2, rows_ref, o_ref, out_ref, lse_ref, bias_ref,
                 *, chunk):
    c = pl.program_id(2)

    @pl.when(c == 0)
    def _():
        for g in range(N_GROUPS):
            for v in range(3):
                row0 = jnp.broadcast_to(rows_ref[g, 0, v], (Q_TILE, BIAS_PERIOD))
                circulant = pltpu.roll(row0, 0, 1, stride=1, stride_axis=0)
                bias_ref[g, v] = circulant[:, :K_TILE]

    for g, (q_ref, k_ref, v_ref) in enumerate(((q0, k0, v0), (q1, k1, v1), (q2, k2, v2))):
        dil = ATTN_PATTERNS[g][1]
        sub_len = k_ref.shape[2]
        rows_per_sub = chunk // dil
        tiles_per_sub = rows_per_sub // Q_TILE

        def tile(i, carry, g=g, dil=dil, sub_len=sub_len, rows_per_sub=rows_per_sub,
                 tiles_per_sub=tiles_per_sub, q_ref=q_ref, k_ref=k_ref, v_ref=v_ref):
            r = i // tiles_per_sub
            t = i % tiles_per_sub
            q_start = c * rows_per_sub + t * Q_TILE
            k_start = jnp.clip(q_start - HALF_WIN, 0, sub_len - K_TILE)
            variant = (q_start - k_start) // HALF_WIN
            k_start = pl.multiple_of(k_start, HALF_WIN)
            q = q_ref[0, r, pl.ds(pl.multiple_of(t * Q_TILE, Q_TILE), Q_TILE), :]
            k = k_ref[0, r, pl.ds(k_start, K_TILE), :]
            v = v_ref[0, r, pl.ds(k_start, K_TILE), :]
            s = lax.dot_general(q, k, (((1,), (1,)), ((), ())), preferred_element_type=F32)
            s = s + bias_ref[g, variant]
            m = jnp.max(s, axis=-1, keepdims=True)
            p = jnp.exp2(s - m)
            den = jnp.sum(p, axis=-1, keepdims=True)
            row0 = t * Q_TILE * dil + r
            rows = pl.ds(row0, Q_TILE) if dil == 1 else pl.ds(row0, Q_TILE, stride=dil)
            out_ref[g, rows, :] = (jnp.dot(p.astype(BF16), v, preferred_element_type=F32)
                                   * (1.0 / den))
            lse_ref[g, rows, :] = jnp.broadcast_to(m + jnp.log2(den), (Q_TILE, LANES))
            return carry

        lax.fori_loop(0, dil * tiles_per_sub, tile, 0, unroll=ATTN_UNROLL)

    blk = 256
    for i in range(chunk // blk):
        rows = slice(i * blk, (i + 1) * blk)
        lses = [lse_ref[g, rows, :] for g in range(N_GROUPS)]
        mx = jnp.maximum(jnp.maximum(lses[0], lses[1]), lses[2])
        ws = [jnp.exp2(l - mx) for l in lses]
        num = ws[0] * out_ref[0, rows, :] + ws[1] * out_ref[1, rows, :] + ws[2] * out_ref[2, rows, :]
        o_ref[0, rows, :] = (num / (ws[0] + ws[1] + ws[2])).astype(BF16)


def _attention(qkv, bias_rows, *, batch, seq, chunk=ATTN_CHUNK):
    in_specs = []
    for g, arr in enumerate(qkv):
        dil = ATTN_PATTERNS[g][1]
        sub_len = seq // dil
        in_specs += [
            pl.BlockSpec((1, dil, chunk // dil, HEAD_DIM), lambda b, h, c: (b, 0, c, h)),
            pl.BlockSpec((1, dil, sub_len, HEAD_DIM),
                         lambda b, h, c: (b, 0, 0, HEADS_PER_GROUP + h)),
            pl.BlockSpec((1, dil, sub_len, HEAD_DIM),
                         lambda b, h, c: (b, 0, 0, 2 * HEADS_PER_GROUP + h)),
        ]
    in_specs.append(pl.BlockSpec((N_GROUPS, 1, 3, 1, BIAS_PERIOD),
                                 lambda b, h, c: (0, h, 0, 0, 0)))
    args = [a for arr in qkv for a in (arr, arr, arr)] + [bias_rows]
    return pl.pallas_call(
        functools.partial(_attn_kernel, chunk=chunk),
        out_shape=jax.ShapeDtypeStruct((batch, seq, GROUP_WIDTH), BF16),
        grid=(batch, HEADS_PER_GROUP, seq // chunk),
        in_specs=in_specs,
        out_specs=pl.BlockSpec((1, chunk, HEAD_DIM), lambda b, h, c: (b, c, h)),
        scratch_shapes=[pltpu.VMEM((N_GROUPS, chunk, LANES), F32)] * 2
        + [pltpu.VMEM((N_GROUPS, 3, Q_TILE, K_TILE), F32)],
        compiler_params=_compiler_params(("parallel", "parallel", "arbitrary")),
        name="attention",
    )(*args)


def _fft_tables(seq):
    assert seq == FFT_N1 * FFT_N2
    k1 = np.arange(FFT_N1)[None, :, None]
    n1 = np.arange(FFT_N1)[None, None, :]
    n2 = np.arange(FFT_N2)[:, None, None]
    ang = 2.0 * np.pi * ((n2 * k1 + FFT_N2 * n1 * k1) % seq) / seq
    t1 = np.concatenate([np.cos(ang), -np.sin(ang)], axis=1)
    k2 = np.arange(FFT_N2)[:, None]
    m2 = np.arange(FFT_N2)[None, :]
    ang2 = 2.0 * np.pi * ((k2 * m2) % FFT_N2) / FFT_N2
    cr, ci = np.cos(ang2), -np.sin(ang2)
    t2 = np.block([[cr, -ci], [ci, cr]])
    cc = np.arange(FOURIER_GROUP_DIM)
    ang3 = 2.0 * np.pi * ((cc[:, None] * cc[None, :]) % FOURIER_GROUP_DIM) / FOURIER_GROUP_DIM
    norm = 1.0 / math.sqrt(seq * FOURIER_GROUP_DIM)
    t3 = np.concatenate([np.cos(ang3), np.sin(ang3)], axis=0) * norm
    return (jnp.asarray(t1, BF16), jnp.asarray(t2, BF16), jnp.asarray(t3, BF16))


def _fft_kernel(u_ref, t1_ref, t2_ref, t3_ref, o_ref, uf_ref, yr_ref, yi_ref, gi_ref):
    slabs = range(FFT_GROUPS_PER_STEP)
    gr_ref = uf_ref

    def lanes(h):
        return slice(h * LANES, (h + 1) * LANES)

    def widen(n1, carry):
        src = pl.multiple_of(n1 * FFT_N2, FFT_N2)
        dst = pl.multiple_of(n1 * FFT_IN_PITCH, 8)
        u = u_ref[0, pl.ds(src, FFT_N2), :].astype(F32)
        for h in slabs:
            uf_ref[h, pl.ds(dst, FFT_N2), :] = u[:, lanes(h)]
        return carry
    lax.fori_loop(0, FFT_N1, widen, 0, unroll=FFT_UNROLL)

    def stage1(n2, carry):
        rows = pl.ds(n2, FFT_N1, stride=FFT_IN_PITCH)
        x = jnp.concatenate([uf_ref[h, rows, :] for h in slabs], axis=1).astype(BF16)
        y = jnp.dot(t1_ref[n2], x, preferred_element_type=F32)
        dst = pl.multiple_of(n2 * FFT_MID_PITCH, 8)
        for h in slabs:
            yr_ref[h, pl.ds(dst, FFT_N1), :] = y[:FFT_N1, lanes(h)]
            yi_ref[h, pl.ds(dst, FFT_N1), :] = y[FFT_N1:, lanes(h)]
        return carry
    lax.fori_loop(0, FFT_N2, stage1, 0, unroll=FFT_UNROLL)

    def stage2(k1, carry):
        rows = pl.ds(k1, FFT_N2, stride=FFT_MID_PITCH)
        y = jnp.concatenate(
            [jnp.concatenate([yr_ref[h, rows, :], yi_ref[h, rows, :]], axis=0) for h in slabs],
            axis=1).astype(BF16)
        gk = jnp.dot(t2_ref[...], y, preferred_element_type=F32)
        for h in slabs:
            gr_ref[h, rows, :] = gk[:FFT_N2, lanes(h)]
            gi_ref[h, rows, :] = gk[FFT_N2:, lanes(h)]
        return carry
    lax.fori_loop(0, FFT_N1, stage2, 0, unroll=FFT_UNROLL)

    def stage3(k2, carry):
        src = pl.ds(pl.multiple_of(k2 * FFT_MID_PITCH, 8), FFT_N1)
        dst = pl.ds(pl.multiple_of(k2 * FFT_N1, FFT_N1), FFT_N1)
        for h in slabs:
            gk = jnp.concatenate([gr_ref[h, src, :], gi_ref[h, src, :]], axis=1).astype(BF16)
            out = jnp.dot(gk, t3_ref[...], preferred_element_type=F32)
            o_ref[0, dst, lanes(h)] = out.astype(BF16)
        return carry
    lax.fori_loop(0, FFT_N2, stage3, 0, unroll=FFT_UNROLL)


def _fft(u, *, batch, seq):
    t1, t2, t3 = _fft_tables(seq)
    width = FFT_GROUPS_PER_STEP * FOURIER_GROUP_DIM
    blk = pl.BlockSpec((1, seq, width), lambda b, g: (b, 0, g))
    return pl.pallas_call(
        _fft_kernel,
        out_shape=jax.ShapeDtypeStruct((batch, seq, FOURIER_WIDTH), BF16),
        grid=(batch, FOURIER_WIDTH // width),
        in_specs=[blk, _resident(t1.shape), _resident(t2.shape), _resident(t3.shape)],
        out_specs=blk,
        scratch_shapes=[pltpu.VMEM((FFT_GROUPS_PER_STEP, FFT_N1 * FFT_IN_PITCH, LANES), F32)]
        + [pltpu.VMEM((FFT_GROUPS_PER_STEP, FFT_N2 * FFT_MID_PITCH, LANES), F32)] * 3,
        compiler_params=_compiler_params(("parallel", "parallel")),
        name="fft",
    )(u, t1, t2, t3)


def _merge_ln_kernel(h_ref, a_ref, f_ref, wgate_ref, bgate_ref, wpa_ref, wpf_ref, wout_ref,
                     g_ref, b_ref, o_ref, *, alpha):
    d = h_ref.shape[1]
    for s in range(h_ref.shape[0] // MERGE_SUB_ROWS):
        rows = slice(s * MERGE_SUB_ROWS, (s + 1) * MERGE_SUB_ROWS)
        h = h_ref[rows, :]
        hb = h.astype(BF16)
        a = a_ref[rows, :]
        f = f_ref[rows, :]
        merged = []
        for c in range(d // MERGE_CHUNK):
            ca = slice(c * MERGE_CHUNK, (c + 1) * MERGE_CHUNK)
            cf = slice(d + c * MERGE_CHUNK, d + (c + 1) * MERGE_CHUNK)
            ga = jax.nn.sigmoid(jnp.dot(hb, wgate_ref[:, ca], preferred_element_type=F32)
                                + bgate_ref[:, ca])
            gf = jax.nn.sigmoid(jnp.dot(hb, wgate_ref[:, cf], preferred_element_type=F32)
                                + bgate_ref[:, cf])
            pa = jnp.dot(a, wpa_ref[:, ca], preferred_element_type=F32)
            pf = jnp.dot(f, wpf_ref[:, ca], preferred_element_type=F32)
            merged.append((ga * pa + gf * pf).astype(BF16))
        y = jnp.dot(jnp.concatenate(merged, axis=1), wout_ref[...], preferred_element_type=F32)
        o_ref[rows, :] = _layer_norm(alpha * h + y, g_ref[...], b_ref[...])


def _merge_ln(h, attn, four, w_gate, b_gate, w_pa, w_pf, w_out, ln_g, ln_b, *, alpha,
              tm=ROW_TILE):
    n, d = h.shape
    row = pl.BlockSpec((tm, d), lambda i: (i, 0))
    half = pl.BlockSpec((tm, attn.shape[1]), lambda i: (i, 0))
    return pl.pallas_call(
        functools.partial(_merge_ln_kernel, alpha=alpha),
        out_shape=jax.ShapeDtypeStruct((n, d), F32),
        grid=(n // tm,),
        in_specs=[row, half, half, _resident(w_gate.shape), _resident(b_gate.shape),
                  _resident(w_pa.shape), _resident(w_pf.shape), _resident(w_out.shape),
                  _resident((1, d)), _resident((1, d))],
        out_specs=row,
        compiler_params=_compiler_params(("parallel",)),
        name="merge_ln",
    )(h, attn, four, w_gate, b_gate, w_pa, w_pf, w_out, ln_g.reshape(1, d), ln_b.reshape(1, d))


GATE_COL0 = 3 * ATTN_QKV_WIDTH + FOURIER_WIDTH


def _in_proj_pieces():
    a = ATTN_QKV_WIDTH
    return tuple((part * a + g * GROUP_WIDTH, GROUP_WIDTH,
                  HEAD_DIM ** -0.5 * LOG2_E if part == 0 else 1.0)
                 for g in range(N_GROUPS) for part in range(3)) + ((3 * a, FOURIER_WIDTH, 1.0),)


def _mixer_ln(h, w_qkvu, w_gate, b_in, rel_bias, w_pa, w_pf, w_out, ln_g, ln_b,
              *, alpha, batch, seq):
    b_qkvu = jnp.concatenate([b_in[s:s + w] * f for s, w, f in _in_proj_pieces()]).reshape(1, -1)
    b_gate = b_in[GATE_COL0:].reshape(1, -1)
    qkv0, qkv1, qkv2, u = _in_proj(h, w_qkvu, b_qkvu, batch=batch, seq=seq)
    attn = _attention((qkv0, qkv1, qkv2), _attention_bias_rows(rel_bias), batch=batch, seq=seq)
    four = _fft(u, batch=batch, seq=seq)
    return _merge_ln(h, attn.reshape(batch * seq, -1), four.reshape(batch * seq, -1),
                     w_gate, b_gate, w_pa, w_pf, w_out, ln_g, ln_b, alpha=alpha)


def kernel(x, ln1_g, ln1_b, ffn1_w_gate, ffn1_w_up, ffn1_w_down, w_in, b_in, rel_bias,
           w_proj_attn, w_proj_fourier, w_out, ln2_g, ln2_b, ffn2_w_gate, ffn2_w_up,
           ffn2_w_down, ln3_g, ln3_b):
    batch, seq, d = x.shape
    depth = ln1_g.shape[0]
    alpha = (2 * depth) ** 0.25
    h = x.reshape(batch * seq, d)
    for l in range(depth):
        gates = ((GATE_COL0, w_in.shape[2] - GATE_COL0, 1.0),)
        jobs = (CastJob(w_in[l], (_in_proj_pieces(), gates)),
                _whole(w_proj_attn[l]), _whole(w_proj_fourier[l]), _whole(w_out[l]))
        h, (w_qkvu, w_gate, w_pa, w_pf, w_o) = _ffn_ln(
            h, ffn1_w_gate[l].astype(BF16), ffn1_w_up[l].astype(BF16),
            ffn1_w_down[l].astype(BF16), ln1_g[l], ln1_b[l], alpha=alpha, cast_jobs=jobs)
        h = _mixer_ln(h, w_qkvu, w_gate, b_in[l], rel_bias, w_pa, w_pf, w_o,
                      ln2_g[l], ln2_b[l], alpha=alpha, batch=batch, seq=seq)
        h, _ = _ffn_ln(h, ffn2_w_gate[l].astype(BF16), ffn2_w_up[l].astype(BF16),
                       ffn2_w_down[l].astype(BF16), ln3_g[l], ln3_b[l], alpha=alpha)
    return h.reshape(batch, seq, d)
```

```python
import functools
import math
from typing import NamedTuple

import numpy as np
import jax
import jax.numpy as jnp
from jax import lax
from jax.experimental import pallas as pl
from jax.experimental.pallas import tpu as pltpu

F32 = jnp.float32
BF16 = jnp.bfloat16

HEAD_DIM = 128
HEADS_PER_GROUP = 4
ATTN_PATTERNS = ((128, 1), (512, 4), (2048, 16))
N_GROUPS = len(ATTN_PATTERNS)
GROUP_WIDTH = HEADS_PER_GROUP * HEAD_DIM
ATTN_QKV_WIDTH = N_GROUPS * GROUP_WIDTH
FOURIER_GROUPS = 4
FOURIER_GROUP_DIM = 128
FOURIER_WIDTH = FOURIER_GROUPS * FOURIER_GROUP_DIM
NUM_BUCKETS = 32
MAX_EXACT = 8
MAX_DISTANCE = 1024
NEG_INF = -1e30
LN_EPS = 1e-5
LOG2_E = math.log2(math.e)

LANES = 128
BF16_SUBLANES = 16
MXU_DIM = 256
V7X_VMEM_BYTES = 64 * 1024 * 1024
VMEM_LIMIT_BYTES = V7X_VMEM_BYTES * 7 // 8

ROW_TILE = 1024
SUB_ROWS = 512
MERGE_SUB_ROWS = 256
SINGLE_OP_STRIDE = 4
MERGE_CHUNK = 2 * MXU_DIM
STAGE_SLOTS = 3

HALF_WIN = 64
Q_TILE = 128
K_TILE = Q_TILE + 2 * HALF_WIN
BIAS_PERIOD = Q_TILE + K_TILE
ATTN_CHUNK = 2048
ATTN_UNROLL = 16
FFT_UNROLL = 64

FFT_N1 = 128
FFT_N2 = 64
FFT_K1_PER_DOT = 2
FFT_GROUPS_PER_STEP = 1
FFT_IN_PITCH = FFT_N2 + 8
FFT_MID_PITCH = FFT_N1 + 8


def _compiler_params(semantics):
    return pltpu.CompilerParams(dimension_semantics=semantics,
                                vmem_limit_bytes=VMEM_LIMIT_BYTES)


def _resident(shape):
    zeros = (0,) * len(shape)
    return pl.BlockSpec(shape, lambda *_: zeros, pipeline_mode=pl.Buffered(1))


def _layer_norm(z, g, b):
    mu = jnp.mean(z, axis=-1, keepdims=True)
    zc = z - mu
    var = jnp.mean(zc * zc, axis=-1, keepdims=True)
    return zc * lax.rsqrt(var + LN_EPS) * g + b


class CastJob(NamedTuple):
    src: jax.Array
    outputs: tuple


def _whole(w):
    return CastJob(w, (((0, w.shape[1], 1.0),),))


def _run_cast_jobs(job_outputs, src_refs, out_refs):
    out_refs = iter(out_refs)
    for outputs, src_ref in zip(job_outputs, src_refs):
        for pieces in outputs:
            out_ref = next(out_refs)
            dst = 0
            for col0, width, factor in pieces:
                blk = src_ref[:, col0:col0 + width]
                out_ref[:, dst:dst + width] = (blk if factor == 1.0 else blk * factor).astype(BF16)
                dst += width


def _ffn_ln_kernel(x_ref, wg_ref, wu_ref, wd_ref, g_ref, b_ref, *refs, alpha, f_chunk, jobs):
    o_ref = refs[len(jobs)]
    _run_cast_jobs(jobs, refs[:len(jobs)], refs[len(jobs) + 1:])
    f = wd_ref.shape[0]
    for s in range(x_ref.shape[0] // SUB_ROWS):
        rows = slice(s * SUB_ROWS, (s + 1) * SUB_ROWS)
        x = x_ref[rows, :]
        xb = x.astype(BF16)
        acts = []
        for c0 in range(0, f, f_chunk):
            sl = slice(c0, min(c0 + f_chunk, f))
            gate = jnp.dot(xb, wg_ref[:, sl], preferred_element_type=F32)
            up = jnp.dot(xb, wu_ref[:, sl], preferred_element_type=F32)
            acts.append((gate * jax.nn.sigmoid(gate) * up).astype(BF16))
        y = jnp.dot(jnp.concatenate(acts, axis=1), wd_ref[...], preferred_element_type=F32)
        o_ref[rows, :] = _layer_norm(alpha * x + 0.5 * y, g_ref[...], b_ref[...])


def _ffn_ln(x, w_gate, w_up, w_down, ln_g, ln_b, *, alpha, cast_jobs=(), tm=ROW_TILE,
            f_chunk=MXU_DIM):
    n, d = x.shape
    f = w_gate.shape[1]
    steps = n // tm
    row = pl.BlockSpec((tm, d), lambda i: (i, 0))

    def row_block(rows, cols):
        assert rows % steps == 0 and (rows // steps) % BF16_SUBLANES == 0
        return pl.BlockSpec((rows // steps, cols), lambda i: (i, 0))

    job_in_specs = [row_block(*job.src.shape) for job in cast_jobs]
    job_out_widths = [(job.src.shape[0], sum(width for _, width, _ in pieces))
                      for job in cast_jobs for pieces in job.outputs]
    outs = pl.pallas_call(
        functools.partial(_ffn_ln_kernel, alpha=alpha, f_chunk=f_chunk,
                          jobs=tuple(job.outputs for job in cast_jobs)),
        out_shape=(jax.ShapeDtypeStruct((n, d), F32),)
        + tuple(jax.ShapeDtypeStruct(shape, BF16) for shape in job_out_widths),
        grid=(steps,),
        in_specs=[row, _resident((d, f)), _resident((d, f)), _resident((f, d)),
                  _resident((1, d)), _resident((1, d))] + job_in_specs,
        out_specs=(row,) + tuple(row_block(*shape) for shape in job_out_widths),
        compiler_params=_compiler_params(("parallel",)),
        name="ffn_ln",
    )(x, w_gate, w_up, w_down, ln_g.reshape(1, d), ln_b.reshape(1, d),
      *[job.src for job in cast_jobs])
    return outs[0], outs[1:]


def _store_deinterleaved(res, out_ref, dil, t, col0, stage_ref, mid_ref, slot):
    n = res.shape[0]
    n_sub = n // dil
    out_rows = slice(t * n_sub, (t + 1) * n_sub)
    inner = min(dil, SINGLE_OP_STRIDE)
    outer = dil // inner
    part = n // inner
    for h in range(res.shape[1] // LANES):
        cols = slice(col0 + h * LANES, col0 + (h + 1) * LANES)
        stage_ref[slot, h] = res[:, h * LANES:(h + 1) * LANES]
        if outer == 1:
            for r in range(dil):
                out_ref[0, r, out_rows, cols] = (
                    stage_ref[slot, h, pl.ds(r, n_sub, stride=dil), :].astype(BF16))
            continue
        for ra in range(inner):
            mid_ref[slot, h, ra * part:(ra + 1) * part] = (
                stage_ref[slot, h, pl.ds(ra, part, stride=inner), :])
        for ra in range(inner):
            for rb in range(outer):
                out_ref[0, ra + inner * rb, out_rows, cols] = (
                    mid_ref[slot, h, pl.ds(ra * part + rb, n_sub, stride=outer), :].astype(BF16))


def _in_proj_kernel(x_ref, w_ref, b_ref, qkv0_ref, qkv1_ref, qkv2_ref, u_ref, stage_ref, mid_ref):
    gw = 3 * GROUP_WIDTH
    piece = 0
    for t in range(x_ref.shape[0] // SUB_ROWS):
        rows = slice(t * SUB_ROWS, (t + 1) * SUB_ROWS)
        xb = x_ref[rows, :].astype(BF16)

        def proj(c0, xb=xb):
            return (jnp.dot(xb, w_ref[:, c0:c0 + MXU_DIM], preferred_element_type=F32)
                    + b_ref[:, c0:c0 + MXU_DIM])

        for g, out_ref in ((2, qkv2_ref), (1, qkv1_ref)):
            for j in range(gw // MXU_DIM):
                _store_deinterleaved(proj(g * gw + j * MXU_DIM), out_ref, ATTN_PATTERNS[g][1], t,
                                     j * MXU_DIM, stage_ref, mid_ref, piece % STAGE_SLOTS)
                piece += 1

        for j in range(gw // MXU_DIM):
            qkv0_ref[0, 0, rows, j * MXU_DIM:(j + 1) * MXU_DIM] = proj(j * MXU_DIM).astype(BF16)
        for j in range(FOURIER_WIDTH // MXU_DIM):
            u_ref[0, rows, j * MXU_DIM:(j + 1) * MXU_DIM] = (
                proj(N_GROUPS * gw + j * MXU_DIM).astype(BF16))


def _in_proj(h, w, b, *, batch, seq, tm=ROW_TILE):
    n, d = h.shape
    width = w.shape[1]
    gw = 3 * GROUP_WIDTH
    tiles_per_seq = seq // tm
    d1, d2 = ATTN_PATTERNS[1][1], ATTN_PATTERNS[2][1]
    out_shape = (
        jax.ShapeDtypeStruct((batch, 1, seq, gw), BF16),
        jax.ShapeDtypeStruct((batch, d1, seq // d1, gw), BF16),
        jax.ShapeDtypeStruct((batch, d2, seq // d2, gw), BF16),
        jax.ShapeDtypeStruct((batch, seq, FOURIER_WIDTH), BF16),
    )

    def bt(i):
        return i // tiles_per_seq, i % tiles_per_seq

    out_specs = (
        pl.BlockSpec((1, 1, tm, gw), lambda i: (bt(i)[0], 0, bt(i)[1], 0)),
        pl.BlockSpec((1, d1, tm // d1, gw), lambda i: (bt(i)[0], 0, bt(i)[1], 0)),
        pl.BlockSpec((1, d2, tm // d2, gw), lambda i: (bt(i)[0], 0, bt(i)[1], 0)),
        pl.BlockSpec((1, tm, FOURIER_WIDTH), lambda i: (bt(i)[0], bt(i)[1], 0)),
    )
    return pl.pallas_call(
        _in_proj_kernel,
        out_shape=out_shape,
        grid=(n // tm,),
        in_specs=[pl.BlockSpec((tm, d), lambda i: (i, 0)), _resident((d, width)),
                  _resident((1, width))],
        out_specs=out_specs,
        scratch_shapes=[pltpu.VMEM((STAGE_SLOTS, MXU_DIM // LANES, SUB_ROWS, LANES), F32)] * 2,
        compiler_params=_compiler_params(("parallel",)),
        name="in_proj",
    )(h, w, b)


def _t5_bucket_np(rel):
    half = NUM_BUCKETS // 2
    ret = (rel > 0).astype(np.int64) * half
    n = np.abs(rel)
    nf = np.maximum(n, 1).astype(np.float64)
    large = MAX_EXACT + (np.log(nf / MAX_EXACT) / math.log(MAX_DISTANCE / MAX_EXACT)
                         * (half - MAX_EXACT)).astype(np.int64)
    large = np.minimum(large, half - 1)
    return ret + np.where(n < MAX_EXACT, n, large)


def _attention_bias_rows(rel_bias):
    offs = np.arange(-HALF_WIN, HALF_WIN + 1)
    band = []
    for g, (_, dil) in enumerate(ATTN_PATTERNS):
        buckets = _t5_bucket_np(offs * dil)
        heads = rel_bias[:, g * HEADS_PER_GROUP:(g + 1) * HEADS_PER_GROUP]
        band.append(jnp.take(heads, jnp.asarray(buckets, jnp.int32), axis=0).T)
    band = jnp.stack(band).astype(F32) * LOG2_E
    base = jnp.concatenate(
        [band, jnp.full(band.shape[:2] + (BIAS_PERIOD - band.shape[-1],), NEG_INF, F32)], axis=-1)
    rows = [jnp.roll(base, HALF_WIN * (v - 1), axis=-1) for v in range(3)]
    return jnp.stack(rows, axis=2)[:, :, :, None, :]


def _attn_kernel(q0, k0, v0, q1, k1, v1, q2, k2, v2, rows_ref, o_ref, out_ref, lse_ref, bias_ref,
                 *, chunk):
    c = pl.program_id(2)

    @pl.when(c == 0)
    def _():
        for g in range(N_GROUPS):
            for v in range(3):
                row0 = jnp.broadcast_to(rows_ref[g, 0, v], (Q_TILE, BIAS_PERIOD))
                circulant = pltpu.roll(row0, 0, 1, stride=1, stride_axis=0)
                bias_ref[g, v] = circulant[:, :K_TILE]

    for g, (q_ref, k_ref, v_ref) in enumerate(((q0, k0, v0), (q1, k1, v1), (q2, k2, v2))):
        dil = ATTN_PATTERNS[g][1]
        sub_len = k_ref.shape[2]
        rows_per_sub = chunk // dil
        tiles_per_sub = rows_per_sub // Q_TILE

        def tile(i, carry, g=g, dil=dil, sub_len=sub_len, rows_per_sub=rows_per_sub,
                 tiles_per_sub=tiles_per_sub, q_ref=q_ref, k_ref=k_ref, v_ref=v_ref):
            r = i // tiles_per_sub
            t = i % tiles_per_sub
            q_start = c * rows_per_sub + t * Q_TILE
            k_start = jnp.clip(q_start - HALF_WIN, 0, sub_len - K_TILE)
            variant = (q_start - k_start) // HALF_WIN
            k_start = pl.multiple_of(k_start, HALF_WIN)
            q = q_ref[0, r, pl.ds(pl.multiple_of(t * Q_TILE, Q_TILE), Q_TILE), :]
            k = k_ref[0, r, pl.ds(k_start, K_TILE), :]
            v = v_ref[0, r, pl.ds(k_start, K_TILE), :]
            s = lax.dot_general(q, k, (((1,), (1,)), ((), ())), preferred_element_type=F32)
            s = s + bias_ref[g, variant]
            m = jnp.max(s, axis=-1, keepdims=True)
            p = jnp.exp2(s - m)
            den = jnp.sum(p, axis=-1, keepdims=True)
            row0 = t * Q_TILE * dil + r
            rows = pl.ds(row0, Q_TILE) if dil == 1 else pl.ds(row0, Q_TILE, stride=dil)
            out_ref[g, rows, :] = (jnp.dot(p.astype(BF16), v, preferred_element_type=F32)
                                   * (1.0 / den))
            lse_ref[g, rows, :] = jnp.broadcast_to(m + jnp.log2(den), (Q_TILE, LANES))
            return carry

        lax.fori_loop(0, dil * tiles_per_sub, tile, 0, unroll=ATTN_UNROLL)

    blk = 256
    for i in range(chunk // blk):
        rows = slice(i * blk, (i + 1) * blk)
        lses = [lse_ref[g, rows, :] for g in range(N_GROUPS)]
        mx = jnp.maximum(jnp.maximum(lses[0], lses[1]), lses[2])
        ws = [jnp.exp2(l - mx) for l in lses]
        num = ws[0] * out_ref[0, rows, :] + ws[1] * out_ref[1, rows, :] + ws[2] * out_ref[2, rows, :]
        o_ref[0, rows, :] = (num / (ws[0] + ws[1] + ws[2])).astype(BF16)


def _attention(qkv, bias_rows, *, batch, seq, chunk=ATTN_CHUNK):
    in_specs = []
    for g, arr in enumerate(qkv):
        dil = ATTN_PATTERNS[g][1]
        sub_len = seq // dil
        in_specs += [
            pl.BlockSpec((1, dil, chunk // dil, HEAD_DIM), lambda b, h, c: (b, 0, c, h)),
            pl.BlockSpec((1, dil, sub_len, HEAD_DIM),
                         lambda b, h, c: (b, 0, 0, HEADS_PER_GROUP + h)),
            pl.BlockSpec((1, dil, sub_len, HEAD_DIM),
                         lambda b, h, c: (b, 0, 0, 2 * HEADS_PER_GROUP + h)),
        ]
    in_specs.append(pl.BlockSpec((N_GROUPS, 1, 3, 1, BIAS_PERIOD),
                                 lambda b, h, c: (0, h, 0, 0, 0)))
    args = [a for arr in qkv for a in (arr, arr, arr)] + [bias_rows]
    return pl.pallas_call(
        functools.partial(_attn_kernel, chunk=chunk),
        out_shape=jax.ShapeDtypeStruct((batch, seq, GROUP_WIDTH), BF16),
        grid=(batch, HEADS_PER_GROUP, seq // chunk),
        in_specs=in_specs,
        out_specs=pl.BlockSpec((1, chunk, HEAD_DIM), lambda b, h, c: (b, c, h)),
        scratch_shapes=[pltpu.VMEM((N_GROUPS, chunk, LANES), F32)] * 2
        + [pltpu.VMEM((N_GROUPS, 3, Q_TILE, K_TILE), F32)],
        compiler_params=_compiler_params(("parallel", "parallel", "arbitrary")),
        name="attention",
    )(*args)


def _fft_tables(seq):
    assert seq == FFT_N1 * FFT_N2
    k1 = np.arange(FFT_N1)[None, :, None]
    n1 = np.arange(FFT_N1)[None, None, :]
    n2 = np.arange(FFT_N2)[:, None, None]
    ang = 2.0 * np.pi * ((n2 * k1 + FFT_N2 * n1 * k1) % seq) / seq
    t1 = np.concatenate([np.cos(ang), -np.sin(ang)], axis=1)
    k2 = np.arange(FFT_N2)[:, None]
    m2 = np.arange(FFT_N2)[None, :]
    ang2 = 2.0 * np.pi * ((k2 * m2) % FFT_N2) / FFT_N2
    cr, ci = np.cos(ang2), -np.sin(ang2)
    t2 = np.block([[cr, -ci], [ci, cr]])
    cc = np.arange(FOURIER_GROUP_DIM)
    ang3 = 2.0 * np.pi * ((cc[:, None] * cc[None, :]) % FOURIER_GROUP_DIM) / FOURIER_GROUP_DIM
    norm = 1.0 / math.sqrt(seq * FOURIER_GROUP_DIM)
    t3 = np.concatenate([np.cos(ang3), np.sin(ang3)], axis=0) * norm
    return (jnp.asarray(t1, BF16), jnp.asarray(t2, BF16), jnp.asarray(t3, BF16))


def _fft_kernel(u_ref, t1_ref, t2_ref, t3_ref, o_ref, uf_ref, yr_ref, yi_ref, gi_ref):
    slabs = range(FFT_GROUPS_PER_STEP)
    gr_ref = uf_ref

    def lanes(h):
        return slice(h * LANES, (h + 1) * LANES)

    def widen(n1, carry):
        src = pl.multiple_of(n1 * FFT_N2, FFT_N2)
        dst = pl.multiple_of(n1 * FFT_IN_PITCH, 8)
        u = u_ref[0, pl.ds(src, FFT_N2), :].astype(F32)
        for h in slabs:
            uf_ref[h, pl.ds(dst, FFT_N2), :] = u[:, lanes(h)]
        return carry
    lax.fori_loop(0, FFT_N1, widen, 0, unroll=FFT_UNROLL)

    def stage1(n2, carry):
        rows = pl.ds(n2, FFT_N1, stride=FFT_IN_PITCH)
        x = jnp.concatenate([uf_ref[h, rows, :] for h in slabs], axis=1).astype(BF16)
        y = jnp.dot(t1_ref[n2], x, preferred_element_type=F32)
        dst = pl.multiple_of(n2 * FFT_MID_PITCH, 8)
        for h in slabs:
            yr_ref[h, pl.ds(dst, FFT_N1), :] = y[:FFT_N1, lanes(h)]
            yi_ref[h, pl.ds(dst, FFT_N1), :] = y[FFT_N1:, lanes(h)]
        return carry
    lax.fori_loop(0, FFT_N2, stage1, 0, unroll=FFT_UNROLL)

    def stage2(i, carry):
        cols = [(h, pl.ds(FFT_K1_PER_DOT * i + j, FFT_N2, stride=FFT_MID_PITCH))
                for j in range(FFT_K1_PER_DOT) for h in slabs]
        y = jnp.concatenate(
            [jnp.concatenate([yr_ref[h, rows, :], yi_ref[h, rows, :]], axis=0) for h, rows in cols],
            axis=1).astype(BF16)
        gk = jnp.dot(t2_ref[...], y, preferred_element_type=F32)
        for c, (h, rows) in enumerate(cols):
            gr_ref[h, rows, :] = gk[:FFT_N2, lanes(c)]
            gi_ref[h, rows, :] = gk[FFT_N2:, lanes(c)]
        return carry
    lax.fori_loop(0, FFT_N1 // FFT_K1_PER_DOT, stage2, 0, unroll=FFT_UNROLL)

    def stage3(k2, carry):
        src = pl.ds(pl.multiple_of(k2 * FFT_MID_PITCH, 8), FFT_N1)
        dst = pl.ds(pl.multiple_of(k2 * FFT_N1, FFT_N1), FFT_N1)
        for h in slabs:
            gk = jnp.concatenate([gr_ref[h, src, :], gi_ref[h, src, :]], axis=1).astype(BF16)
            out = jnp.dot(gk, t3_ref[...], preferred_element_type=F32)
            o_ref[0, dst, lanes(h)] = out.astype(BF16)
        return carry
    lax.fori_loop(0, FFT_N2, stage3, 0, unroll=FFT_UNROLL)


def _fft(u, *, batch, seq):
    t1, t2, t3 = _fft_tables(seq)
    width = FFT_GROUPS_PER_STEP * FOURIER_GROUP_DIM
    blk = pl.BlockSpec((1, seq, width), lambda b, g: (b, 0, g))
    return pl.pallas_call(
        _fft_kernel,
        out_shape=jax.ShapeDtypeStruct((batch, seq, FOURIER_WIDTH), BF16),
        grid=(batch, FOURIER_WIDTH // width),
        in_specs=[blk, _resident(t1.shape), _resident(t2.shape), _resident(t3.shape)],
        out_specs=blk,
        scratch_shapes=[pltpu.VMEM((FFT_GROUPS_PER_STEP, FFT_N1 * FFT_IN_PITCH, LANES), F32)]
        + [pltpu.VMEM((FFT_GROUPS_PER_STEP, FFT_N2 * FFT_MID_PITCH, LANES), F32)] * 3,
        compiler_params=_compiler_params(("parallel", "parallel")),
        name="fft",
    )(u, t1, t2, t3)


def _merge_ln_kernel(h_ref, a_ref, f_ref, wgate_ref, bgate_ref, wpa_ref, wpf_ref, wout_ref,
                     g_ref, b_ref, o_ref, *, alpha):
    d = h_ref.shape[1]
    for s in range(h_ref.shape[0] // MERGE_SUB_ROWS):
        rows = slice(s * MERGE_SUB_ROWS, (s + 1) * MERGE_SUB_ROWS)
        h = h_ref[rows, :]
        hb = h.astype(BF16)
        a = a_ref[rows, :]
        f = f_ref[rows, :]
        merged = []
        for c in range(d // MERGE_CHUNK):
            ca = slice(c * MERGE_CHUNK, (c + 1) * MERGE_CHUNK)
            cf = slice(d + c * MERGE_CHUNK, d + (c + 1) * MERGE_CHUNK)
            ga = jax.nn.sigmoid(jnp.dot(hb, wgate_ref[:, ca], preferred_element_type=F32)
                                + bgate_ref[:, ca])
            gf = jax.nn.sigmoid(jnp.dot(hb, wgate_ref[:, cf], preferred_element_type=F32)
                                + bgate_ref[:, cf])
            pa = jnp.dot(a, wpa_ref[:, ca], preferred_element_type=F32)
            pf = jnp.dot(f, wpf_ref[:, ca], preferred_element_type=F32)
            merged.append((ga * pa + gf * pf).astype(BF16))
        y = jnp.dot(jnp.concatenate(merged, axis=1), wout_ref[...], preferred_element_type=F32)
        o_ref[rows, :] = _layer_norm(alpha * h + y, g_ref[...], b_ref[...])


def _merge_ln(h, attn, four, w_gate, b_gate, w_pa, w_pf, w_out, ln_g, ln_b, *, alpha,
              tm=ROW_TILE):
    n, d = h.shape
    row = pl.BlockSpec((tm, d), lambda i: (i, 0))
    half = pl.BlockSpec((tm, attn.shape[1]), lambda i: (i, 0))
    return pl.pallas_call(
        functools.partial(_merge_ln_kernel, alpha=alpha),
        out_shape=jax.ShapeDtypeStruct((n, d), F32),
        grid=(n // tm,),
        in_specs=[row, half, half, _resident(w_gate.shape), _resident(b_gate.shape),
                  _resident(w_pa.shape), _resident(w_pf.shape), _resident(w_out.shape),
                  _resident((1, d)), _resident((1, d))],
        out_specs=row,
        compiler_params=_compiler_params(("parallel",)),
        name="merge_ln",
    )(h, attn, four, w_gate, b_gate, w_pa, w_pf, w_out, ln_g.reshape(1, d), ln_b.reshape(1, d))


GATE_COL0 = 3 * ATTN_QKV_WIDTH + FOURIER_WIDTH


def _in_proj_pieces():
    a = ATTN_QKV_WIDTH
    return tuple((part * a + g * GROUP_WIDTH, GROUP_WIDTH,
                  HEAD_DIM ** -0.5 * LOG2_E if part == 0 else 1.0)
                 for g in range(N_GROUPS) for part in range(3)) + ((3 * a, FOURIER_WIDTH, 1.0),)


def _mixer_ln(h, w_qkvu, w_gate, b_in, rel_bias, w_pa, w_pf, w_out, ln_g, ln_b,
              *, alpha, batch, seq):
    b_qkvu = jnp.concatenate([b_in[s:s + w] * f for s, w, f in _in_proj_pieces()]).reshape(1, -1)
    b_gate = b_in[GATE_COL0:].reshape(1, -1)
    qkv0, qkv1, qkv2, u = _in_proj(h, w_qkvu, b_qkvu, batch=batch, seq=seq)
    attn = _attention((qkv0, qkv1, qkv2), _attention_bias_rows(rel_bias), batch=batch, seq=seq)
    four = _fft(u, batch=batch, seq=seq)
    return _merge_ln(h, attn.reshape(batch * seq, -1), four.reshape(batch * seq, -1),
                     w_gate, b_gate, w_pa, w_pf, w_out, ln_g, ln_b, alpha=alpha)


def kernel(x, ln1_g, ln1_b, ffn1_w_gate, ffn1_w_up, ffn1_w_down, w_in, b_in, rel_bias,
           w_proj_attn, w_proj_fourier, w_out, ln2_g, ln2_b, ffn2_w_gate, ffn2_w_up,
           ffn2_w_down, ln3_g, ln3_b):
    batch, seq, d = x.shape
    depth = ln1_g.shape[0]
    alpha = (2 * depth) ** 0.25
    h = x.reshape(batch * seq, d)
    for l in range(depth):
        gates = ((GATE_COL0, w_in.shape[2] - GATE_COL0, 1.0),)
        jobs = (CastJob(w_in[l], (_in_proj_pieces(), gates)),
                _whole(w_proj_attn[l]), _whole(w_proj_fourier[l]), _whole(w_out[l]))
        h, (w_qkvu, w_gate, w_pa, w_pf, w_o) = _ffn_ln(
            h, ffn1_w_gate[l].astype(BF16), ffn1_w_up[l].astype(BF16),
            ffn1_w_down[l].astype(BF16), ln1_g[l], ln1_b[l], alpha=alpha, cast_jobs=jobs)
        h = _mixer_ln(h, w_qkvu, w_gate, b_in[l], rel_bias, w_pa, w_pf, w_o,
                      ln2_g[l], ln2_b[l], alpha=alpha, batch=batch, seq=seq)
        h, _ = _ffn_ln(h, ffn2_w_gate[l].astype(BF16), ffn2_w_up[l].astype(BF16),
                       ffn2_w_down[l].astype(BF16), ln3_g[l], ln3_b[l], alpha=alpha)
    return h.reshape(batch, seq, d)
```

```python
import functools
import math
from typing import NamedTuple

import numpy as np
import jax
import jax.numpy as jnp
from jax import lax
from jax.experimental import pallas as pl
from jax.experimental.pallas import tpu as pltpu

F32 = jnp.float32
BF16 = jnp.bfloat16

HEAD_DIM = 128
HEADS_PER_GROUP = 4
ATTN_PATTERNS = ((128, 1), (512, 4), (2048, 16))
N_GROUPS = len(ATTN_PATTERNS)
GROUP_WIDTH = HEADS_PER_GROUP * HEAD_DIM
ATTN_QKV_WIDTH = N_GROUPS * GROUP_WIDTH
FOURIER_GROUPS = 4
FOURIER_GROUP_DIM = 128
FOURIER_WIDTH = FOURIER_GROUPS * FOURIER_GROUP_DIM
NUM_BUCKETS = 32
MAX_EXACT = 8
MAX_DISTANCE = 1024
NEG_INF = -1e30
LN_EPS = 1e-5
LOG2_E = math.log2(math.e)

LANES = 128
BF16_SUBLANES = 16
MXU_DIM = 256
V7X_VMEM_BYTES = 64 * 1024 * 1024
VMEM_LIMIT_BYTES = V7X_VMEM_BYTES * 7 // 8

ROW_TILE = 1024
SUB_ROWS = 512
MERGE_SUB_ROWS = 256
SINGLE_OP_STRIDE = 4
MERGE_CHUNK = 2 * MXU_DIM
STAGE_SLOTS = 3

HALF_WIN = 64
Q_TILE = 128
K_TILE = Q_TILE + 2 * HALF_WIN
BIAS_PERIOD = Q_TILE + K_TILE
ATTN_CHUNK = 2048
ATTN_UNROLL = 16
FFT_UNROLL = 64

FFT_N1 = 128
FFT_N2 = 64
FFT_K1_PER_DOT = 2
FFT_GROUPS_PER_STEP = 1
FFT_IN_PITCH = FFT_N2 + 8
FFT_MID_PITCH = FFT_N1 + 8


def _compiler_params(semantics):
    return pltpu.CompilerParams(dimension_semantics=semantics,
                                vmem_limit_bytes=VMEM_LIMIT_BYTES)


def _resident(shape):
    zeros = (0,) * len(shape)
    return pl.BlockSpec(shape, lambda *_: zeros, pipeline_mode=pl.Buffered(1))


def _layer_norm(z, g, b):
    mu = jnp.mean(z, axis=-1, keepdims=True)
    zc = z - mu
    var = jnp.mean(zc * zc, axis=-1, keepdims=True)
    return zc * lax.rsqrt(var + LN_EPS) * g + b


class CastJob(NamedTuple):
    src: jax.Array
    outputs: tuple


def _whole(w):
    return CastJob(w, (((0, w.shape[1], 1.0),),))


def _run_cast_jobs(job_outputs, src_refs, out_refs):
    out_refs = iter(out_refs)
    for outputs, src_ref in zip(job_outputs, src_refs):
        for pieces in outputs:
            out_ref = next(out_refs)
            dst = 0
            for col0, width, factor in pieces:
                blk = src_ref[:, col0:col0 + width]
                out_ref[:, dst:dst + width] = (blk if factor == 1.0 else blk * factor).astype(BF16)
                dst += width


def _ffn_ln_kernel(x_ref, wg_ref, wu_ref, wd_ref, g_ref, b_ref, *refs, alpha, f_chunk, jobs,
                   n_main_outs):
    out_refs = refs[len(jobs):len(jobs) + n_main_outs]
    _run_cast_jobs(jobs, refs[:len(jobs)], refs[len(jobs) + n_main_outs:])
    f = wd_ref.shape[0]
    for s in range(x_ref.shape[0] // SUB_ROWS):
        rows = slice(s * SUB_ROWS, (s + 1) * SUB_ROWS)
        x = x_ref[rows, :]
        xb = x.astype(BF16)
        acts = []
        for c0 in range(0, f, f_chunk):
            sl = slice(c0, min(c0 + f_chunk, f))
            gate = jnp.dot(xb, wg_ref[:, sl], preferred_element_type=F32)
            up = jnp.dot(xb, wu_ref[:, sl], preferred_element_type=F32)
            acts.append((gate * jax.nn.sigmoid(gate) * up).astype(BF16))
        y = jnp.dot(jnp.concatenate(acts, axis=1), wd_ref[...], preferred_element_type=F32)
        out = _layer_norm(alpha * x + 0.5 * y, g_ref[...], b_ref[...])
        for out_ref in out_refs:
            out_ref[rows, :] = out.astype(out_ref.dtype)


def _ffn_ln(x, w_gate, w_up, w_down, ln_g, ln_b, *, alpha, cast_jobs=(), with_bf16_copy=False,
            tm=ROW_TILE, f_chunk=MXU_DIM):
    n, d = x.shape
    f = w_gate.shape[1]
    steps = n // tm
    row = pl.BlockSpec((tm, d), lambda i: (i, 0))

    def row_block(rows, cols):
        assert rows % steps == 0 and (rows // steps) % BF16_SUBLANES == 0
        return pl.BlockSpec((rows // steps, cols), lambda i: (i, 0))

    job_in_specs = [row_block(*job.src.shape) for job in cast_jobs]
    job_out_widths = [(job.src.shape[0], sum(width for _, width, _ in pieces))
                      for job in cast_jobs for pieces in job.outputs]
    main_dtypes = (F32, BF16) if with_bf16_copy else (F32,)
    outs = pl.pallas_call(
        functools.partial(_ffn_ln_kernel, alpha=alpha, f_chunk=f_chunk,
                          jobs=tuple(job.outputs for job in cast_jobs),
                          n_main_outs=len(main_dtypes)),
        out_shape=tuple(jax.ShapeDtypeStruct((n, d), dt) for dt in main_dtypes)
        + tuple(jax.ShapeDtypeStruct(shape, BF16) for shape in job_out_widths),
        grid=(steps,),
        in_specs=[row, _resident((d, f)), _resident((d, f)), _resident((f, d)),
                  _resident((1, d)), _resident((1, d))] + job_in_specs,
        out_specs=(row,) * len(main_dtypes)
        + tuple(row_block(*shape) for shape in job_out_widths),
        compiler_params=_compiler_params(("parallel",)),
        name="ffn_ln",
    )(x, w_gate, w_up, w_down, ln_g.reshape(1, d), ln_b.reshape(1, d),
      *[job.src for job in cast_jobs])
    return outs[:len(main_dtypes)], outs[len(main_dtypes):]


def _store_deinterleaved(res, out_ref, dil, t, col0, stage_ref, mid_ref, slot):
    n = res.shape[0]
    n_sub = n // dil
    out_rows = slice(t * n_sub, (t + 1) * n_sub)
    inner = min(dil, SINGLE_OP_STRIDE)
    outer = dil // inner
    part = n // inner
    for h in range(res.shape[1] // LANES):
        cols = slice(col0 + h * LANES, col0 + (h + 1) * LANES)
        stage_ref[slot, h] = res[:, h * LANES:(h + 1) * LANES]
        if outer == 1:
            for r in range(dil):
                out_ref[0, r, out_rows, cols] = (
                    stage_ref[slot, h, pl.ds(r, n_sub, stride=dil), :].astype(BF16))
            continue
        for ra in range(inner):
            mid_ref[slot, h, ra * part:(ra + 1) * part] = (
                stage_ref[slot, h, pl.ds(ra, part, stride=inner), :])
        for ra in range(inner):
            for rb in range(outer):
                out_ref[0, ra + inner * rb, out_rows, cols] = (
                    mid_ref[slot, h, pl.ds(ra * part + rb, n_sub, stride=outer), :].astype(BF16))


def _in_proj_kernel(x_ref, w_ref, b_ref, qkv0_ref, qkv1_ref, qkv2_ref, u_ref, stage_ref, mid_ref):
    gw = 3 * GROUP_WIDTH
    piece = 0
    for t in range(x_ref.shape[0] // SUB_ROWS):
        rows = slice(t * SUB_ROWS, (t + 1) * SUB_ROWS)
        xb = x_ref[rows, :]

        def proj(c0, xb=xb):
            return (jnp.dot(xb, w_ref[:, c0:c0 + MXU_DIM], preferred_element_type=F32)
                    + b_ref[:, c0:c0 + MXU_DIM])

        for g, out_ref in ((2, qkv2_ref), (1, qkv1_ref)):
            for j in range(gw // MXU_DIM):
                _store_deinterleaved(proj(g * gw + j * MXU_DIM), out_ref, ATTN_PATTERNS[g][1], t,
                                     j * MXU_DIM, stage_ref, mid_ref, piece % STAGE_SLOTS)
                piece += 1

        for j in range(gw // MXU_DIM):
            qkv0_ref[0, 0, rows, j * MXU_DIM:(j + 1) * MXU_DIM] = proj(j * MXU_DIM).astype(BF16)
        for j in range(FOURIER_WIDTH // MXU_DIM):
            u_ref[0, rows, j * MXU_DIM:(j + 1) * MXU_DIM] = (
                proj(N_GROUPS * gw + j * MXU_DIM).astype(BF16))


def _in_proj(h, w, b, *, batch, seq, tm=ROW_TILE):
    n, d = h.shape
    width = w.shape[1]
    gw = 3 * GROUP_WIDTH
    tiles_per_seq = seq // tm
    d1, d2 = ATTN_PATTERNS[1][1], ATTN_PATTERNS[2][1]
    out_shape = (
        jax.ShapeDtypeStruct((batch, 1, seq, gw), BF16),
        jax.ShapeDtypeStruct((batch, d1, seq // d1, gw), BF16),
        jax.ShapeDtypeStruct((batch, d2, seq // d2, gw), BF16),
        jax.ShapeDtypeStruct((batch, seq, FOURIER_WIDTH), BF16),
    )

    def bt(i):
        return i // tiles_per_seq, i % tiles_per_seq

    out_specs = (
        pl.BlockSpec((1, 1, tm, gw), lambda i: (bt(i)[0], 0, bt(i)[1], 0)),
        pl.BlockSpec((1, d1, tm // d1, gw), lambda i: (bt(i)[0], 0, bt(i)[1], 0)),
        pl.BlockSpec((1, d2, tm // d2, gw), lambda i: (bt(i)[0], 0, bt(i)[1], 0)),
        pl.BlockSpec((1, tm, FOURIER_WIDTH), lambda i: (bt(i)[0], bt(i)[1], 0)),
    )
    return pl.pallas_call(
        _in_proj_kernel,
        out_shape=out_shape,
        grid=(n // tm,),
        in_specs=[pl.BlockSpec((tm, d), lambda i: (i, 0)), _resident((d, width)),
                  _resident((1, width))],
        out_specs=out_specs,
        scratch_shapes=[pltpu.VMEM((STAGE_SLOTS, MXU_DIM // LANES, SUB_ROWS, LANES), F32)] * 2,
        compiler_params=_compiler_params(("parallel",)),
        name="in_proj",
    )(h, w, b)


def _t5_bucket_np(rel):
    half = NUM_BUCKETS // 2
    ret = (rel > 0).astype(np.int64) * half
    n = np.abs(rel)
    nf = np.maximum(n, 1).astype(np.float64)
    large = MAX_EXACT + (np.log(nf / MAX_EXACT) / math.log(MAX_DISTANCE / MAX_EXACT)
                         * (half - MAX_EXACT)).astype(np.int64)
    large = np.minimum(large, half - 1)
    return ret + np.where(n < MAX_EXACT, n, large)


def _attention_bias_rows(rel_bias):
    offs = np.arange(-HALF_WIN, HALF_WIN + 1)
    band = []
    for g, (_, dil) in enumerate(ATTN_PATTERNS):
        buckets = _t5_bucket_np(offs * dil)
        heads = rel_bias[:, g * HEADS_PER_GROUP:(g + 1) * HEADS_PER_GROUP]
        band.append(jnp.take(heads, jnp.asarray(buckets, jnp.int32), axis=0).T)
    band = jnp.stack(band).astype(F32) * LOG2_E
    base = jnp.concatenate(
        [band, jnp.full(band.shape[:2] + (BIAS_PERIOD - band.shape[-1],), NEG_INF, F32)], axis=-1)
    rows = [jnp.roll(base, HALF_WIN * (v - 1), axis=-1) for v in range(3)]
    return jnp.stack(rows, axis=2)[:, :, :, None, :]


def _attn_kernel(q0, k0, v0, q1, k1, v1, q2, k2, v2, rows_ref, o_ref, out_ref, lse_ref, bias_ref,
                 *, chunk):
    c = pl.program_id(2)

    @pl.when(c == 0)
    def _():
        for g in range(N_GROUPS):
            for v in range(3):
                row0 = jnp.broadcast_to(rows_ref[g, 0, v], (Q_TILE, BIAS_PERIOD))
                circulant = pltpu.roll(row0, 0, 1, stride=1, stride_axis=0)
                bias_ref[g, v] = circulant[:, :K_TILE]

    for g, (q_ref, k_ref, v_ref) in enumerate(((q0, k0, v0), (q1, k1, v1), (q2, k2, v2))):
        dil = ATTN_PATTERNS[g][1]
        sub_len = k_ref.shape[2]
        rows_per_sub = chunk // dil
        tiles_per_sub = rows_per_sub // Q_TILE

        def tile(i, carry, g=g, dil=dil, sub_len=sub_len, rows_per_sub=rows_per_sub,
                 tiles_per_sub=tiles_per_sub, q_ref=q_ref, k_ref=k_ref, v_ref=v_ref):
            r = i // tiles_per_sub
            t = i % tiles_per_sub
            q_start = c * rows_per_sub + t * Q_TILE
            k_start = jnp.clip(q_start - HALF_WIN, 0, sub_len - K_TILE)
            variant = (q_start - k_start) // HALF_WIN
            k_start = pl.multiple_of(k_start, HALF_WIN)
            q = q_ref[0, r, pl.ds(pl.multiple_of(t * Q_TILE, Q_TILE), Q_TILE), :]
            k = k_ref[0, r, pl.ds(k_start, K_TILE), :]
            v = v_ref[0, r, pl.ds(k_start, K_TILE), :]
            s = lax.dot_general(q, k, (((1,), (1,)), ((), ())), preferred_element_type=F32)
            s = s + bias_ref[g, variant]
            m = jnp.max(s, axis=-1, keepdims=True)
            p = jnp.exp2(s - m)
            den = jnp.sum(p, axis=-1, keepdims=True)
            row0 = t * Q_TILE * dil + r
            rows = pl.ds(row0, Q_TILE) if dil == 1 else pl.ds(row0, Q_TILE, stride=dil)
            out_ref[g, rows, :] = (jnp.dot(p.astype(BF16), v, preferred_element_type=F32)
                                   * (1.0 / den))
            lse_ref[g, rows, :] = jnp.broadcast_to(m + jnp.log2(den), (Q_TILE, LANES))
            return carry

        lax.fori_loop(0, dil * tiles_per_sub, tile, 0, unroll=ATTN_UNROLL)

    blk = 256
    for i in range(chunk // blk):
        rows = slice(i * blk, (i + 1) * blk)
        lses = [lse_ref[g, rows, :] for g in range(N_GROUPS)]
        mx = jnp.maximum(jnp.maximum(lses[0], lses[1]), lses[2])
        ws = [jnp.exp2(l - mx) for l in lses]
        num = ws[0] * out_ref[0, rows, :] + ws[1] * out_ref[1, rows, :] + ws[2] * out_ref[2, rows, :]
        o_ref[0, rows, :] = (num / (ws[0] + ws[1] + ws[2])).astype(BF16)


def _attention(qkv, bias_rows, *, batch, seq, chunk=ATTN_CHUNK):
    assert all(window == 2 * HALF_WIN * dil for window, dil in ATTN_PATTERNS)
    assert all(chunk % (dil * Q_TILE) == 0 and seq // dil >= K_TILE for _, dil in ATTN_PATTERNS)
    in_specs = []
    for g, arr in enumerate(qkv):
        dil = ATTN_PATTERNS[g][1]
        sub_len = seq // dil
        in_specs += [
            pl.BlockSpec((1, dil, chunk // dil, HEAD_DIM), lambda b, h, c: (b, 0, c, h)),
            pl.BlockSpec((1, dil, sub_len, HEAD_DIM),
                         lambda b, h, c: (b, 0, 0, HEADS_PER_GROUP + h)),
            pl.BlockSpec((1, dil, sub_len, HEAD_DIM),
                         lambda b, h, c: (b, 0, 0, 2 * HEADS_PER_GROUP + h)),
        ]
    in_specs.append(pl.BlockSpec((N_GROUPS, 1, 3, 1, BIAS_PERIOD),
                                 lambda b, h, c: (0, h, 0, 0, 0)))
    args = [a for arr in qkv for a in (arr, arr, arr)] + [bias_rows]
    return pl.pallas_call(
        functools.partial(_attn_kernel, chunk=chunk),
        out_shape=jax.ShapeDtypeStruct((batch, seq, GROUP_WIDTH), BF16),
        grid=(batch, HEADS_PER_GROUP, seq // chunk),
        in_specs=in_specs,
        out_specs=pl.BlockSpec((1, chunk, HEAD_DIM), lambda b, h, c: (b, c, h)),
        scratch_shapes=[pltpu.VMEM((N_GROUPS, chunk, LANES), F32)] * 2
        + [pltpu.VMEM((N_GROUPS, 3, Q_TILE, K_TILE), F32)],
        compiler_params=_compiler_params(("parallel", "parallel", "arbitrary")),
        name="attention",
    )(*args)


def _fft_tables(seq):
    assert seq == FFT_N1 * FFT_N2
    k1 = np.arange(FFT_N1)[None, :, None]
    n1 = np.arange(FFT_N1)[None, None, :]
    n2 = np.arange(FFT_N2)[:, None, None]
    ang = 2.0 * np.pi * ((n2 * k1 + FFT_N2 * n1 * k1) % seq) / seq
    t1 = np.concatenate([np.cos(ang), -np.sin(ang)], axis=1)
    k2 = np.arange(FFT_N2)[:, None]
    m2 = np.arange(FFT_N2)[None, :]
    ang2 = 2.0 * np.pi * ((k2 * m2) % FFT_N2) / FFT_N2
    cr, ci = np.cos(ang2), -np.sin(ang2)
    t2 = np.block([[cr, -ci], [ci, cr]])
    cc = np.arange(FOURIER_GROUP_DIM)
    ang3 = 2.0 * np.pi * ((cc[:, None] * cc[None, :]) % FOURIER_GROUP_DIM) / FOURIER_GROUP_DIM
    norm = 1.0 / math.sqrt(seq * FOURIER_GROUP_DIM)
    t3 = np.concatenate([np.cos(ang3), np.sin(ang3)], axis=0) * norm
    return (jnp.asarray(t1, BF16), jnp.asarray(t2, BF16), jnp.asarray(t3, BF16))


def _fft_kernel(u_ref, t1_ref, t2_ref, t3_ref, o_ref, uf_ref, yr_ref, yi_ref, gi_ref):
    slabs = range(FFT_GROUPS_PER_STEP)
    gr_ref = uf_ref

    def lanes(h):
        return slice(h * LANES, (h + 1) * LANES)

    def widen(n1, carry):
        src = pl.multiple_of(n1 * FFT_N2, FFT_N2)
        dst = pl.multiple_of(n1 * FFT_IN_PITCH, 8)
        u = u_ref[0, pl.ds(src, FFT_N2), :].astype(F32)
        for h in slabs:
            uf_ref[h, pl.ds(dst, FFT_N2), :] = u[:, lanes(h)]
        return carry
    lax.fori_loop(0, FFT_N1, widen, 0, unroll=FFT_UNROLL)

    def stage1(n2, carry):
        rows = pl.ds(n2, FFT_N1, stride=FFT_IN_PITCH)
        x = jnp.concatenate([uf_ref[h, rows, :] for h in slabs], axis=1).astype(BF16)
        y = jnp.dot(t1_ref[n2], x, preferred_element_type=F32)
        dst = pl.multiple_of(n2 * FFT_MID_PITCH, 8)
        for h in slabs:
            yr_ref[h, pl.ds(dst, FFT_N1), :] = y[:FFT_N1, lanes(h)]
            yi_ref[h, pl.ds(dst, FFT_N1), :] = y[FFT_N1:, lanes(h)]
        return carry
    lax.fori_loop(0, FFT_N2, stage1, 0, unroll=FFT_UNROLL)

    def stage2(i, carry):
        cols = [(h, pl.ds(FFT_K1_PER_DOT * i + j, FFT_N2, stride=FFT_MID_PITCH))
                for j in range(FFT_K1_PER_DOT) for h in slabs]
        y = jnp.concatenate(
            [jnp.concatenate([yr_ref[h, rows, :], yi_ref[h, rows, :]], axis=0) for h, rows in cols],
            axis=1).astype(BF16)
        gk = jnp.dot(t2_ref[...], y, preferred_element_type=F32)
        for c, (h, rows) in enumerate(cols):
            gr_ref[h, rows, :] = gk[:FFT_N2, lanes(c)]
            gi_ref[h, rows, :] = gk[FFT_N2:, lanes(c)]
        return carry
    lax.fori_loop(0, FFT_N1 // FFT_K1_PER_DOT, stage2, 0, unroll=FFT_UNROLL)

    def stage3(k2, carry):
        src = pl.ds(pl.multiple_of(k2 * FFT_MID_PITCH, 8), FFT_N1)
        dst = pl.ds(pl.multiple_of(k2 * FFT_N1, FFT_N1), FFT_N1)
        for h in slabs:
            gk = jnp.concatenate([gr_ref[h, src, :], gi_ref[h, src, :]], axis=1).astype(BF16)
            out = jnp.dot(gk, t3_ref[...], preferred_element_type=F32)
            o_ref[0, dst, lanes(h)] = out.astype(BF16)
        return carry
    lax.fori_loop(0, FFT_N2, stage3, 0, unroll=FFT_UNROLL)


def _fft(u, *, batch, seq):
    t1, t2, t3 = _fft_tables(seq)
    width = FFT_GROUPS_PER_STEP * FOURIER_GROUP_DIM
    blk = pl.BlockSpec((1, seq, width), lambda b, g: (b, 0, g))
    return pl.pallas_call(
        _fft_kernel,
        out_shape=jax.ShapeDtypeStruct((batch, seq, FOURIER_WIDTH), BF16),
        grid=(batch, FOURIER_WIDTH // width),
        in_specs=[blk, _resident(t1.shape), _resident(t2.shape), _resident(t3.shape)],
        out_specs=blk,
        scratch_shapes=[pltpu.VMEM((FFT_GROUPS_PER_STEP, FFT_N1 * FFT_IN_PITCH, LANES), F32)]
        + [pltpu.VMEM((FFT_GROUPS_PER_STEP, FFT_N2 * FFT_MID_PITCH, LANES), F32)] * 3,
        compiler_params=_compiler_params(("parallel", "parallel")),
        name="fft",
    )(u, t1, t2, t3)


def _merge_ln_kernel(h_ref, a_ref, f_ref, wgate_ref, bgate_ref, wpa_ref, wpf_ref, wout_ref,
                     g_ref, b_ref, o_ref, *, alpha):
    d = h_ref.shape[1]
    for s in range(h_ref.shape[0] // MERGE_SUB_ROWS):
        rows = slice(s * MERGE_SUB_ROWS, (s + 1) * MERGE_SUB_ROWS)
        h = h_ref[rows, :]
        hb = h.astype(BF16)
        a = a_ref[rows, :]
        f = f_ref[rows, :]
        merged = []
        for c in range(d // MERGE_CHUNK):
            ca = slice(c * MERGE_CHUNK, (c + 1) * MERGE_CHUNK)
            cf = slice(d + c * MERGE_CHUNK, d + (c + 1) * MERGE_CHUNK)
            ga = jax.nn.sigmoid(jnp.dot(hb, wgate_ref[:, ca], preferred_element_type=F32)
                                + bgate_ref[:, ca])
            gf = jax.nn.sigmoid(jnp.dot(hb, wgate_ref[:, cf], preferred_element_type=F32)
                                + bgate_ref[:, cf])
            pa = jnp.dot(a, wpa_ref[:, ca], preferred_element_type=F32)
            pf = jnp.dot(f, wpf_ref[:, ca], preferred_element_type=F32)
            merged.append((ga * pa + gf * pf).astype(BF16))
        y = jnp.dot(jnp.concatenate(merged, axis=1), wout_ref[...], preferred_element_type=F32)
        o_ref[rows, :] = _layer_norm(alpha * h + y, g_ref[...], b_ref[...])


def _merge_ln(h, attn, four, w_gate, b_gate, w_pa, w_pf, w_out, ln_g, ln_b, *, alpha,
              tm=ROW_TILE):
    n, d = h.shape
    row = pl.BlockSpec((tm, d), lambda i: (i, 0))
    half = pl.BlockSpec((tm, attn.shape[1]), lambda i: (i, 0))
    return pl.pallas_call(
        functools.partial(_merge_ln_kernel, alpha=alpha),
        out_shape=jax.ShapeDtypeStruct((n, d), F32),
        grid=(n // tm,),
        in_specs=[row, half, half, _resident(w_gate.shape), _resident(b_gate.shape),
                  _resident(w_pa.shape), _resident(w_pf.shape), _resident(w_out.shape),
                  _resident((1, d)), _resident((1, d))],
        out_specs=row,
        compiler_params=_compiler_params(("parallel",)),
        name="merge_ln",
    )(h, attn, four, w_gate, b_gate, w_pa, w_pf, w_out, ln_g.reshape(1, d), ln_b.reshape(1, d))


GATE_COL0 = 3 * ATTN_QKV_WIDTH + FOURIER_WIDTH


def _in_proj_pieces():
    a = ATTN_QKV_WIDTH
    return tuple((part * a + g * GROUP_WIDTH, GROUP_WIDTH,
                  HEAD_DIM ** -0.5 * LOG2_E if part == 0 else 1.0)
                 for g in range(N_GROUPS) for part in range(3)) + ((3 * a, FOURIER_WIDTH, 1.0),)


def _mixer_ln(h, h_bf16, w_qkvu, w_gate, b_in, rel_bias, w_pa, w_pf, w_out, ln_g, ln_b,
              *, alpha, batch, seq):
    b_qkvu = jnp.concatenate([b_in[s:s + w] * f for s, w, f in _in_proj_pieces()]).reshape(1, -1)
    b_gate = b_in[GATE_COL0:].reshape(1, -1)
    qkv0, qkv1, qkv2, u = _in_proj(h_bf16, w_qkvu, b_qkvu, batch=batch, seq=seq)
    attn = _attention((qkv0, qkv1, qkv2), _attention_bias_rows(rel_bias), batch=batch, seq=seq)
    four = _fft(u, batch=batch, seq=seq)
    return _merge_ln(h, attn.reshape(batch * seq, -1), four.reshape(batch * seq, -1),
                     w_gate, b_gate, w_pa, w_pf, w_out, ln_g, ln_b, alpha=alpha)


def kernel(x, ln1_g, ln1_b, ffn1_w_gate, ffn1_w_up, ffn1_w_down, w_in, b_in, rel_bias,
           w_proj_attn, w_proj_fourier, w_out, ln2_g, ln2_b, ffn2_w_gate, ffn2_w_up,
           ffn2_w_down, ln3_g, ln3_b):
    batch, seq, d = x.shape
    depth = ln1_g.shape[0]
    assert seq % ROW_TILE == 0 and d % MXU_DIM == 0
    assert w_in.shape[2] == GATE_COL0 + 2 * d
    assert rel_bias.shape == (NUM_BUCKETS, N_GROUPS * HEADS_PER_GROUP)
    alpha = (2 * depth) ** 0.25
    h = x.reshape(batch * seq, d)
    for l in range(depth):
        gates = ((GATE_COL0, w_in.shape[2] - GATE_COL0, 1.0),)
        jobs = (CastJob(w_in[l], (_in_proj_pieces(), gates)),
                _whole(w_proj_attn[l]), _whole(w_proj_fourier[l]), _whole(w_out[l]))
        (h, h_bf16), (w_qkvu, w_gate, w_pa, w_pf, w_o) = _ffn_ln(
            h, ffn1_w_gate[l].astype(BF16), ffn1_w_up[l].astype(BF16),
            ffn1_w_down[l].astype(BF16), ln1_g[l], ln1_b[l], alpha=alpha, cast_jobs=jobs,
            with_bf16_copy=True)
        h = _mixer_ln(h, h_bf16, w_qkvu, w_gate, b_in[l], rel_bias, w_pa, w_pf, w_o,
                      ln2_g[l], ln2_b[l], alpha=alpha, batch=batch, seq=seq)
        (h,), _ = _ffn_ln(h, ffn2_w_gate[l].astype(BF16), ffn2_w_up[l].astype(BF16),
                          ffn2_w_down[l].astype(BF16), ln3_g[l], ln3_b[l], alpha=alpha)
    return h.reshape(batch, seq, d)
```

```python
import functools
import math
from typing import NamedTuple

import numpy as np
import jax
import jax.numpy as jnp
from jax import lax
from jax.experimental import pallas as pl
from jax.experimental.pallas import tpu as pltpu

F32 = jnp.float32
BF16 = jnp.bfloat16

HEAD_DIM = 128
HEADS_PER_GROUP = 4
ATTN_PATTERNS = ((128, 1), (512, 4), (2048, 16))
N_GROUPS = len(ATTN_PATTERNS)
GROUP_WIDTH = HEADS_PER_GROUP * HEAD_DIM
ATTN_QKV_WIDTH = N_GROUPS * GROUP_WIDTH
FOURIER_GROUPS = 4
FOURIER_GROUP_DIM = 128
FOURIER_WIDTH = FOURIER_GROUPS * FOURIER_GROUP_DIM
NUM_BUCKETS = 32
MAX_EXACT = 8
MAX_DISTANCE = 1024
NEG_INF = -1e30
LN_EPS = 1e-5
LOG2_E = math.log2(math.e)

LANES = 128
BF16_SUBLANES = 16
MXU_DIM = 256
V7X_VMEM_BYTES = 64 * 1024 * 1024
VMEM_LIMIT_BYTES = V7X_VMEM_BYTES * 7 // 8

ROW_TILE = 1024
SUB_ROWS = 512
MERGE_SUB_ROWS = 256
SINGLE_OP_STRIDE = 4
MERGE_CHUNK = 2 * MXU_DIM
STAGE_SLOTS = 3

HALF_WIN = 64
Q_TILE = 128
K_TILE = Q_TILE + 2 * HALF_WIN
BIAS_PERIOD = Q_TILE + K_TILE
ATTN_CHUNK = 2048
ATTN_UNROLL = 16
FFT_UNROLL = 64

FFT_N1 = 128
FFT_N2 = 64
FFT_K1_PER_DOT = 2
FFT_GROUPS_PER_STEP = 1
FFT_IN_PITCH = FFT_N2 + 8
FFT_MID_PITCH = FFT_N1 + 8


def _compiler_params(semantics):
    return pltpu.CompilerParams(dimension_semantics=semantics,
                                vmem_limit_bytes=VMEM_LIMIT_BYTES)


def _resident(shape):
    zeros = (0,) * len(shape)
    return pl.BlockSpec(shape, lambda *_: zeros, pipeline_mode=pl.Buffered(1))


def _layer_norm(z, g, b):
    mu = jnp.mean(z, axis=-1, keepdims=True)
    zc = z - mu
    var = jnp.mean(zc * zc, axis=-1, keepdims=True)
    return zc * lax.rsqrt(var + LN_EPS) * g + b


class CastJob(NamedTuple):
    src: jax.Array
    outputs: tuple


def _whole(w):
    return CastJob(w, (((0, w.shape[1], 1.0),),))


def _run_cast_jobs(job_outputs, src_refs, out_refs):
    out_refs = iter(out_refs)
    for outputs, src_ref in zip(job_outputs, src_refs):
        for pieces in outputs:
            out_ref = next(out_refs)
            dst = 0
            for col0, width, factor in pieces:
                blk = src_ref[:, col0:col0 + width]
                out_ref[:, dst:dst + width] = (blk if factor == 1.0 else blk * factor).astype(BF16)
                dst += width


def _ffn_ln_kernel(x_ref, wg_ref, wu_ref, wd_ref, g_ref, b_ref, *refs, alpha, f_chunk, jobs):
    o_ref = refs[len(jobs)]
    _run_cast_jobs(jobs, refs[:len(jobs)], refs[len(jobs) + 1:])
    f = wd_ref.shape[0]
    for s in range(x_ref.shape[0] // SUB_ROWS):
        rows = slice(s * SUB_ROWS, (s + 1) * SUB_ROWS)
        x = x_ref[rows, :]
        xb = x.astype(BF16)
        acts = []
        for c0 in range(0, f, f_chunk):
            sl = slice(c0, min(c0 + f_chunk, f))
            gate = jnp.dot(xb, wg_ref[:, sl], preferred_element_type=F32)
            up = jnp.dot(xb, wu_ref[:, sl], preferred_element_type=F32)
            acts.append((gate * jax.nn.sigmoid(gate) * up).astype(BF16))
        y = jnp.dot(jnp.concatenate(acts, axis=1), wd_ref[...], preferred_element_type=F32)
        o_ref[rows, :] = _layer_norm(alpha * x + 0.5 * y, g_ref[...], b_ref[...])


def _ffn_ln(x, w_gate, w_up, w_down, ln_g, ln_b, *, alpha, cast_jobs=(), tm=ROW_TILE,
            f_chunk=MXU_DIM):
    n, d = x.shape
    f = w_gate.shape[1]
    steps = n // tm
    row = pl.BlockSpec((tm, d), lambda i: (i, 0))

    def row_block(rows, cols):
        assert rows % steps == 0 and (rows // steps) % BF16_SUBLANES == 0
        return pl.BlockSpec((rows // steps, cols), lambda i: (i, 0))

    job_in_specs = [row_block(*job.src.shape) for job in cast_jobs]
    job_out_widths = [(job.src.shape[0], sum(width for _, width, _ in pieces))
                      for job in cast_jobs for pieces in job.outputs]
    outs = pl.pallas_call(
        functools.partial(_ffn_ln_kernel, alpha=alpha, f_chunk=f_chunk,
                          jobs=tuple(job.outputs for job in cast_jobs)),
        out_shape=(jax.ShapeDtypeStruct((n, d), F32),)
        + tuple(jax.ShapeDtypeStruct(shape, BF16) for shape in job_out_widths),
        grid=(steps,),
        in_specs=[row, _resident((d, f)), _resident((d, f)), _resident((f, d)),
                  _resident((1, d)), _resident((1, d))] + job_in_specs,
        out_specs=(row,) + tuple(row_block(*shape) for shape in job_out_widths),
        compiler_params=_compiler_params(("parallel",)),
        name="ffn_ln",
    )(x, w_gate, w_up, w_down, ln_g.reshape(1, d), ln_b.reshape(1, d),
      *[job.src for job in cast_jobs])
    return outs[0], outs[1:]


def _store_deinterleaved(res, out_ref, dil, t, col0, stage_ref, mid_ref, slot):
    n = res.shape[0]
    n_sub = n // dil
    out_rows = slice(t * n_sub, (t + 1) * n_sub)
    inner = min(dil, SINGLE_OP_STRIDE)
    outer = dil // inner
    part = n // inner
    for h in range(res.shape[1] // LANES):
        cols = slice(col0 + h * LANES, col0 + (h + 1) * LANES)
        stage_ref[slot, h] = res[:, h * LANES:(h + 1) * LANES]
        if outer == 1:
            for r in range(dil):
                out_ref[0, r, out_rows, cols] = (
                    stage_ref[slot, h, pl.ds(r, n_sub, stride=dil), :].astype(BF16))
            continue
        for ra in range(inner):
            mid_ref[slot, h, ra * part:(ra + 1) * part] = (
                stage_ref[slot, h, pl.ds(ra, part, stride=inner), :])
        for ra in range(inner):
            for rb in range(outer):
                out_ref[0, ra + inner * rb, out_rows, cols] = (
                    mid_ref[slot, h, pl.ds(ra * part + rb, n_sub, stride=outer), :].astype(BF16))


def _in_proj_kernel(x_ref, w_ref, b_ref, qkv0_ref, qkv1_ref, qkv2_ref, u_ref, stage_ref, mid_ref):
    gw = 3 * GROUP_WIDTH
    piece = 0
    for t in range(x_ref.shape[0] // SUB_ROWS):
        rows = slice(t * SUB_ROWS, (t + 1) * SUB_ROWS)
        xb = x_ref[rows, :].astype(BF16)

        def proj(c0, xb=xb):
            return (jnp.dot(xb, w_ref[:, c0:c0 + MXU_DIM], preferred_element_type=F32)
                    + b_ref[:, c0:c0 + MXU_DIM])

        for g, out_ref in ((2, qkv2_ref), (1, qkv1_ref)):
            for j in range(gw // MXU_DIM):
                _store_deinterleaved(proj(g * gw + j * MXU_DIM), out_ref, ATTN_PATTERNS[g][1], t,
                                     j * MXU_DIM, stage_ref, mid_ref, piece % STAGE_SLOTS)
                piece += 1

        for j in range(gw // MXU_DIM):
            qkv0_ref[0, 0, rows, j * MXU_DIM:(j + 1) * MXU_DIM] = proj(j * MXU_DIM).astype(BF16)
        for j in range(FOURIER_WIDTH // MXU_DIM):
            u_ref[0, rows, j * MXU_DIM:(j + 1) * MXU_DIM] = (
                proj(N_GROUPS * gw + j * MXU_DIM).astype(BF16))


def _in_proj(h, w, b, *, batch, seq, tm=ROW_TILE):
    n, d = h.shape
    width = w.shape[1]
    gw = 3 * GROUP_WIDTH
    tiles_per_seq = seq // tm
    d1, d2 = ATTN_PATTERNS[1][1], ATTN_PATTERNS[2][1]
    out_shape = (
        jax.ShapeDtypeStruct((batch, 1, seq, gw), BF16),
        jax.ShapeDtypeStruct((batch, d1, seq // d1, gw), BF16),
        jax.ShapeDtypeStruct((batch, d2, seq // d2, gw), BF16),
        jax.ShapeDtypeStruct((batch, seq, FOURIER_WIDTH), BF16),
    )

    def bt(i):
        return i // tiles_per_seq, i % tiles_per_seq

    out_specs = (
        pl.BlockSpec((1, 1, tm, gw), lambda i: (bt(i)[0], 0, bt(i)[1], 0)),
        pl.BlockSpec((1, d1, tm // d1, gw), lambda i: (bt(i)[0], 0, bt(i)[1], 0)),
        pl.BlockSpec((1, d2, tm // d2, gw), lambda i: (bt(i)[0], 0, bt(i)[1], 0)),
        pl.BlockSpec((1, tm, FOURIER_WIDTH), lambda i: (bt(i)[0], bt(i)[1], 0)),
    )
    return pl.pallas_call(
        _in_proj_kernel,
        out_shape=out_shape,
        grid=(n // tm,),
        in_specs=[pl.BlockSpec((tm, d), lambda i: (i, 0)), _resident((d, width)),
                  _resident((1, width))],
        out_specs=out_specs,
        scratch_shapes=[pltpu.VMEM((STAGE_SLOTS, MXU_DIM // LANES, SUB_ROWS, LANES), F32)] * 2,
        compiler_params=_compiler_params(("parallel",)),
        name="in_proj",
    )(h, w, b)


def _t5_bucket_np(rel):
    half = NUM_BUCKETS // 2
    ret = (rel > 0).astype(np.int64) * half
    n = np.abs(rel)
    nf = np.maximum(n, 1).astype(np.float64)
    large = MAX_EXACT + (np.log(nf / MAX_EXACT) / math.log(MAX_DISTANCE / MAX_EXACT)
                         * (half - MAX_EXACT)).astype(np.int64)
    large = np.minimum(large, half - 1)
    return ret + np.where(n < MAX_EXACT, n, large)


def _attention_bias_rows(rel_bias):
    offs = np.arange(-HALF_WIN, HALF_WIN + 1)
    band = []
    for g, (_, dil) in enumerate(ATTN_PATTERNS):
        buckets = _t5_bucket_np(offs * dil)
        heads = rel_bias[:, g * HEADS_PER_GROUP:(g + 1) * HEADS_PER_GROUP]
        band.append(jnp.take(heads, jnp.asarray(buckets, jnp.int32), axis=0).T)
    band = jnp.stack(band).astype(F32) * LOG2_E
    base = jnp.concatenate(
        [band, jnp.full(band.shape[:2] + (BIAS_PERIOD - band.shape[-1],), NEG_INF, F32)], axis=-1)
    rows = [jnp.roll(base, HALF_WIN * (v - 1), axis=-1) for v in range(3)]
    return jnp.stack(rows, axis=2)[:, :, :, None, :]


def _attn_kernel(q0, k0, v0, q1, k1, v1, q2, k2, v2, rows_ref, o_ref, out_ref, lse_ref, bias_ref,
                 *, chunk):
    c = pl.program_id(2)

    @pl.when(c == 0)
    def _():
        for g in range(N_GROUPS):
            for v in range(3):
                row0 = jnp.broadcast_to(rows_ref[g, 0, v], (Q_TILE, BIAS_PERIOD))
                circulant = pltpu.roll(row0, 0, 1, stride=1, stride_axis=0)
                bias_ref[g, v] = circulant[:, :K_TILE]

    for g, (q_ref, k_ref, v_ref) in enumerate(((q0, k0, v0), (q1, k1, v1), (q2, k2, v2))):
        dil = ATTN_PATTERNS[g][1]
        sub_len = k_ref.shape[2]
        rows_per_sub = chunk // dil
        tiles_per_sub = rows_per_sub // Q_TILE

        def tile(i, carry, g=g, dil=dil, sub_len=sub_len, rows_per_sub=rows_per_sub,
                 tiles_per_sub=tiles_per_sub, q_ref=q_ref, k_ref=k_ref, v_ref=v_ref):
            r = i // tiles_per_sub
            t = i % tiles_per_sub
            q_start = c * rows_per_sub + t * Q_TILE
            k_start = jnp.clip(q_start - HALF_WIN, 0, sub_len - K_TILE)
            variant = (q_start - k_start) // HALF_WIN
            k_start = pl.multiple_of(k_start, HALF_WIN)
            q = q_ref[0, r, pl.ds(pl.multiple_of(t * Q_TILE, Q_TILE), Q_TILE), :]
            k = k_ref[0, r, pl.ds(k_start, K_TILE), :]
            v = v_ref[0, r, pl.ds(k_start, K_TILE), :]
            s = lax.dot_general(q, k, (((1,), (1,)), ((), ())), preferred_element_type=F32)
            s = s + bias_ref[g, variant]
            m = jnp.max(s, axis=-1, keepdims=True)
            p = jnp.exp2(s - m)
            den = jnp.sum(p, axis=-1, keepdims=True)
            row0 = t * Q_TILE * dil + r
            rows = pl.ds(row0, Q_TILE) if dil == 1 else pl.ds(row0, Q_TILE, stride=dil)
            out_ref[g, rows, :] = (jnp.dot(p.astype(BF16), v, preferred_element_type=F32)
                                   * (1.0 / den))
            lse_ref[g, rows, :] = jnp.broadcast_to(m + jnp.log2(den), (Q_TILE, LANES))
            return carry

        lax.fori_loop(0, dil * tiles_per_sub, tile, 0, unroll=ATTN_UNROLL)

    blk = 256
    for i in range(chunk // blk):
        rows = slice(i * blk, (i + 1) * blk)
        lses = [lse_ref[g, rows, :] for g in range(N_GROUPS)]
        mx = jnp.maximum(jnp.maximum(lses[0], lses[1]), lses[2])
        ws = [jnp.exp2(l - mx) for l in lses]
        num = ws[0] * out_ref[0, rows, :] + ws[1] * out_ref[1, rows, :] + ws[2] * out_ref[2, rows, :]
        o_ref[0, rows, :] = (num / (ws[0] + ws[1] + ws[2])).astype(BF16)


def _attention(qkv, bias_rows, *, batch, seq, chunk=ATTN_CHUNK):
    assert all(window == 2 * HALF_WIN * dil for window, dil in ATTN_PATTERNS)
    assert all(chunk % (dil * Q_TILE) == 0 and seq // dil >= K_TILE for _, dil in ATTN_PATTERNS)
    in_specs = []
    for g, arr in enumerate(qkv):
        dil = ATTN_PATTERNS[g][1]
        sub_len = seq // dil
        in_specs += [
            pl.BlockSpec((1, dil, chunk // dil, HEAD_DIM), lambda b, h, c: (b, 0, c, h)),
            pl.BlockSpec((1, dil, sub_len, HEAD_DIM),
                         lambda b, h, c: (b, 0, 0, HEADS_PER_GROUP + h)),
            pl.BlockSpec((1, dil, sub_len, HEAD_DIM),
                         lambda b, h, c: (b, 0, 0, 2 * HEADS_PER_GROUP + h)),
        ]
    in_specs.append(pl.BlockSpec((N_GROUPS, 1, 3, 1, BIAS_PERIOD),
                                 lambda b, h, c: (0, h, 0, 0, 0)))
    args = [a for arr in qkv for a in (arr, arr, arr)] + [bias_rows]
    return pl.pallas_call(
        functools.partial(_attn_kernel, chunk=chunk),
        out_shape=jax.ShapeDtypeStruct((batch, seq, GROUP_WIDTH), BF16),
        grid=(batch, HEADS_PER_GROUP, seq // chunk),
        in_specs=in_specs,
        out_specs=pl.BlockSpec((1, chunk, HEAD_DIM), lambda b, h, c: (b, c, h)),
        scratch_shapes=[pltpu.VMEM((N_GROUPS, chunk, LANES), F32)] * 2
        + [pltpu.VMEM((N_GROUPS, 3, Q_TILE, K_TILE), F32)],
        compiler_params=_compiler_params(("parallel", "parallel", "arbitrary")),
        name="attention",
    )(*args)


def _fft_tables(seq):
    assert seq == FFT_N1 * FFT_N2
    k1 = np.arange(FFT_N1)[None, :, None]
    n1 = np.arange(FFT_N1)[None, None, :]
    n2 = np.arange(FFT_N2)[:, None, None]
    ang = 2.0 * np.pi * ((n2 * k1 + FFT_N2 * n1 * k1) % seq) / seq
    t1 = np.concatenate([np.cos(ang), -np.sin(ang)], axis=1)
    k2 = np.arange(FFT_N2)[:, None]
    m2 = np.arange(FFT_N2)[None, :]
    ang2 = 2.0 * np.pi * ((k2 * m2) % FFT_N2) / FFT_N2
    cr, ci = np.cos(ang2), -np.sin(ang2)
    t2 = np.block([[cr, -ci], [ci, cr]])
    cc = np.arange(FOURIER_GROUP_DIM)
    ang3 = 2.0 * np.pi * ((cc[:, None] * cc[None, :]) % FOURIER_GROUP_DIM) / FOURIER_GROUP_DIM
    norm = 1.0 / math.sqrt(seq * FOURIER_GROUP_DIM)
    t3 = np.concatenate([np.cos(ang3), np.sin(ang3)], axis=0) * norm
    return (jnp.asarray(t1, BF16), jnp.asarray(t2, BF16), jnp.asarray(t3, BF16))


def _fft_kernel(u_ref, t1_ref, t2_ref, t3_ref, o_ref, uf_ref, yr_ref, yi_ref, gi_ref):
    slabs = range(FFT_GROUPS_PER_STEP)
    gr_ref = uf_ref

    def lanes(h):
        return slice(h * LANES, (h + 1) * LANES)

    def widen(n1, carry):
        src = pl.multiple_of(n1 * FFT_N2, FFT_N2)
        dst = pl.multiple_of(n1 * FFT_IN_PITCH, 8)
        u = u_ref[0, pl.ds(src, FFT_N2), :].astype(F32)
        for h in slabs:
            uf_ref[h, pl.ds(dst, FFT_N2), :] = u[:, lanes(h)]
        return carry
    lax.fori_loop(0, FFT_N1, widen, 0, unroll=FFT_UNROLL)

    def stage1(n2, carry):
        rows = pl.ds(n2, FFT_N1, stride=FFT_IN_PITCH)
        x = jnp.concatenate([uf_ref[h, rows, :] for h in slabs], axis=1).astype(BF16)
        y = jnp.dot(t1_ref[n2], x, preferred_element_type=F32)
        dst = pl.multiple_of(n2 * FFT_MID_PITCH, 8)
        for h in slabs:
            yr_ref[h, pl.ds(dst, FFT_N1), :] = y[:FFT_N1, lanes(h)]
            yi_ref[h, pl.ds(dst, FFT_N1), :] = y[FFT_N1:, lanes(h)]
        return carry
    lax.fori_loop(0, FFT_N2, stage1, 0, unroll=FFT_UNROLL)

    def stage2(i, carry):
        cols = [(h, pl.ds(FFT_K1_PER_DOT * i + j, FFT_N2, stride=FFT_MID_PITCH))
                for j in range(FFT_K1_PER_DOT) for h in slabs]
        y = jnp.concatenate(
            [jnp.concatenate([yr_ref[h, rows, :], yi_ref[h, rows, :]], axis=0) for h, rows in cols],
            axis=1).astype(BF16)
        gk = jnp.dot(t2_ref[...], y, preferred_element_type=F32)
        for c, (h, rows) in enumerate(cols):
            gr_ref[h, rows, :] = gk[:FFT_N2, lanes(c)]
            gi_ref[h, rows, :] = gk[FFT_N2:, lanes(c)]
        return carry
    lax.fori_loop(0, FFT_N1 // FFT_K1_PER_DOT, stage2, 0, unroll=FFT_UNROLL)

    def stage3(k2, carry):
        src = pl.ds(pl.multiple_of(k2 * FFT_MID_PITCH, 8), FFT_N1)
        dst = pl.ds(pl.multiple_of(k2 * FFT_N1, FFT_N1), FFT_N1)
        for h in slabs:
            gk = jnp.concatenate([gr_ref[h, src, :], gi_ref[h, src, :]], axis=1).astype(BF16)
            out = jnp.dot(gk, t3_ref[...], preferred_element_type=F32)
            o_ref[0, dst, lanes(h)] = out.astype(BF16)
        return carry
    lax.fori_loop(0, FFT_N2, stage3, 0, unroll=FFT_UNROLL)


def _fft(u, *, batch, seq):
    t1, t2, t3 = _fft_tables(seq)
    width = FFT_GROUPS_PER_STEP * FOURIER_GROUP_DIM
    blk = pl.BlockSpec((1, seq, width), lambda b, g: (b, 0, g))
    return pl.pallas_call(
        _fft_kernel,
        out_shape=jax.ShapeDtypeStruct((batch, seq, FOURIER_WIDTH), BF16),
        grid=(batch, FOURIER_WIDTH // width),
        in_specs=[blk, _resident(t1.shape), _resident(t2.shape), _resident(t3.shape)],
        out_specs=blk,
        scratch_shapes=[pltpu.VMEM((FFT_GROUPS_PER_STEP, FFT_N1 * FFT_IN_PITCH, LANES), F32)]
        + [pltpu.VMEM((FFT_GROUPS_PER_STEP, FFT_N2 * FFT_MID_PITCH, LANES), F32)] * 3,
        compiler_params=_compiler_params(("parallel", "parallel")),
        name="fft",
    )(u, t1, t2, t3)


def _merge_ln_kernel(h_ref, a_ref, f_ref, wgate_ref, bgate_ref, wpa_ref, wpf_ref, wout_ref,
                     g_ref, b_ref, o_ref, *, alpha):
    d = h_ref.shape[1]
    for s in range(h_ref.shape[0] // MERGE_SUB_ROWS):
        rows = slice(s * MERGE_SUB_ROWS, (s + 1) * MERGE_SUB_ROWS)
        h = h_ref[rows, :]
        hb = h.astype(BF16)
        a = a_ref[rows, :]
        f = f_ref[rows, :]
        merged = []
        for c in range(d // MERGE_CHUNK):
            ca = slice(c * MERGE_CHUNK, (c + 1) * MERGE_CHUNK)
            cf = slice(d + c * MERGE_CHUNK, d + (c + 1) * MERGE_CHUNK)
            ga = jax.nn.sigmoid(jnp.dot(hb, wgate_ref[:, ca], preferred_element_type=F32)
                                + bgate_ref[:, ca])
            gf = jax.nn.sigmoid(jnp.dot(hb, wgate_ref[:, cf], preferred_element_type=F32)
                                + bgate_ref[:, cf])
            pa = jnp.dot(a, wpa_ref[:, ca], preferred_element_type=F32)
            pf = jnp.dot(f, wpf_ref[:, ca], preferred_element_type=F32)
            merged.append((ga * pa + gf * pf).astype(BF16))
        y = jnp.dot(jnp.concatenate(merged, axis=1), wout_ref[...], preferred_element_type=F32)
        o_ref[rows, :] = _layer_norm(alpha * h + y, g_ref[...], b_ref[...])


def _merge_ln(h, attn, four, w_gate, b_gate, w_pa, w_pf, w_out, ln_g, ln_b, *, alpha,
              tm=ROW_TILE):
    n, d = h.shape
    row = pl.BlockSpec((tm, d), lambda i: (i, 0))
    half = pl.BlockSpec((tm, attn.shape[1]), lambda i: (i, 0))
    return pl.pallas_call(
        functools.partial(_merge_ln_kernel, alpha=alpha),
        out_shape=jax.ShapeDtypeStruct((n, d), F32),
        grid=(n // tm,),
        in_specs=[row, half, half, _resident(w_gate.shape), _resident(b_gate.shape),
                  _resident(w_pa.shape), _resident(w_pf.shape), _resident(w_out.shape),
                  _resident((1, d)), _resident((1, d))],
        out_specs=row,
        compiler_params=_compiler_params(("parallel",)),
        name="merge_ln",
    )(h, attn, four, w_gate, b_gate, w_pa, w_pf, w_out, ln_g.reshape(1, d), ln_b.reshape(1, d))


GATE_COL0 = 3 * ATTN_QKV_WIDTH + FOURIER_WIDTH


def _in_proj_pieces():
    a = ATTN_QKV_WIDTH
    return tuple((part * a + g * GROUP_WIDTH, GROUP_WIDTH,
                  HEAD_DIM ** -0.5 * LOG2_E if part == 0 else 1.0)
                 for g in range(N_GROUPS) for part in range(3)) + ((3 * a, FOURIER_WIDTH, 1.0),)


def _mixer_ln(h, w_qkvu, w_gate, b_in, rel_bias, w_pa, w_pf, w_out, ln_g, ln_b,
              *, alpha, batch, seq):
    b_qkvu = jnp.concatenate([b_in[s:s + w] * f for s, w, f in _in_proj_pieces()]).reshape(1, -1)
    b_gate = b_in[GATE_COL0:].reshape(1, -1)
    qkv0, qkv1, qkv2, u = _in_proj(h, w_qkvu, b_qkvu, batch=batch, seq=seq)
    attn = _attention((qkv0, qkv1, qkv2), _attention_bias_rows(rel_bias), batch=batch, seq=seq)
    four = _fft(u, batch=batch, seq=seq)
    return _merge_ln(h, attn.reshape(batch * seq, -1), four.reshape(batch * seq, -1),
                     w_gate, b_gate, w_pa, w_pf, w_out, ln_g, ln_b, alpha=alpha)


def kernel(x, ln1_g, ln1_b, ffn1_w_gate, ffn1_w_up, ffn1_w_down, w_in, b_in, rel_bias,
           w_proj_attn, w_proj_fourier, w_out, ln2_g, ln2_b, ffn2_w_gate, ffn2_w_up,
           ffn2_w_down, ln3_g, ln3_b):
    batch, seq, d = x.shape
    depth = ln1_g.shape[0]
    assert seq % ROW_TILE == 0 and d % MXU_DIM == 0
    assert w_in.shape[2] == GATE_COL0 + 2 * d
    assert rel_bias.shape == (NUM_BUCKETS, N_GROUPS * HEADS_PER_GROUP)
    alpha = (2 * depth) ** 0.25
    h = x.reshape(batch * seq, d)
    for l in range(depth):
        gates = ((GATE_COL0, w_in.shape[2] - GATE_COL0, 1.0),)
        jobs = (CastJob(w_in[l], (_in_proj_pieces(), gates)),
                _whole(w_proj_attn[l]), _whole(w_proj_fourier[l]), _whole(w_out[l]))
        h, (w_qkvu, w_gate, w_pa, w_pf, w_o) = _ffn_ln(
            h, ffn1_w_gate[l].astype(BF16), ffn1_w_up[l].astype(BF16),
            ffn1_w_down[l].astype(BF16), ln1_g[l], ln1_b[l], alpha=alpha, cast_jobs=jobs)
        h = _mixer_ln(h, w_qkvu, w_gate, b_in[l], rel_bias, w_pa, w_pf, w_o,
                      ln2_g[l], ln2_b[l], alpha=alpha, batch=batch, seq=seq)
        h, _ = _ffn_ln(h, ffn2_w_gate[l].astype(BF16), ffn2_w_up[l].astype(BF16),
                       ffn2_w_down[l].astype(BF16), ln3_g[l], ln3_b[l], alpha=alpha)
    return h.reshape(batch, seq, d)
```

```python
import functools
import math
from typing import NamedTuple

import numpy as np
import jax
import jax.numpy as jnp
from jax import lax
from jax.experimental import pallas as pl
from jax.experimental.pallas import tpu as pltpu

F32 = jnp.float32
BF16 = jnp.bfloat16

HEAD_DIM = 128
HEADS_PER_GROUP = 4
ATTN_PATTERNS = ((128, 1), (512, 4), (2048, 16))
N_GROUPS = len(ATTN_PATTERNS)
GROUP_WIDTH = HEADS_PER_GROUP * HEAD_DIM
ATTN_QKV_WIDTH = N_GROUPS * GROUP_WIDTH
FOURIER_GROUPS = 4
FOURIER_GROUP_DIM = 128
FOURIER_WIDTH = FOURIER_GROUPS * FOURIER_GROUP_DIM
NUM_BUCKETS = 32
MAX_EXACT = 8
MAX_DISTANCE = 1024
NEG_INF = -1e30
LN_EPS = 1e-5
LOG2_E = math.log2(math.e)

LANES = 128
BF16_SUBLANES = 16
MXU_DIM = 256
V7X_VMEM_BYTES = 64 * 1024 * 1024
VMEM_LIMIT_BYTES = V7X_VMEM_BYTES * 7 // 8

ROW_TILE = 1024
SUB_ROWS = 512
MERGE_SUB_ROWS = 256
SINGLE_OP_STRIDE = 4
MERGE_CHUNK = 2 * MXU_DIM
STAGE_SLOTS = 3

HALF_WIN = 64
Q_TILE = 128
K_TILE = Q_TILE + 2 * HALF_WIN
BIAS_PERIOD = Q_TILE + K_TILE
ATTN_CHUNK = 2048
ATTN_UNROLL = 16
FFT_UNROLL = 64

FFT_N1 = 128
FFT_N2 = 64
FFT_K1_PER_DOT = 2
FFT_GROUPS_PER_STEP = 1
FFT_IN_PITCH = FFT_N2 + 8
FFT_MID_PITCH = FFT_N1 + 8


def _compiler_params(semantics):
    return pltpu.CompilerParams(dimension_semantics=semantics,
                                vmem_limit_bytes=VMEM_LIMIT_BYTES)


def _resident(shape):
    zeros = (0,) * len(shape)
    return pl.BlockSpec(shape, lambda *_: zeros, pipeline_mode=pl.Buffered(1))


def _layer_norm(z, g, b):
    mu = jnp.mean(z, axis=-1, keepdims=True)
    zc = z - mu
    var = jnp.mean(zc * zc, axis=-1, keepdims=True)
    return zc * lax.rsqrt(var + LN_EPS) * g + b


class CastJob(NamedTuple):
    src: jax.Array
    outputs: tuple


def _whole(w):
    return CastJob(w, (((0, w.shape[1], 1.0),),))


def _run_cast_jobs(job_outputs, src_refs, out_refs):
    out_refs = iter(out_refs)
    for outputs, src_ref in zip(job_outputs, src_refs):
        for pieces in outputs:
            out_ref = next(out_refs)
            dst = 0
            for col0, width, factor in pieces:
                blk = src_ref[:, col0:col0 + width]
                out_ref[:, dst:dst + width] = (blk if factor == 1.0 else blk * factor).astype(BF16)
                dst += width


def _ffn_ln_kernel(x_ref, wg_ref, wu_ref, wd_ref, g_ref, b_ref, *refs, alpha, f_chunk, jobs):
    o_ref = refs[len(jobs)]
    _run_cast_jobs(jobs, refs[:len(jobs)], refs[len(jobs) + 1:])
    f = wd_ref.shape[0]
    for s in range(x_ref.shape[0] // SUB_ROWS):
        rows = slice(s * SUB_ROWS, (s + 1) * SUB_ROWS)
        x = x_ref[rows, :]
        xb = x.astype(BF16)
        acts = []
        for c0 in range(0, f, f_chunk):
            sl = slice(c0, min(c0 + f_chunk, f))
            nt = (((1,), (1,)), ((), ()))
            gate = lax.dot_general(xb, wg_ref[sl, :], nt, preferred_element_type=F32)
            up = lax.dot_general(xb, wu_ref[sl, :], nt, preferred_element_type=F32)
            acts.append((gate * jax.nn.sigmoid(gate) * up).astype(BF16))
        y = jnp.dot(jnp.concatenate(acts, axis=1), wd_ref[...], preferred_element_type=F32)
        o_ref[rows, :] = _layer_norm(alpha * x + 0.5 * y, g_ref[...], b_ref[...])


def _ffn_ln(x, w_gate, w_up, w_down, ln_g, ln_b, *, alpha, cast_jobs=(), tm=ROW_TILE,
            f_chunk=MXU_DIM):
    n, d = x.shape
    f = w_down.shape[0]
    steps = n // tm
    row = pl.BlockSpec((tm, d), lambda i: (i, 0))

    def row_block(rows, cols):
        assert rows % steps == 0 and (rows // steps) % BF16_SUBLANES == 0
        return pl.BlockSpec((rows // steps, cols), lambda i: (i, 0))

    job_in_specs = [row_block(*job.src.shape) for job in cast_jobs]
    job_out_widths = [(job.src.shape[0], sum(width for _, width, _ in pieces))
                      for job in cast_jobs for pieces in job.outputs]
    outs = pl.pallas_call(
        functools.partial(_ffn_ln_kernel, alpha=alpha, f_chunk=f_chunk,
                          jobs=tuple(job.outputs for job in cast_jobs)),
        out_shape=(jax.ShapeDtypeStruct((n, d), F32),)
        + tuple(jax.ShapeDtypeStruct(shape, BF16) for shape in job_out_widths),
        grid=(steps,),
        in_specs=[row, _resident((f, d)), _resident((f, d)), _resident((f, d)),
                  _resident((1, d)), _resident((1, d))] + job_in_specs,
        out_specs=(row,) + tuple(row_block(*shape) for shape in job_out_widths),
        compiler_params=_compiler_params(("parallel",)),
        name="ffn_ln",
    )(x, w_gate, w_up, w_down, ln_g.reshape(1, d), ln_b.reshape(1, d),
      *[job.src for job in cast_jobs])
    return outs[0], outs[1:]


def _store_deinterleaved(res, out_ref, dil, t, col0, stage_ref, mid_ref, slot):
    n = res.shape[0]
    n_sub = n // dil
    out_rows = slice(t * n_sub, (t + 1) * n_sub)
    inner = min(dil, SINGLE_OP_STRIDE)
    outer = dil // inner
    part = n // inner
    for h in range(res.shape[1] // LANES):
        cols = slice(col0 + h * LANES, col0 + (h + 1) * LANES)
        stage_ref[slot, h] = res[:, h * LANES:(h + 1) * LANES]
        if outer == 1:
            for r in range(dil):
                out_ref[0, r, out_rows, cols] = (
                    stage_ref[slot, h, pl.ds(r, n_sub, stride=dil), :].astype(BF16))
            continue
        for ra in range(inner):
            mid_ref[slot, h, ra * part:(ra + 1) * part] = (
                stage_ref[slot, h, pl.ds(ra, part, stride=inner), :])
        for ra in range(inner):
            for rb in range(outer):
                out_ref[0, ra + inner * rb, out_rows, cols] = (
                    mid_ref[slot, h, pl.ds(ra * part + rb, n_sub, stride=outer), :].astype(BF16))


def _in_proj_kernel(x_ref, w_ref, b_ref, qkv0_ref, qkv1_ref, qkv2_ref, u_ref, stage_ref, mid_ref):
    gw = 3 * GROUP_WIDTH
    piece = 0
    for t in range(x_ref.shape[0] // SUB_ROWS):
        rows = slice(t * SUB_ROWS, (t + 1) * SUB_ROWS)
        xb = x_ref[rows, :].astype(BF16)

        def proj(c0, xb=xb):
            return (jnp.dot(xb, w_ref[:, c0:c0 + MXU_DIM], preferred_element_type=F32)
                    + b_ref[:, c0:c0 + MXU_DIM])

        for g, out_ref in ((2, qkv2_ref), (1, qkv1_ref)):
            for j in range(gw // MXU_DIM):
                _store_deinterleaved(proj(g * gw + j * MXU_DIM), out_ref, ATTN_PATTERNS[g][1], t,
                                     j * MXU_DIM, stage_ref, mid_ref, piece % STAGE_SLOTS)
                piece += 1

        for j in range(gw // MXU_DIM):
            qkv0_ref[0, 0, rows, j * MXU_DIM:(j + 1) * MXU_DIM] = proj(j * MXU_DIM).astype(BF16)
        for j in range(FOURIER_WIDTH // MXU_DIM):
            u_ref[0, rows, j * MXU_DIM:(j + 1) * MXU_DIM] = (
                proj(N_GROUPS * gw + j * MXU_DIM).astype(BF16))


def _in_proj(h, w, b, *, batch, seq, tm=ROW_TILE):
    n, d = h.shape
    width = w.shape[1]
    gw = 3 * GROUP_WIDTH
    tiles_per_seq = seq // tm
    d1, d2 = ATTN_PATTERNS[1][1], ATTN_PATTERNS[2][1]
    out_shape = (
        jax.ShapeDtypeStruct((batch, 1, seq, gw), BF16),
        jax.ShapeDtypeStruct((batch, d1, seq // d1, gw), BF16),
        jax.ShapeDtypeStruct((batch, d2, seq // d2, gw), BF16),
        jax.ShapeDtypeStruct((batch, seq, FOURIER_WIDTH), BF16),
    )

    def bt(i):
        return i // tiles_per_seq, i % tiles_per_seq

    out_specs = (
        pl.BlockSpec((1, 1, tm, gw), lambda i: (bt(i)[0], 0, bt(i)[1], 0)),
        pl.BlockSpec((1, d1, tm // d1, gw), lambda i: (bt(i)[0], 0, bt(i)[1], 0)),
        pl.BlockSpec((1, d2, tm // d2, gw), lambda i: (bt(i)[0], 0, bt(i)[1], 0)),
        pl.BlockSpec((1, tm, FOURIER_WIDTH), lambda i: (bt(i)[0], bt(i)[1], 0)),
    )
    return pl.pallas_call(
        _in_proj_kernel,
        out_shape=out_shape,
        grid=(n // tm,),
        in_specs=[pl.BlockSpec((tm, d), lambda i: (i, 0)), _resident((d, width)),
                  _resident((1, width))],
        out_specs=out_specs,
        scratch_shapes=[pltpu.VMEM((STAGE_SLOTS, MXU_DIM // LANES, SUB_ROWS, LANES), F32)] * 2,
        compiler_params=_compiler_params(("parallel",)),
        name="in_proj",
    )(h, w, b)


def _t5_bucket_np(rel):
    half = NUM_BUCKETS // 2
    ret = (rel > 0).astype(np.int64) * half
    n = np.abs(rel)
    nf = np.maximum(n, 1).astype(np.float64)
    large = MAX_EXACT + (np.log(nf / MAX_EXACT) / math.log(MAX_DISTANCE / MAX_EXACT)
                         * (half - MAX_EXACT)).astype(np.int64)
    large = np.minimum(large, half - 1)
    return ret + np.where(n < MAX_EXACT, n, large)


def _attention_bias_rows(rel_bias):
    offs = np.arange(-HALF_WIN, HALF_WIN + 1)
    band = []
    for g, (_, dil) in enumerate(ATTN_PATTERNS):
        buckets = _t5_bucket_np(offs * dil)
        heads = rel_bias[:, g * HEADS_PER_GROUP:(g + 1) * HEADS_PER_GROUP]
        band.append(jnp.take(heads, jnp.asarray(buckets, jnp.int32), axis=0).T)
    band = jnp.stack(band).astype(F32) * LOG2_E
    base = jnp.concatenate(
        [band, jnp.full(band.shape[:2] + (BIAS_PERIOD - band.shape[-1],), NEG_INF, F32)], axis=-1)
    rows = [jnp.roll(base, HALF_WIN * (v - 1), axis=-1) for v in range(3)]
    return jnp.stack(rows, axis=2)[:, :, :, None, :]


def _attn_kernel(q0, k0, v0, q1, k1, v1, q2, k2, v2, rows_ref, o_ref, out_ref, lse_ref, bias_ref,
                 *, chunk):
    c = pl.program_id(2)

    @pl.when(c == 0)
    def _():
        for g in range(N_GROUPS):
            for v in range(3):
                row0 = jnp.broadcast_to(rows_ref[g, 0, v], (Q_TILE, BIAS_PERIOD))
                circulant = pltpu.roll(row0, 0, 1, stride=1, stride_axis=0)
                bias_ref[g, v] = circulant[:, :K_TILE]

    for g, (q_ref, k_ref, v_ref) in enumerate(((q0, k0, v0), (q1, k1, v1), (q2, k2, v2))):
        dil = ATTN_PATTERNS[g][1]
        sub_len = k_ref.shape[2]
        rows_per_sub = chunk // dil
        tiles_per_sub = rows_per_sub // Q_TILE

        def tile(i, carry, g=g, dil=dil, sub_len=sub_len, rows_per_sub=rows_per_sub,
                 tiles_per_sub=tiles_per_sub, q_ref=q_ref, k_ref=k_ref, v_ref=v_ref):
            r = i // tiles_per_sub
            t = i % tiles_per_sub
            q_start = c * rows_per_sub + t * Q_TILE
            k_start = jnp.clip(q_start - HALF_WIN, 0, sub_len - K_TILE)
            variant = (q_start - k_start) // HALF_WIN
            k_start = pl.multiple_of(k_start, HALF_WIN)
            q = q_ref[0, r, pl.ds(pl.multiple_of(t * Q_TILE, Q_TILE), Q_TILE), :]
            k = k_ref[0, r, pl.ds(k_start, K_TILE), :]
            v = v_ref[0, r, pl.ds(k_start, K_TILE), :]
            s = lax.dot_general(q, k, (((1,), (1,)), ((), ())), preferred_element_type=F32)
            s = s + bias_ref[g, variant]
            m = jnp.max(s, axis=-1, keepdims=True)
            p = jnp.exp2(s - m)
            den = jnp.sum(p, axis=-1, keepdims=True)
            row0 = t * Q_TILE * dil + r
            rows = pl.ds(row0, Q_TILE) if dil == 1 else pl.ds(row0, Q_TILE, stride=dil)
            out_ref[g, rows, :] = (jnp.dot(p.astype(BF16), v, preferred_element_type=F32)
                                   * (1.0 / den))
            lse_ref[g, rows, :] = jnp.broadcast_to(m + jnp.log2(den), (Q_TILE, LANES))
            return carry

        lax.fori_loop(0, dil * tiles_per_sub, tile, 0, unroll=ATTN_UNROLL)

    blk = 256
    for i in range(chunk // blk):
        rows = slice(i * blk, (i + 1) * blk)
        lses = [lse_ref[g, rows, :] for g in range(N_GROUPS)]
        mx = jnp.maximum(jnp.maximum(lses[0], lses[1]), lses[2])
        ws = [jnp.exp2(l - mx) for l in lses]
        num = ws[0] * out_ref[0, rows, :] + ws[1] * out_ref[1, rows, :] + ws[2] * out_ref[2, rows, :]
        o_ref[0, rows, :] = (num / (ws[0] + ws[1] + ws[2])).astype(BF16)


def _attention(qkv, bias_rows, *, batch, seq, chunk=ATTN_CHUNK):
    assert all(window == 2 * HALF_WIN * dil for window, dil in ATTN_PATTERNS)
    assert all(chunk % (dil * Q_TILE) == 0 and seq // dil >= K_TILE for _, dil in ATTN_PATTERNS)
    in_specs = []
    for g, arr in enumerate(qkv):
        dil = ATTN_PATTERNS[g][1]
        sub_len = seq // dil
        in_specs += [
            pl.BlockSpec((1, dil, chunk // dil, HEAD_DIM), lambda b, h, c: (b, 0, c, h)),
            pl.BlockSpec((1, dil, sub_len, HEAD_DIM),
                         lambda b, h, c: (b, 0, 0, HEADS_PER_GROUP + h)),
            pl.BlockSpec((1, dil, sub_len, HEAD_DIM),
                         lambda b, h, c: (b, 0, 0, 2 * HEADS_PER_GROUP + h)),
        ]
    in_specs.append(pl.BlockSpec((N_GROUPS, 1, 3, 1, BIAS_PERIOD),
                                 lambda b, h, c: (0, h, 0, 0, 0)))
    args = [a for arr in qkv for a in (arr, arr, arr)] + [bias_rows]
    return pl.pallas_call(
        functools.partial(_attn_kernel, chunk=chunk),
        out_shape=jax.ShapeDtypeStruct((batch, seq, GROUP_WIDTH), BF16),
        grid=(batch, HEADS_PER_GROUP, seq // chunk),
        in_specs=in_specs,
        out_specs=pl.BlockSpec((1, chunk, HEAD_DIM), lambda b, h, c: (b, c, h)),
        scratch_shapes=[pltpu.VMEM((N_GROUPS, chunk, LANES), F32)] * 2
        + [pltpu.VMEM((N_GROUPS, 3, Q_TILE, K_TILE), F32)],
        compiler_params=_compiler_params(("parallel", "parallel", "arbitrary")),
        name="attention",
    )(*args)


def _fft_tables(seq):
    assert seq == FFT_N1 * FFT_N2
    k1 = np.arange(FFT_N1)[None, :, None]
    n1 = np.arange(FFT_N1)[None, None, :]
    n2 = np.arange(FFT_N2)[:, None, None]
    ang = 2.0 * np.pi * ((n2 * k1 + FFT_N2 * n1 * k1) % seq) / seq
    t1 = np.concatenate([np.cos(ang), -np.sin(ang)], axis=1)
    k2 = np.arange(FFT_N2)[:, None]
    m2 = np.arange(FFT_N2)[None, :]
    ang2 = 2.0 * np.pi * ((k2 * m2) % FFT_N2) / FFT_N2
    cr, ci = np.cos(ang2), -np.sin(ang2)
    t2 = np.block([[cr, -ci], [ci, cr]])
    cc = np.arange(FOURIER_GROUP_DIM)
    ang3 = 2.0 * np.pi * ((cc[:, None] * cc[None, :]) % FOURIER_GROUP_DIM) / FOURIER_GROUP_DIM
    norm = 1.0 / math.sqrt(seq * FOURIER_GROUP_DIM)
    t3 = np.concatenate([np.cos(ang3), np.sin(ang3)], axis=0) * norm
    return (jnp.asarray(t1, BF16), jnp.asarray(t2, BF16), jnp.asarray(t3, BF16))


def _fft_kernel(u_ref, t1_ref, t2_ref, t3_ref, o_ref, uf_ref, yr_ref, yi_ref, gi_ref):
    slabs = range(FFT_GROUPS_PER_STEP)
    gr_ref = uf_ref

    def lanes(h):
        return slice(h * LANES, (h + 1) * LANES)

    def widen(n1, carry):
        src = pl.multiple_of(n1 * FFT_N2, FFT_N2)
        dst = pl.multiple_of(n1 * FFT_IN_PITCH, 8)
        u = u_ref[0, pl.ds(src, FFT_N2), :].astype(F32)
        for h in slabs:
            uf_ref[h, pl.ds(dst, FFT_N2), :] = u[:, lanes(h)]
        return carry
    lax.fori_loop(0, FFT_N1, widen, 0, unroll=FFT_UNROLL)

    def stage1(n2, carry):
        rows = pl.ds(n2, FFT_N1, stride=FFT_IN_PITCH)
        x = jnp.concatenate([uf_ref[h, rows, :] for h in slabs], axis=1).astype(BF16)
        y = jnp.dot(t1_ref[n2], x, preferred_element_type=F32)
        dst = pl.multiple_of(n2 * FFT_MID_PITCH, 8)
        for h in slabs:
            yr_ref[h, pl.ds(dst, FFT_N1), :] = y[:FFT_N1, lanes(h)]
            yi_ref[h, pl.ds(dst, FFT_N1), :] = y[FFT_N1:, lanes(h)]
        return carry
    lax.fori_loop(0, FFT_N2, stage1, 0, unroll=FFT_UNROLL)

    def stage2(i, carry):
        cols = [(h, pl.ds(FFT_K1_PER_DOT * i + j, FFT_N2, stride=FFT_MID_PITCH))
                for j in range(FFT_K1_PER_DOT) for h in slabs]
        y = jnp.concatenate(
            [jnp.concatenate([yr_ref[h, rows, :], yi_ref[h, rows, :]], axis=0) for h, rows in cols],
            axis=1).astype(BF16)
        gk = jnp.dot(t2_ref[...], y, preferred_element_type=F32)
        for c, (h, rows) in enumerate(cols):
            gr_ref[h, rows, :] = gk[:FFT_N2, lanes(c)]
            gi_ref[h, rows, :] = gk[FFT_N2:, lanes(c)]
        return carry
    lax.fori_loop(0, FFT_N1 // FFT_K1_PER_DOT, stage2, 0, unroll=FFT_UNROLL)

    def stage3(k2, carry):
        src = pl.ds(pl.multiple_of(k2 * FFT_MID_PITCH, 8), FFT_N1)
        dst = pl.ds(pl.multiple_of(k2 * FFT_N1, FFT_N1), FFT_N1)
        for h in slabs:
            gk = jnp.concatenate([gr_ref[h, src, :], gi_ref[h, src, :]], axis=1).astype(BF16)
            out = jnp.dot(gk, t3_ref[...], preferred_element_type=F32)
            o_ref[0, dst, lanes(h)] = out.astype(BF16)
        return carry
    lax.fori_loop(0, FFT_N2, stage3, 0, unroll=FFT_UNROLL)


def _fft(u, *, batch, seq):
    t1, t2, t3 = _fft_tables(seq)
    width = FFT_GROUPS_PER_STEP * FOURIER_GROUP_DIM
    blk = pl.BlockSpec((1, seq, width), lambda b, g: (b, 0, g))
    return pl.pallas_call(
        _fft_kernel,
        out_shape=jax.ShapeDtypeStruct((batch, seq, FOURIER_WIDTH), BF16),
        grid=(batch, FOURIER_WIDTH // width),
        in_specs=[blk, _resident(t1.shape), _resident(t2.shape), _resident(t3.shape)],
        out_specs=blk,
        scratch_shapes=[pltpu.VMEM((FFT_GROUPS_PER_STEP, FFT_N1 * FFT_IN_PITCH, LANES), F32)]
        + [pltpu.VMEM((FFT_GROUPS_PER_STEP, FFT_N2 * FFT_MID_PITCH, LANES), F32)] * 3,
        compiler_params=_compiler_params(("parallel", "parallel")),
        name="fft",
    )(u, t1, t2, t3)


def _merge_ln_kernel(h_ref, a_ref, f_ref, wgate_ref, bgate_ref, wpa_ref, wpf_ref, wout_ref,
                     g_ref, b_ref, o_ref, *, alpha):
    d = h_ref.shape[1]
    for s in range(h_ref.shape[0] // MERGE_SUB_ROWS):
        rows = slice(s * MERGE_SUB_ROWS, (s + 1) * MERGE_SUB_ROWS)
        h = h_ref[rows, :]
        hb = h.astype(BF16)
        a = a_ref[rows, :]
        f = f_ref[rows, :]
        merged = []
        for c in range(d // MERGE_CHUNK):
            ca = slice(c * MERGE_CHUNK, (c + 1) * MERGE_CHUNK)
            cf = slice(d + c * MERGE_CHUNK, d + (c + 1) * MERGE_CHUNK)
            ga = jax.nn.sigmoid(jnp.dot(hb, wgate_ref[:, ca], preferred_element_type=F32)
                                + bgate_ref[:, ca])
            gf = jax.nn.sigmoid(jnp.dot(hb, wgate_ref[:, cf], preferred_element_type=F32)
                                + bgate_ref[:, cf])
            pa = jnp.dot(a, wpa_ref[:, ca], preferred_element_type=F32)
            pf = jnp.dot(f, wpf_ref[:, ca], preferred_element_type=F32)
            merged.append((ga * pa + gf * pf).astype(BF16))
        y = jnp.dot(jnp.concatenate(merged, axis=1), wout_ref[...], preferred_element_type=F32)
        o_ref[rows, :] = _layer_norm(alpha * h + y, g_ref[...], b_ref[...])


def _merge_ln(h, attn, four, w_gate, b_gate, w_pa, w_pf, w_out, ln_g, ln_b, *, alpha,
              tm=ROW_TILE):
    n, d = h.shape
    row = pl.BlockSpec((tm, d), lambda i: (i, 0))
    half = pl.BlockSpec((tm, attn.shape[1]), lambda i: (i, 0))
    return pl.pallas_call(
        functools.partial(_merge_ln_kernel, alpha=alpha),
        out_shape=jax.ShapeDtypeStruct((n, d), F32),
        grid=(n // tm,),
        in_specs=[row, half, half, _resident(w_gate.shape), _resident(b_gate.shape),
                  _resident(w_pa.shape), _resident(w_pf.shape), _resident(w_out.shape),
                  _resident((1, d)), _resident((1, d))],
        out_specs=row,
        compiler_params=_compiler_params(("parallel",)),
        name="merge_ln",
    )(h, attn, four, w_gate, b_gate, w_pa, w_pf, w_out, ln_g.reshape(1, d), ln_b.reshape(1, d))


GATE_COL0 = 3 * ATTN_QKV_WIDTH + FOURIER_WIDTH


def _in_proj_pieces():
    a = ATTN_QKV_WIDTH
    return tuple((part * a + g * GROUP_WIDTH, GROUP_WIDTH,
                  HEAD_DIM ** -0.5 * LOG2_E if part == 0 else 1.0)
                 for g in range(N_GROUPS) for part in range(3)) + ((3 * a, FOURIER_WIDTH, 1.0),)


def _mixer_ln(h, w_qkvu, w_gate, b_in, rel_bias, w_pa, w_pf, w_out, ln_g, ln_b,
              *, alpha, batch, seq):
    b_qkvu = jnp.concatenate([b_in[s:s + w] * f for s, w, f in _in_proj_pieces()]).reshape(1, -1)
    b_gate = b_in[GATE_COL0:].reshape(1, -1)
    qkv0, qkv1, qkv2, u = _in_proj(h, w_qkvu, b_qkvu, batch=batch, seq=seq)
    attn = _attention((qkv0, qkv1, qkv2), _attention_bias_rows(rel_bias), batch=batch, seq=seq)
    four = _fft(u, batch=batch, seq=seq)
    return _merge_ln(h, attn.reshape(batch * seq, -1), four.reshape(batch * seq, -1),
                     w_gate, b_gate, w_pa, w_pf, w_out, ln_g, ln_b, alpha=alpha)


def kernel(x, ln1_g, ln1_b, ffn1_w_gate, ffn1_w_up, ffn1_w_down, w_in, b_in, rel_bias,
           w_proj_attn, w_proj_fourier, w_out, ln2_g, ln2_b, ffn2_w_gate, ffn2_w_up,
           ffn2_w_down, ln3_g, ln3_b):
    batch, seq, d = x.shape
    depth = ln1_g.shape[0]
    assert seq % ROW_TILE == 0 and d % MXU_DIM == 0
    assert w_in.shape[2] == GATE_COL0 + 2 * d
    assert rel_bias.shape == (NUM_BUCKETS, N_GROUPS * HEADS_PER_GROUP)
    alpha = (2 * depth) ** 0.25
    h = x.reshape(batch * seq, d)
    for l in range(depth):
        gates = ((GATE_COL0, w_in.shape[2] - GATE_COL0, 1.0),)
        jobs = (CastJob(w_in[l], (_in_proj_pieces(), gates)),
                _whole(w_proj_attn[l]), _whole(w_proj_fourier[l]), _whole(w_out[l]))
        h, (w_qkvu, w_gate, w_pa, w_pf, w_o) = _ffn_ln(
            h, ffn1_w_gate[l].T.astype(BF16), ffn1_w_up[l].T.astype(BF16),
            ffn1_w_down[l].astype(BF16), ln1_g[l], ln1_b[l], alpha=alpha, cast_jobs=jobs)
        h = _mixer_ln(h, w_qkvu, w_gate, b_in[l], rel_bias, w_pa, w_pf, w_o,
                      ln2_g[l], ln2_b[l], alpha=alpha, batch=batch, seq=seq)
        h, _ = _ffn_ln(h, ffn2_w_gate[l].T.astype(BF16), ffn2_w_up[l].T.astype(BF16),
                       ffn2_w_down[l].astype(BF16), ln3_g[l], ln3_b[l], alpha=alpha)
    return h.reshape(batch, seq, d)
```

```python
import functools
import math
from typing import NamedTuple

import numpy as np
import jax
import jax.numpy as jnp
from jax import lax
from jax.experimental import pallas as pl
from jax.experimental.pallas import tpu as pltpu

F32 = jnp.float32
BF16 = jnp.bfloat16

HEAD_DIM = 128
HEADS_PER_GROUP = 4
ATTN_PATTERNS = ((128, 1), (512, 4), (2048, 16))
N_GROUPS = len(ATTN_PATTERNS)
GROUP_WIDTH = HEADS_PER_GROUP * HEAD_DIM
ATTN_QKV_WIDTH = N_GROUPS * GROUP_WIDTH
FOURIER_GROUPS = 4
FOURIER_GROUP_DIM = 128
FOURIER_WIDTH = FOURIER_GROUPS * FOURIER_GROUP_DIM
NUM_BUCKETS = 32
MAX_EXACT = 8
MAX_DISTANCE = 1024
NEG_INF = -1e30
LN_EPS = 1e-5
LOG2_E = math.log2(math.e)

LANES = 128
BF16_SUBLANES = 16
MXU_DIM = 256
V7X_VMEM_BYTES = 64 * 1024 * 1024
VMEM_LIMIT_BYTES = V7X_VMEM_BYTES * 7 // 8

ROW_TILE = 1024
SUB_ROWS = 512
MERGE_SUB_ROWS = 256
SINGLE_OP_STRIDE = 4
MERGE_CHUNK = 2 * MXU_DIM
STAGE_SLOTS = 3

HALF_WIN = 64
Q_TILE = 128
K_TILE = Q_TILE + 2 * HALF_WIN
BIAS_PERIOD = Q_TILE + K_TILE
ATTN_CHUNK = 4096
ATTN_UNROLL = 32
FFT_UNROLL = 64

FFT_N1 = 128
FFT_N2 = 64
FFT_K1_PER_DOT = 2
FFT_GROUPS_PER_STEP = 1
FFT_IN_PITCH = FFT_N2 + 8
FFT_MID_PITCH = FFT_N1 + 8


def _compiler_params(semantics):
    return pltpu.CompilerParams(dimension_semantics=semantics,
                                vmem_limit_bytes=VMEM_LIMIT_BYTES)


def _resident(shape):
    zeros = (0,) * len(shape)
    return pl.BlockSpec(shape, lambda *_: zeros, pipeline_mode=pl.Buffered(1))


def _layer_norm(z, g, b):
    mu = jnp.mean(z, axis=-1, keepdims=True)
    zc = z - mu
    var = jnp.mean(zc * zc, axis=-1, keepdims=True)
    return zc * lax.rsqrt(var + LN_EPS) * g + b


class CastJob(NamedTuple):
    src: jax.Array
    outputs: tuple


def _whole(w):
    return CastJob(w, (((0, w.shape[1], 1.0),),))


def _run_cast_jobs(job_outputs, src_refs, out_refs):
    out_refs = iter(out_refs)
    for outputs, src_ref in zip(job_outputs, src_refs):
        for pieces in outputs:
            out_ref = next(out_refs)
            dst = 0
            for col0, width, factor in pieces:
                blk = src_ref[:, col0:col0 + width]
                out_ref[:, dst:dst + width] = (blk if factor == 1.0 else blk * factor).astype(BF16)
                dst += width


def _ffn_ln_kernel(x_ref, wg_ref, wu_ref, wd_ref, g_ref, b_ref, *refs, alpha, f_chunk, jobs):
    o_ref = refs[len(jobs)]
    _run_cast_jobs(jobs, refs[:len(jobs)], refs[len(jobs) + 1:])
    f = wd_ref.shape[0]
    for s in range(x_ref.shape[0] // SUB_ROWS):
        rows = slice(s * SUB_ROWS, (s + 1) * SUB_ROWS)
        x = x_ref[rows, :]
        xb = x.astype(BF16)
        acts = []
        for c0 in range(0, f, f_chunk):
            sl = slice(c0, min(c0 + f_chunk, f))
            nt = (((1,), (1,)), ((), ()))
            gate = lax.dot_general(xb, wg_ref[sl, :], nt, preferred_element_type=F32)
            up = lax.dot_general(xb, wu_ref[sl, :], nt, preferred_element_type=F32)
            acts.append((gate * jax.nn.sigmoid(gate) * up).astype(BF16))
        y = jnp.dot(jnp.concatenate(acts, axis=1), wd_ref[...], preferred_element_type=F32)
        o_ref[rows, :] = _layer_norm(alpha * x + 0.5 * y, g_ref[...], b_ref[...])


def _ffn_ln(x, w_gate, w_up, w_down, ln_g, ln_b, *, alpha, cast_jobs=(), tm=ROW_TILE,
            f_chunk=MXU_DIM):
    n, d = x.shape
    f = w_down.shape[0]
    steps = n // tm
    row = pl.BlockSpec((tm, d), lambda i: (i, 0))

    def row_block(rows, cols):
        assert rows % steps == 0 and (rows // steps) % BF16_SUBLANES == 0
        return pl.BlockSpec((rows // steps, cols), lambda i: (i, 0))

    job_in_specs = [row_block(*job.src.shape) for job in cast_jobs]
    job_out_widths = [(job.src.shape[0], sum(width for _, width, _ in pieces))
                      for job in cast_jobs for pieces in job.outputs]
    outs = pl.pallas_call(
        functools.partial(_ffn_ln_kernel, alpha=alpha, f_chunk=f_chunk,
                          jobs=tuple(job.outputs for job in cast_jobs)),
        out_shape=(jax.ShapeDtypeStruct((n, d), F32),)
        + tuple(jax.ShapeDtypeStruct(shape, BF16) for shape in job_out_widths),
        grid=(steps,),
        in_specs=[row, _resident((f, d)), _resident((f, d)), _resident((f, d)),
                  _resident((1, d)), _resident((1, d))] + job_in_specs,
        out_specs=(row,) + tuple(row_block(*shape) for shape in job_out_widths),
        compiler_params=_compiler_params(("parallel",)),
        name="ffn_ln",
    )(x, w_gate, w_up, w_down, ln_g.reshape(1, d), ln_b.reshape(1, d),
      *[job.src for job in cast_jobs])
    return outs[0], outs[1:]


def _store_deinterleaved(res, out_ref, dil, t, col0, stage_ref, mid_ref, slot):
    n = res.shape[0]
    n_sub = n // dil
    out_rows = slice(t * n_sub, (t + 1) * n_sub)
    inner = min(dil, SINGLE_OP_STRIDE)
    outer = dil // inner
    part = n // inner
    for h in range(res.shape[1] // LANES):
        cols = slice(col0 + h * LANES, col0 + (h + 1) * LANES)
        stage_ref[slot, h] = res[:, h * LANES:(h + 1) * LANES]
        if outer == 1:
            for r in range(dil):
                out_ref[0, r, out_rows, cols] = (
                    stage_ref[slot, h, pl.ds(r, n_sub, stride=dil), :].astype(BF16))
            continue
        for ra in range(inner):
            mid_ref[slot, h, ra * part:(ra + 1) * part] = (
                stage_ref[slot, h, pl.ds(ra, part, stride=inner), :])
        for ra in range(inner):
            for rb in range(outer):
                out_ref[0, ra + inner * rb, out_rows, cols] = (
                    mid_ref[slot, h, pl.ds(ra * part + rb, n_sub, stride=outer), :].astype(BF16))


def _in_proj_kernel(x_ref, w_ref, b_ref, qkv0_ref, qkv1_ref, qkv2_ref, u_ref, stage_ref, mid_ref):
    gw = 3 * GROUP_WIDTH
    piece = 0
    for t in range(x_ref.shape[0] // SUB_ROWS):
        rows = slice(t * SUB_ROWS, (t + 1) * SUB_ROWS)
        xb = x_ref[rows, :].astype(BF16)

        def proj(c0, xb=xb):
            return (jnp.dot(xb, w_ref[:, c0:c0 + MXU_DIM], preferred_element_type=F32)
                    + b_ref[:, c0:c0 + MXU_DIM])

        for g, out_ref in ((2, qkv2_ref), (1, qkv1_ref)):
            for j in range(gw // MXU_DIM):
                _store_deinterleaved(proj(g * gw + j * MXU_DIM), out_ref, ATTN_PATTERNS[g][1], t,
                                     j * MXU_DIM, stage_ref, mid_ref, piece % STAGE_SLOTS)
                piece += 1

        for j in range(gw // MXU_DIM):
            qkv0_ref[0, 0, rows, j * MXU_DIM:(j + 1) * MXU_DIM] = proj(j * MXU_DIM).astype(BF16)
        for j in range(FOURIER_WIDTH // MXU_DIM):
            u_ref[0, rows, j * MXU_DIM:(j + 1) * MXU_DIM] = (
                proj(N_GROUPS * gw + j * MXU_DIM).astype(BF16))


def _in_proj(h, w, b, *, batch, seq, tm=ROW_TILE):
    n, d = h.shape
    width = w.shape[1]
    gw = 3 * GROUP_WIDTH
    tiles_per_seq = seq // tm
    d1, d2 = ATTN_PATTERNS[1][1], ATTN_PATTERNS[2][1]
    out_shape = (
        jax.ShapeDtypeStruct((batch, 1, seq, gw), BF16),
        jax.ShapeDtypeStruct((batch, d1, seq // d1, gw), BF16),
        jax.ShapeDtypeStruct((batch, d2, seq // d2, gw), BF16),
        jax.ShapeDtypeStruct((batch, seq, FOURIER_WIDTH), BF16),
    )

    def bt(i):
        return i // tiles_per_seq, i % tiles_per_seq

    out_specs = (
        pl.BlockSpec((1, 1, tm, gw), lambda i: (bt(i)[0], 0, bt(i)[1], 0)),
        pl.BlockSpec((1, d1, tm // d1, gw), lambda i: (bt(i)[0], 0, bt(i)[1], 0)),
        pl.BlockSpec((1, d2, tm // d2, gw), lambda i: (bt(i)[0], 0, bt(i)[1], 0)),
        pl.BlockSpec((1, tm, FOURIER_WIDTH), lambda i: (bt(i)[0], bt(i)[1], 0)),
    )
    return pl.pallas_call(
        _in_proj_kernel,
        out_shape=out_shape,
        grid=(n // tm,),
        in_specs=[pl.BlockSpec((tm, d), lambda i: (i, 0)), _resident((d, width)),
                  _resident((1, width))],
        out_specs=out_specs,
        scratch_shapes=[pltpu.VMEM((STAGE_SLOTS, MXU_DIM // LANES, SUB_ROWS, LANES), F32)] * 2,
        compiler_params=_compiler_params(("parallel",)),
        name="in_proj",
    )(h, w, b)


def _t5_bucket_np(rel):
    half = NUM_BUCKETS // 2
    ret = (rel > 0).astype(np.int64) * half
    n = np.abs(rel)
    nf = np.maximum(n, 1).astype(np.float64)
    large = MAX_EXACT + (np.log(nf / MAX_EXACT) / math.log(MAX_DISTANCE / MAX_EXACT)
                         * (half - MAX_EXACT)).astype(np.int64)
    large = np.minimum(large, half - 1)
    return ret + np.where(n < MAX_EXACT, n, large)


def _attention_bias_rows(rel_bias):
    offs = np.arange(-HALF_WIN, HALF_WIN + 1)
    band = []
    for g, (_, dil) in enumerate(ATTN_PATTERNS):
        buckets = _t5_bucket_np(offs * dil)
        heads = rel_bias[:, g * HEADS_PER_GROUP:(g + 1) * HEADS_PER_GROUP]
        band.append(jnp.take(heads, jnp.asarray(buckets, jnp.int32), axis=0).T)
    band = jnp.stack(band).astype(F32) * LOG2_E
    base = jnp.concatenate(
        [band, jnp.full(band.shape[:2] + (BIAS_PERIOD - band.shape[-1],), NEG_INF, F32)], axis=-1)
    rows = [jnp.roll(base, HALF_WIN * (v - 1), axis=-1) for v in range(3)]
    return jnp.stack(rows, axis=2)[:, :, :, None, :]


def _attn_kernel(q0, k0, v0, q1, k1, v1, q2, k2, v2, rows_ref, o_ref, out_ref, lse_ref, bias_ref,
                 *, chunk):
    c = pl.program_id(2)

    @pl.when(c == 0)
    def _():
        for g in range(N_GROUPS):
            for v in range(3):
                row0 = jnp.broadcast_to(rows_ref[g, 0, v], (Q_TILE, BIAS_PERIOD))
                circulant = pltpu.roll(row0, 0, 1, stride=1, stride_axis=0)
                bias_ref[g, v] = circulant[:, :K_TILE]

    for g, (q_ref, k_ref, v_ref) in enumerate(((q0, k0, v0), (q1, k1, v1), (q2, k2, v2))):
        dil = ATTN_PATTERNS[g][1]
        sub_len = k_ref.shape[2]
        rows_per_sub = chunk // dil
        tiles_per_sub = rows_per_sub // Q_TILE

        def tile(i, carry, g=g, dil=dil, sub_len=sub_len, rows_per_sub=rows_per_sub,
                 tiles_per_sub=tiles_per_sub, q_ref=q_ref, k_ref=k_ref, v_ref=v_ref):
            r = i // tiles_per_sub
            t = i % tiles_per_sub
            q_start = c * rows_per_sub + t * Q_TILE
            k_start = jnp.clip(q_start - HALF_WIN, 0, sub_len - K_TILE)
            variant = (q_start - k_start) // HALF_WIN
            k_start = pl.multiple_of(k_start, HALF_WIN)
            q = q_ref[0, r, pl.ds(pl.multiple_of(t * Q_TILE, Q_TILE), Q_TILE), :]
            k = k_ref[0, r, pl.ds(k_start, K_TILE), :]
            v = v_ref[0, r, pl.ds(k_start, K_TILE), :]
            s = lax.dot_general(q, k, (((1,), (1,)), ((), ())), preferred_element_type=F32)
            s = s + bias_ref[g, variant]
            m = jnp.max(s, axis=-1, keepdims=True)
            p = jnp.exp2(s - m)
            den = jnp.sum(p, axis=-1, keepdims=True)
            row0 = t * Q_TILE * dil + r
            rows = pl.ds(row0, Q_TILE) if dil == 1 else pl.ds(row0, Q_TILE, stride=dil)
            out_ref[g, rows, :] = (jnp.dot(p.astype(BF16), v, preferred_element_type=F32)
                                   * (1.0 / den))
            lse_ref[g, rows, :] = jnp.broadcast_to(m + jnp.log2(den), (Q_TILE, LANES))
            return carry

        lax.fori_loop(0, dil * tiles_per_sub, tile, 0, unroll=ATTN_UNROLL)

    blk = 256
    for i in range(chunk // blk):
        rows = slice(i * blk, (i + 1) * blk)
        lses = [lse_ref[g, rows, :] for g in range(N_GROUPS)]
        mx = jnp.maximum(jnp.maximum(lses[0], lses[1]), lses[2])
        ws = [jnp.exp2(l - mx) for l in lses]
        num = ws[0] * out_ref[0, rows, :] + ws[1] * out_ref[1, rows, :] + ws[2] * out_ref[2, rows, :]
        o_ref[0, rows, :] = (num / (ws[0] + ws[1] + ws[2])).astype(BF16)


def _attention(qkv, bias_rows, *, batch, seq, chunk=ATTN_CHUNK):
    assert all(window == 2 * HALF_WIN * dil for window, dil in ATTN_PATTERNS)
    assert all(chunk % (dil * Q_TILE) == 0 and seq // dil >= K_TILE for _, dil in ATTN_PATTERNS)
    in_specs = []
    for g, arr in enumerate(qkv):
        dil = ATTN_PATTERNS[g][1]
        sub_len = seq // dil
        in_specs += [
            pl.BlockSpec((1, dil, chunk // dil, HEAD_DIM), lambda b, h, c: (b, 0, c, h)),
            pl.BlockSpec((1, dil, sub_len, HEAD_DIM),
                         lambda b, h, c: (b, 0, 0, HEADS_PER_GROUP + h)),
            pl.BlockSpec((1, dil, sub_len, HEAD_DIM),
                         lambda b, h, c: (b, 0, 0, 2 * HEADS_PER_GROUP + h)),
        ]
    in_specs.append(pl.BlockSpec((N_GROUPS, 1, 3, 1, BIAS_PERIOD),
                                 lambda b, h, c: (0, h, 0, 0, 0)))
    args = [a for arr in qkv for a in (arr, arr, arr)] + [bias_rows]
    return pl.pallas_call(
        functools.partial(_attn_kernel, chunk=chunk),
        out_shape=jax.ShapeDtypeStruct((batch, seq, GROUP_WIDTH), BF16),
        grid=(batch, HEADS_PER_GROUP, seq // chunk),
        in_specs=in_specs,
        out_specs=pl.BlockSpec((1, chunk, HEAD_DIM), lambda b, h, c: (b, c, h)),
        scratch_shapes=[pltpu.VMEM((N_GROUPS, chunk, LANES), F32)] * 2
        + [pltpu.VMEM((N_GROUPS, 3, Q_TILE, K_TILE), F32)],
        compiler_params=_compiler_params(("parallel", "parallel", "arbitrary")),
        name="attention",
    )(*args)


def _fft_tables(seq):
    assert seq == FFT_N1 * FFT_N2
    k1 = np.arange(FFT_N1)[None, :, None]
    n1 = np.arange(FFT_N1)[None, None, :]
    n2 = np.arange(FFT_N2)[:, None, None]
    ang = 2.0 * np.pi * ((n2 * k1 + FFT_N2 * n1 * k1) % seq) / seq
    t1 = np.concatenate([np.cos(ang), -np.sin(ang)], axis=1)
    k2 = np.arange(FFT_N2)[:, None]
    m2 = np.arange(FFT_N2)[None, :]
    ang2 = 2.0 * np.pi * ((k2 * m2) % FFT_N2) / FFT_N2
    cr, ci = np.cos(ang2), -np.sin(ang2)
    t2 = np.block([[cr, -ci], [ci, cr]])
    cc = np.arange(FOURIER_GROUP_DIM)
    ang3 = 2.0 * np.pi * ((cc[:, None] * cc[None, :]) % FOURIER_GROUP_DIM) / FOURIER_GROUP_DIM
    norm = 1.0 / math.sqrt(seq * FOURIER_GROUP_DIM)
    t3 = np.concatenate([np.cos(ang3), np.sin(ang3)], axis=0) * norm
    return (jnp.asarray(t1, BF16), jnp.asarray(t2, BF16), jnp.asarray(t3, BF16))


def _fft_kernel(u_ref, t1_ref, t2_ref, t3_ref, o_ref, uf_ref, yr_ref, yi_ref, gi_ref):
    slabs = range(FFT_GROUPS_PER_STEP)
    gr_ref = uf_ref

    def lanes(h):
        return slice(h * LANES, (h + 1) * LANES)

    def widen(n1, carry):
        src = pl.multiple_of(n1 * FFT_N2, FFT_N2)
        dst = pl.multiple_of(n1 * FFT_IN_PITCH, 8)
        u = u_ref[0, pl.ds(src, FFT_N2), :].astype(F32)
        for h in slabs:
            uf_ref[h, pl.ds(dst, FFT_N2), :] = u[:, lanes(h)]
        return carry
    lax.fori_loop(0, FFT_N1, widen, 0, unroll=FFT_UNROLL)

    def stage1(n2, carry):
        rows = pl.ds(n2, FFT_N1, stride=FFT_IN_PITCH)
        x = jnp.concatenate([uf_ref[h, rows, :] for h in slabs], axis=1).astype(BF16)
        y = jnp.dot(t1_ref[n2], x, preferred_element_type=F32)
        dst = pl.multiple_of(n2 * FFT_MID_PITCH, 8)
        for h in slabs:
            yr_ref[h, pl.ds(dst, FFT_N1), :] = y[:FFT_N1, lanes(h)]
            yi_ref[h, pl.ds(dst, FFT_N1), :] = y[FFT_N1:, lanes(h)]
        return carry
    lax.fori_loop(0, FFT_N2, stage1, 0, unroll=FFT_UNROLL)

    def stage2(i, carry):
        cols = [(h, pl.ds(FFT_K1_PER_DOT * i + j, FFT_N2, stride=FFT_MID_PITCH))
                for j in range(FFT_K1_PER_DOT) for h in slabs]
        y = jnp.concatenate(
            [jnp.concatenate([yr_ref[h, rows, :], yi_ref[h, rows, :]], axis=0) for h, rows in cols],
            axis=1).astype(BF16)
        gk = jnp.dot(t2_ref[...], y, preferred_element_type=F32)
        for c, (h, rows) in enumerate(cols):
            gr_ref[h, rows, :] = gk[:FFT_N2, lanes(c)]
            gi_ref[h, rows, :] = gk[FFT_N2:, lanes(c)]
        return carry
    lax.fori_loop(0, FFT_N1 // FFT_K1_PER_DOT, stage2, 0, unroll=FFT_UNROLL)

    def stage3(k2, carry):
        src = pl.ds(pl.multiple_of(k2 * FFT_MID_PITCH, 8), FFT_N1)
        dst = pl.ds(pl.multiple_of(k2 * FFT_N1, FFT_N1), FFT_N1)
        for h in slabs:
            gk = jnp.concatenate([gr_ref[h, src, :], gi_ref[h, src, :]], axis=1).astype(BF16)
            out = jnp.dot(gk, t3_ref[...], preferred_element_type=F32)
            o_ref[0, dst, lanes(h)] = out.astype(BF16)
        return carry
    lax.fori_loop(0, FFT_N2, stage3, 0, unroll=FFT_UNROLL)


def _fft(u, *, batch, seq):
    t1, t2, t3 = _fft_tables(seq)
    width = FFT_GROUPS_PER_STEP * FOURIER_GROUP_DIM
    blk = pl.BlockSpec((1, seq, width), lambda b, g: (b, 0, g))
    return pl.pallas_call(
        _fft_kernel,
        out_shape=jax.ShapeDtypeStruct((batch, seq, FOURIER_WIDTH), BF16),
        grid=(batch, FOURIER_WIDTH // width),
        in_specs=[blk, _resident(t1.shape), _resident(t2.shape), _resident(t3.shape)],
        out_specs=blk,
        scratch_shapes=[pltpu.VMEM((FFT_GROUPS_PER_STEP, FFT_N1 * FFT_IN_PITCH, LANES), F32)]
        + [pltpu.VMEM((FFT_GROUPS_PER_STEP, FFT_N2 * FFT_MID_PITCH, LANES), F32)] * 3,
        compiler_params=_compiler_params(("parallel", "parallel")),
        name="fft",
    )(u, t1, t2, t3)


def _merge_ln_kernel(h_ref, a_ref, f_ref, wgate_ref, bgate_ref, wpa_ref, wpf_ref, wout_ref,
                     g_ref, b_ref, o_ref, *, alpha):
    d = h_ref.shape[1]
    for s in range(h_ref.shape[0] // MERGE_SUB_ROWS):
        rows = slice(s * MERGE_SUB_ROWS, (s + 1) * MERGE_SUB_ROWS)
        h = h_ref[rows, :]
        hb = h.astype(BF16)
        a = a_ref[rows, :]
        f = f_ref[rows, :]
        merged = []
        for c in range(d // MERGE_CHUNK):
            ca = slice(c * MERGE_CHUNK, (c + 1) * MERGE_CHUNK)
            cf = slice(d + c * MERGE_CHUNK, d + (c + 1) * MERGE_CHUNK)
            ga = jax.nn.sigmoid(jnp.dot(hb, wgate_ref[:, ca], preferred_element_type=F32)
                                + bgate_ref[:, ca])
            gf = jax.nn.sigmoid(jnp.dot(hb, wgate_ref[:, cf], preferred_element_type=F32)
                                + bgate_ref[:, cf])
            pa = jnp.dot(a, wpa_ref[:, ca], preferred_element_type=F32)
            pf = jnp.dot(f, wpf_ref[:, ca], preferred_element_type=F32)
            merged.append((ga * pa + gf * pf).astype(BF16))
        y = jnp.dot(jnp.concatenate(merged, axis=1), wout_ref[...], preferred_element_type=F32)
        o_ref[rows, :] = _layer_norm(alpha * h + y, g_ref[...], b_ref[...])


def _merge_ln(h, attn, four, w_gate, b_gate, w_pa, w_pf, w_out, ln_g, ln_b, *, alpha,
              tm=ROW_TILE):
    n, d = h.shape
    row = pl.BlockSpec((tm, d), lambda i: (i, 0))
    half = pl.BlockSpec((tm, attn.shape[1]), lambda i: (i, 0))
    return pl.pallas_call(
        functools.partial(_merge_ln_kernel, alpha=alpha),
        out_shape=jax.ShapeDtypeStruct((n, d), F32),
        grid=(n // tm,),
        in_specs=[row, half, half, _resident(w_gate.shape), _resident(b_gate.shape),
                  _resident(w_pa.shape), _resident(w_pf.shape), _resident(w_out.shape),
                  _resident((1, d)), _resident((1, d))],
        out_specs=row,
        compiler_params=_compiler_params(("parallel",)),
        name="merge_ln",
    )(h, attn, four, w_gate, b_gate, w_pa, w_pf, w_out, ln_g.reshape(1, d), ln_b.reshape(1, d))


GATE_COL0 = 3 * ATTN_QKV_WIDTH + FOURIER_WIDTH


def _in_proj_pieces():
    a = ATTN_QKV_WIDTH
    return tuple((part * a + g * GROUP_WIDTH, GROUP_WIDTH,
                  HEAD_DIM ** -0.5 * LOG2_E if part == 0 else 1.0)
                 for g in range(N_GROUPS) for part in range(3)) + ((3 * a, FOURIER_WIDTH, 1.0),)


def _mixer_ln(h, w_qkvu, w_gate, b_in, rel_bias, w_pa, w_pf, w_out, ln_g, ln_b,
              *, alpha, batch, seq):
    b_qkvu = jnp.concatenate([b_in[s:s + w] * f for s, w, f in _in_proj_pieces()]).reshape(1, -1)
    b_gate = b_in[GATE_COL0:].reshape(1, -1)
    qkv0, qkv1, qkv2, u = _in_proj(h, w_qkvu, b_qkvu, batch=batch, seq=seq)
    attn = _attention((qkv0, qkv1, qkv2), _attention_bias_rows(rel_bias), batch=batch, seq=seq)
    four = _fft(u, batch=batch, seq=seq)
    return _merge_ln(h, attn.reshape(batch * seq, -1), four.reshape(batch * seq, -1),
                     w_gate, b_gate, w_pa, w_pf, w_out, ln_g, ln_b, alpha=alpha)


def kernel(x, ln1_g, ln1_b, ffn1_w_gate, ffn1_w_up, ffn1_w_down, w_in, b_in, rel_bias,
           w_proj_attn, w_proj_fourier, w_out, ln2_g, ln2_b, ffn2_w_gate, ffn2_w_up,
           ffn2_w_down, ln3_g, ln3_b):
    batch, seq, d = x.shape
    depth = ln1_g.shape[0]
    assert seq % ROW_TILE == 0 and d % MXU_DIM == 0
    assert w_in.shape[2] == GATE_COL0 + 2 * d
    assert rel_bias.shape == (NUM_BUCKETS, N_GROUPS * HEADS_PER_GROUP)
    alpha = (2 * depth) ** 0.25
    h = x.reshape(batch * seq, d)
    for l in range(depth):
        gates = ((GATE_COL0, w_in.shape[2] - GATE_COL0, 1.0),)
        jobs = (CastJob(w_in[l], (_in_proj_pieces(), gates)),
                _whole(w_proj_attn[l]), _whole(w_proj_fourier[l]), _whole(w_out[l]))
        h, (w_qkvu, w_gate, w_pa, w_pf, w_o) = _ffn_ln(
            h, ffn1_w_gate[l].T.astype(BF16), ffn1_w_up[l].T.astype(BF16),
            ffn1_w_down[l].astype(BF16), ln1_g[l], ln1_b[l], alpha=alpha, cast_jobs=jobs)
        h = _mixer_ln(h, w_qkvu, w_gate, b_in[l], rel_bias, w_pa, w_pf, w_o,
                      ln2_g[l], ln2_b[l], alpha=alpha, batch=batch, seq=seq)
        h, _ = _ffn_ln(h, ffn2_w_gate[l].T.astype(BF16), ffn2_w_up[l].T.astype(BF16),
                       ffn2_w_down[l].astype(BF16), ln3_g[l], ln3_b[l], alpha=alpha)
    return h.reshape(batch, seq, d)
```

```python
import functools
import math
from typing import NamedTuple

import numpy as np
import jax
import jax.numpy as jnp
from jax import lax
from jax.experimental import pallas as pl
from jax.experimental.pallas import tpu as pltpu

F32 = jnp.float32
BF16 = jnp.bfloat16

HEAD_DIM = 128
HEADS_PER_GROUP = 4
ATTN_PATTERNS = ((128, 1), (512, 4), (2048, 16))
N_GROUPS = len(ATTN_PATTERNS)
GROUP_WIDTH = HEADS_PER_GROUP * HEAD_DIM
ATTN_QKV_WIDTH = N_GROUPS * GROUP_WIDTH
FOURIER_GROUPS = 4
FOURIER_GROUP_DIM = 128
FOURIER_WIDTH = FOURIER_GROUPS * FOURIER_GROUP_DIM
NUM_BUCKETS = 32
MAX_EXACT = 8
MAX_DISTANCE = 1024
NEG_INF = -1e30
LN_EPS = 1e-5
LOG2_E = math.log2(math.e)

LANES = 128
BF16_SUBLANES = 16
MXU_DIM = 256
V7X_VMEM_BYTES = 64 * 1024 * 1024
VMEM_LIMIT_BYTES = V7X_VMEM_BYTES * 7 // 8

ROW_TILE = 1024
SUB_ROWS = 512
MERGE_SUB_ROWS = 256
SINGLE_OP_STRIDE = 4
MERGE_CHUNK = 2 * MXU_DIM
STAGE_SLOTS = 3

HALF_WIN = 64
Q_TILE = 128
K_TILE = Q_TILE + 2 * HALF_WIN
BIAS_PERIOD = Q_TILE + K_TILE
ATTN_CHUNK = 4096
ATTN_UNROLL = 32
FFT_UNROLL = 64

FFT_N1 = 128
FFT_N2 = 64
FFT_K1_PER_DOT = 2
FFT_GROUPS_PER_STEP = 2
FFT_IN_PITCH = FFT_N2 + 8
FFT_MID_PITCH = FFT_N1 + 8


def _compiler_params(semantics):
    return pltpu.CompilerParams(dimension_semantics=semantics,
                                vmem_limit_bytes=VMEM_LIMIT_BYTES)


def _resident(shape):
    zeros = (0,) * len(shape)
    return pl.BlockSpec(shape, lambda *_: zeros, pipeline_mode=pl.Buffered(1))


def _layer_norm(z, g, b):
    mu = jnp.mean(z, axis=-1, keepdims=True)
    zc = z - mu
    var = jnp.mean(zc * zc, axis=-1, keepdims=True)
    return zc * lax.rsqrt(var + LN_EPS) * g + b


class CastJob(NamedTuple):
    src: jax.Array
    outputs: tuple


def _whole(w):
    return CastJob(w, (((0, w.shape[1], 1.0),),))


def _run_cast_jobs(job_outputs, src_refs, out_refs):
    out_refs = iter(out_refs)
    for outputs, src_ref in zip(job_outputs, src_refs):
        for pieces in outputs:
            out_ref = next(out_refs)
            dst = 0
            for col0, width, factor in pieces:
                blk = src_ref[:, col0:col0 + width]
                out_ref[:, dst:dst + width] = (blk if factor == 1.0 else blk * factor).astype(BF16)
                dst += width


def _ffn_ln_kernel(x_ref, wg_ref, wu_ref, wd_ref, g_ref, b_ref, *refs, alpha, f_chunk, jobs):
    o_ref = refs[len(jobs)]
    _run_cast_jobs(jobs, refs[:len(jobs)], refs[len(jobs) + 1:])
    f = wd_ref.shape[0]
    for s in range(x_ref.shape[0] // SUB_ROWS):
        rows = slice(s * SUB_ROWS, (s + 1) * SUB_ROWS)
        x = x_ref[rows, :]
        xb = x.astype(BF16)
        acts = []
        for c0 in range(0, f, f_chunk):
            sl = slice(c0, min(c0 + f_chunk, f))
            nt = (((1,), (1,)), ((), ()))
            gate = lax.dot_general(xb, wg_ref[sl, :], nt, preferred_element_type=F32)
            up = lax.dot_general(xb, wu_ref[sl, :], nt, preferred_element_type=F32)
            acts.append((gate * jax.nn.sigmoid(gate) * up).astype(BF16))
        y = jnp.dot(jnp.concatenate(acts, axis=1), wd_ref[...], preferred_element_type=F32)
        o_ref[rows, :] = _layer_norm(alpha * x + 0.5 * y, g_ref[...], b_ref[...])


def _ffn_ln(x, w_gate, w_up, w_down, ln_g, ln_b, *, alpha, cast_jobs=(), tm=ROW_TILE,
            f_chunk=MXU_DIM):
    n, d = x.shape
    f = w_down.shape[0]
    steps = n // tm
    row = pl.BlockSpec((tm, d), lambda i: (i, 0))

    def row_block(rows, cols):
        assert rows % steps == 0 and (rows // steps) % BF16_SUBLANES == 0
        return pl.BlockSpec((rows // steps, cols), lambda i: (i, 0))

    job_in_specs = [row_block(*job.src.shape) for job in cast_jobs]
    job_out_widths = [(job.src.shape[0], sum(width for _, width, _ in pieces))
                      for job in cast_jobs for pieces in job.outputs]
    outs = pl.pallas_call(
        functools.partial(_ffn_ln_kernel, alpha=alpha, f_chunk=f_chunk,
                          jobs=tuple(job.outputs for job in cast_jobs)),
        out_shape=(jax.ShapeDtypeStruct((n, d), F32),)
        + tuple(jax.ShapeDtypeStruct(shape, BF16) for shape in job_out_widths),
        grid=(steps,),
        in_specs=[row, _resident((f, d)), _resident((f, d)), _resident((f, d)),
                  _resident((1, d)), _resident((1, d))] + job_in_specs,
        out_specs=(row,) + tuple(row_block(*shape) for shape in job_out_widths),
        compiler_params=_compiler_params(("parallel",)),
        name="ffn_ln",
    )(x, w_gate, w_up, w_down, ln_g.reshape(1, d), ln_b.reshape(1, d),
      *[job.src for job in cast_jobs])
    return outs[0], outs[1:]


def _store_deinterleaved(res, out_ref, dil, t, col0, stage_ref, mid_ref, slot):
    n = res.shape[0]
    n_sub = n // dil
    out_rows = slice(t * n_sub, (t + 1) * n_sub)
    inner = min(dil, SINGLE_OP_STRIDE)
    outer = dil // inner
    part = n // inner
    for h in range(res.shape[1] // LANES):
        cols = slice(col0 + h * LANES, col0 + (h + 1) * LANES)
        stage_ref[slot, h] = res[:, h * LANES:(h + 1) * LANES]
        if outer == 1:
            for r in range(dil):
                out_ref[0, r, out_rows, cols] = (
                    stage_ref[slot, h, pl.ds(r, n_sub, stride=dil), :].astype(BF16))
            continue
        for ra in range(inner):
            mid_ref[slot, h, ra * part:(ra + 1) * part] = (
                stage_ref[slot, h, pl.ds(ra, part, stride=inner), :])
        for ra in range(inner):
            for rb in range(outer):
                out_ref[0, ra + inner * rb, out_rows, cols] = (
                    mid_ref[slot, h, pl.ds(ra * part + rb, n_sub, stride=outer), :].astype(BF16))


def _in_proj_kernel(x_ref, w_ref, b_ref, qkv0_ref, qkv1_ref, qkv2_ref, u_ref, stage_ref, mid_ref):
    gw = 3 * GROUP_WIDTH
    piece = 0
    for t in range(x_ref.shape[0] // SUB_ROWS):
        rows = slice(t * SUB_ROWS, (t + 1) * SUB_ROWS)
        xb = x_ref[rows, :].astype(BF16)

        def proj(c0, xb=xb):
            return (jnp.dot(xb, w_ref[:, c0:c0 + MXU_DIM], preferred_element_type=F32)
                    + b_ref[:, c0:c0 + MXU_DIM])

        for g, out_ref in ((2, qkv2_ref), (1, qkv1_ref)):
            for j in range(gw // MXU_DIM):
                _store_deinterleaved(proj(g * gw + j * MXU_DIM), out_ref, ATTN_PATTERNS[g][1], t,
                                     j * MXU_DIM, stage_ref, mid_ref, piece % STAGE_SLOTS)
                piece += 1

        for j in range(gw // MXU_DIM):
            qkv0_ref[0, 0, rows, j * MXU_DIM:(j + 1) * MXU_DIM] = proj(j * MXU_DIM).astype(BF16)
        for j in range(FOURIER_WIDTH // MXU_DIM):
            u_ref[0, rows, j * MXU_DIM:(j + 1) * MXU_DIM] = (
                proj(N_GROUPS * gw + j * MXU_DIM).astype(BF16))


def _in_proj(h, w, b, *, batch, seq, tm=ROW_TILE):
    n, d = h.shape
    width = w.shape[1]
    gw = 3 * GROUP_WIDTH
    tiles_per_seq = seq // tm
    d1, d2 = ATTN_PATTERNS[1][1], ATTN_PATTERNS[2][1]
    out_shape = (
        jax.ShapeDtypeStruct((batch, 1, seq, gw), BF16),
        jax.ShapeDtypeStruct((batch, d1, seq // d1, gw), BF16),
        jax.ShapeDtypeStruct((batch, d2, seq // d2, gw), BF16),
        jax.ShapeDtypeStruct((batch, seq, FOURIER_WIDTH), BF16),
    )

    def bt(i):
        return i // tiles_per_seq, i % tiles_per_seq

    out_specs = (
        pl.BlockSpec((1, 1, tm, gw), lambda i: (bt(i)[0], 0, bt(i)[1], 0)),
        pl.BlockSpec((1, d1, tm // d1, gw), lambda i: (bt(i)[0], 0, bt(i)[1], 0)),
        pl.BlockSpec((1, d2, tm // d2, gw), lambda i: (bt(i)[0], 0, bt(i)[1], 0)),
        pl.BlockSpec((1, tm, FOURIER_WIDTH), lambda i: (bt(i)[0], bt(i)[1], 0)),
    )
    return pl.pallas_call(
        _in_proj_kernel,
        out_shape=out_shape,
        grid=(n // tm,),
        in_specs=[pl.BlockSpec((tm, d), lambda i: (i, 0)), _resident((d, width)),
                  _resident((1, width))],
        out_specs=out_specs,
        scratch_shapes=[pltpu.VMEM((STAGE_SLOTS, MXU_DIM // LANES, SUB_ROWS, LANES), F32)] * 2,
        compiler_params=_compiler_params(("parallel",)),
        name="in_proj",
    )(h, w, b)


def _t5_bucket_np(rel):
    half = NUM_BUCKETS // 2
    ret = (rel > 0).astype(np.int64) * half
    n = np.abs(rel)
    nf = np.maximum(n, 1).astype(np.float64)
    large = MAX_EXACT + (np.log(nf / MAX_EXACT) / math.log(MAX_DISTANCE / MAX_EXACT)
                         * (half - MAX_EXACT)).astype(np.int64)
    large = np.minimum(large, half - 1)
    return ret + np.where(n < MAX_EXACT, n, large)


def _attention_bias_rows(rel_bias):
    offs = np.arange(-HALF_WIN, HALF_WIN + 1)
    band = []
    for g, (_, dil) in enumerate(ATTN_PATTERNS):
        buckets = _t5_bucket_np(offs * dil)
        heads = rel_bias[:, g * HEADS_PER_GROUP:(g + 1) * HEADS_PER_GROUP]
        band.append(jnp.take(heads, jnp.asarray(buckets, jnp.int32), axis=0).T)
    band = jnp.stack(band).astype(F32) * LOG2_E
    base = jnp.concatenate(
        [band, jnp.full(band.shape[:2] + (BIAS_PERIOD - band.shape[-1],), NEG_INF, F32)], axis=-1)
    rows = [jnp.roll(base, HALF_WIN * (v - 1), axis=-1) for v in range(3)]
    return jnp.stack(rows, axis=2)[:, :, :, None, :]


def _attn_kernel(q0, k0, v0, q1, k1, v1, q2, k2, v2, rows_ref, o_ref, out_ref, lse_ref, bias_ref,
                 *, chunk):
    c = pl.program_id(2)

    @pl.when(c == 0)
    def _():
        for g in range(N_GROUPS):
            for v in range(3):
                row0 = jnp.broadcast_to(rows_ref[g, 0, v], (Q_TILE, BIAS_PERIOD))
                circulant = pltpu.roll(row0, 0, 1, stride=1, stride_axis=0)
                bias_ref[g, v] = circulant[:, :K_TILE]

    for g, (q_ref, k_ref, v_ref) in enumerate(((q0, k0, v0), (q1, k1, v1), (q2, k2, v2))):
        dil = ATTN_PATTERNS[g][1]
        sub_len = k_ref.shape[2]
        rows_per_sub = chunk // dil
        tiles_per_sub = rows_per_sub // Q_TILE

        def tile(i, carry, g=g, dil=dil, sub_len=sub_len, rows_per_sub=rows_per_sub,
                 tiles_per_sub=tiles_per_sub, q_ref=q_ref, k_ref=k_ref, v_ref=v_ref):
            r = i // tiles_per_sub
            t = i % tiles_per_sub
            q_start = c * rows_per_sub + t * Q_TILE
            k_start = jnp.clip(q_start - HALF_WIN, 0, sub_len - K_TILE)
            variant = (q_start - k_start) // HALF_WIN
            k_start = pl.multiple_of(k_start, HALF_WIN)
            q = q_ref[0, r, pl.ds(pl.multiple_of(t * Q_TILE, Q_TILE), Q_TILE), :]
            k = k_ref[0, r, pl.ds(k_start, K_TILE), :]
            v = v_ref[0, r, pl.ds(k_start, K_TILE), :]
            s = lax.dot_general(q, k, (((1,), (1,)), ((), ())), preferred_element_type=F32)
            s = s + bias_ref[g, variant]
            m = jnp.max(s, axis=-1, keepdims=True)
            p = jnp.exp2(s - m)
            den = jnp.sum(p, axis=-1, keepdims=True)
            row0 = t * Q_TILE * dil + r
            rows = pl.ds(row0, Q_TILE) if dil == 1 else pl.ds(row0, Q_TILE, stride=dil)
            out_ref[g, rows, :] = (jnp.dot(p.astype(BF16), v, preferred_element_type=F32)
                                   * (1.0 / den))
            lse_ref[g, rows, :] = jnp.broadcast_to(m + jnp.log2(den), (Q_TILE, LANES))
            return carry

        lax.fori_loop(0, dil * tiles_per_sub, tile, 0, unroll=ATTN_UNROLL)

    blk = 256
    for i in range(chunk // blk):
        rows = slice(i * blk, (i + 1) * blk)
        lses = [lse_ref[g, rows, :] for g in range(N_GROUPS)]
        mx = jnp.maximum(jnp.maximum(lses[0], lses[1]), lses[2])
        ws = [jnp.exp2(l - mx) for l in lses]
        num = ws[0] * out_ref[0, rows, :] + ws[1] * out_ref[1, rows, :] + ws[2] * out_ref[2, rows, :]
        o_ref[0, rows, :] = (num / (ws[0] + ws[1] + ws[2])).astype(BF16)


def _attention(qkv, bias_rows, *, batch, seq, chunk=ATTN_CHUNK):
    assert all(window == 2 * HALF_WIN * dil for window, dil in ATTN_PATTERNS)
    assert all(chunk % (dil * Q_TILE) == 0 and seq // dil >= K_TILE for _, dil in ATTN_PATTERNS)
    in_specs = []
    for g, arr in enumerate(qkv):
        dil = ATTN_PATTERNS[g][1]
        sub_len = seq // dil
        in_specs += [
            pl.BlockSpec((1, dil, chunk // dil, HEAD_DIM), lambda b, h, c: (b, 0, c, h)),
            pl.BlockSpec((1, dil, sub_len, HEAD_DIM),
                         lambda b, h, c: (b, 0, 0, HEADS_PER_GROUP + h)),
            pl.BlockSpec((1, dil, sub_len, HEAD_DIM),
                         lambda b, h, c: (b, 0, 0, 2 * HEADS_PER_GROUP + h)),
        ]
    in_specs.append(pl.BlockSpec((N_GROUPS, 1, 3, 1, BIAS_PERIOD),
                                 lambda b, h, c: (0, h, 0, 0, 0)))
    args = [a for arr in qkv for a in (arr, arr, arr)] + [bias_rows]
    return pl.pallas_call(
        functools.partial(_attn_kernel, chunk=chunk),
        out_shape=jax.ShapeDtypeStruct((batch, seq, GROUP_WIDTH), BF16),
        grid=(batch, HEADS_PER_GROUP, seq // chunk),
        in_specs=in_specs,
        out_specs=pl.BlockSpec((1, chunk, HEAD_DIM), lambda b, h, c: (b, c, h)),
        scratch_shapes=[pltpu.VMEM((N_GROUPS, chunk, LANES), F32)] * 2
        + [pltpu.VMEM((N_GROUPS, 3, Q_TILE, K_TILE), F32)],
        compiler_params=_compiler_params(("parallel", "parallel", "arbitrary")),
        name="attention",
    )(*args)


def _fft_tables(seq):
    assert seq == FFT_N1 * FFT_N2
    k1 = np.arange(FFT_N1)[None, :, None]
    n1 = np.arange(FFT_N1)[None, None, :]
    n2 = np.arange(FFT_N2)[:, None, None]
    ang = 2.0 * np.pi * ((n2 * k1 + FFT_N2 * n1 * k1) % seq) / seq
    t1 = np.concatenate([np.cos(ang), -np.sin(ang)], axis=1)
    k2 = np.arange(FFT_N2)[:, None]
    m2 = np.arange(FFT_N2)[None, :]
    ang2 = 2.0 * np.pi * ((k2 * m2) % FFT_N2) / FFT_N2
    cr, ci = np.cos(ang2), -np.sin(ang2)
    t2 = np.block([[cr, -ci], [ci, cr]])
    cc = np.arange(FOURIER_GROUP_DIM)
    ang3 = 2.0 * np.pi * ((cc[:, None] * cc[None, :]) % FOURIER_GROUP_DIM) / FOURIER_GROUP_DIM
    norm = 1.0 / math.sqrt(seq * FOURIER_GROUP_DIM)
    t3 = np.concatenate([np.cos(ang3), np.sin(ang3)], axis=0) * norm
    return (jnp.asarray(t1, BF16), jnp.asarray(t2, BF16), jnp.asarray(t3, BF16))


def _fft_kernel(u_ref, t1_ref, t2_ref, t3_ref, o_ref, uf_ref, yr_ref, yi_ref, gi_ref):
    slabs = range(FFT_GROUPS_PER_STEP)
    gr_ref = uf_ref

    def lanes(h):
        return slice(h * LANES, (h + 1) * LANES)

    def widen(n1, carry):
        src = pl.multiple_of(n1 * FFT_N2, FFT_N2)
        dst = pl.multiple_of(n1 * FFT_IN_PITCH, 8)
        u = u_ref[0, pl.ds(src, FFT_N2), :].astype(F32)
        for h in slabs:
            uf_ref[h, pl.ds(dst, FFT_N2), :] = u[:, lanes(h)]
        return carry
    lax.fori_loop(0, FFT_N1, widen, 0, unroll=FFT_UNROLL)

    def stage1(n2, carry):
        rows = pl.ds(n2, FFT_N1, stride=FFT_IN_PITCH)
        x = jnp.concatenate([uf_ref[h, rows, :] for h in slabs], axis=1).astype(BF16)
        y = jnp.dot(t1_ref[n2], x, preferred_element_type=F32)
        dst = pl.multiple_of(n2 * FFT_MID_PITCH, 8)
        for h in slabs:
            yr_ref[h, pl.ds(dst, FFT_N1), :] = y[:FFT_N1, lanes(h)]
            yi_ref[h, pl.ds(dst, FFT_N1), :] = y[FFT_N1:, lanes(h)]
        return carry
    lax.fori_loop(0, FFT_N2, stage1, 0, unroll=FFT_UNROLL)

    def stage2(i, carry):
        cols = [(h, pl.ds(FFT_K1_PER_DOT * i + j, FFT_N2, stride=FFT_MID_PITCH))
                for j in range(FFT_K1_PER_DOT) for h in slabs]
        y = jnp.concatenate(
            [jnp.concatenate([yr_ref[h, rows, :], yi_ref[h, rows, :]], axis=0) for h, rows in cols],
            axis=1).astype(BF16)
        gk = jnp.dot(t2_ref[...], y, preferred_element_type=F32)
        for c, (h, rows) in enumerate(cols):
            gr_ref[h, rows, :] = gk[:FFT_N2, lanes(c)]
            gi_ref[h, rows, :] = gk[FFT_N2:, lanes(c)]
        return carry
    lax.fori_loop(0, FFT_N1 // FFT_K1_PER_DOT, stage2, 0, unroll=FFT_UNROLL)

    def stage3(k2, carry):
        src = pl.ds(pl.multiple_of(k2 * FFT_MID_PITCH, 8), FFT_N1)
        dst = pl.ds(pl.multiple_of(k2 * FFT_N1, FFT_N1), FFT_N1)
        for h in slabs:
            gk = jnp.concatenate([gr_ref[h, src, :], gi_ref[h, src, :]], axis=1).astype(BF16)
            out = jnp.dot(gk, t3_ref[...], preferred_element_type=F32)
            o_ref[0, dst, lanes(h)] = out.astype(BF16)
        return carry
    lax.fori_loop(0, FFT_N2, stage3, 0, unroll=FFT_UNROLL)


def _fft(u, *, batch, seq):
    t1, t2, t3 = _fft_tables(seq)
    width = FFT_GROUPS_PER_STEP * FOURIER_GROUP_DIM
    blk = pl.BlockSpec((1, seq, width), lambda b, g: (b, 0, g))
    return pl.pallas_call(
        _fft_kernel,
        out_shape=jax.ShapeDtypeStruct((batch, seq, FOURIER_WIDTH), BF16),
        grid=(batch, FOURIER_WIDTH // width),
        in_specs=[blk, _resident(t1.shape), _resident(t2.shape), _resident(t3.shape)],
        out_specs=blk,
        scratch_shapes=[pltpu.VMEM((FFT_GROUPS_PER_STEP, FFT_N1 * FFT_IN_PITCH, LANES), F32)]
        + [pltpu.VMEM((FFT_GROUPS_PER_STEP, FFT_N2 * FFT_MID_PITCH, LANES), F32)] * 3,
        compiler_params=_compiler_params(("parallel", "parallel")),
        name="fft",
    )(u, t1, t2, t3)


def _merge_ln_kernel(h_ref, a_ref, f_ref, wgate_ref, bgate_ref, wpa_ref, wpf_ref, wout_ref,
                     g_ref, b_ref, o_ref, *, alpha):
    d = h_ref.shape[1]
    for s in range(h_ref.shape[0] // MERGE_SUB_ROWS):
        rows = slice(s * MERGE_SUB_ROWS, (s + 1) * MERGE_SUB_ROWS)
        h = h_ref[rows, :]
        hb = h.astype(BF16)
        a = a_ref[rows, :]
        f = f_ref[rows, :]
        merged = []
        for c in range(d // MERGE_CHUNK):
            ca = slice(c * MERGE_CHUNK, (c + 1) * MERGE_CHUNK)
            cf = slice(d + c * MERGE_CHUNK, d + (c + 1) * MERGE_CHUNK)
            ga = jax.nn.sigmoid(jnp.dot(hb, wgate_ref[:, ca], preferred_element_type=F32)
                                + bgate_ref[:, ca])
            gf = jax.nn.sigmoid(jnp.dot(hb, wgate_ref[:, cf], preferred_element_type=F32)
                                + bgate_ref[:, cf])
            pa = jnp.dot(a, wpa_ref[:, ca], preferred_element_type=F32)
            pf = jnp.dot(f, wpf_ref[:, ca], preferred_element_type=F32)
            merged.append((ga * pa + gf * pf).astype(BF16))
        y = jnp.dot(jnp.concatenate(merged, axis=1), wout_ref[...], preferred_element_type=F32)
        o_ref[rows, :] = _layer_norm(alpha * h + y, g_ref[...], b_ref[...])


def _merge_ln(h, attn, four, w_gate, b_gate, w_pa, w_pf, w_out, ln_g, ln_b, *, alpha,
              tm=ROW_TILE):
    n, d = h.shape
    row = pl.BlockSpec((tm, d), lambda i: (i, 0))
    half = pl.BlockSpec((tm, attn.shape[1]), lambda i: (i, 0))
    return pl.pallas_call(
        functools.partial(_merge_ln_kernel, alpha=alpha),
        out_shape=jax.ShapeDtypeStruct((n, d), F32),
        grid=(n // tm,),
        in_specs=[row, half, half, _resident(w_gate.shape), _resident(b_gate.shape),
                  _resident(w_pa.shape), _resident(w_pf.shape), _resident(w_out.shape),
                  _resident((1, d)), _resident((1, d))],
        out_specs=row,
        compiler_params=_compiler_params(("parallel",)),
        name="merge_ln",
    )(h, attn, four, w_gate, b_gate, w_pa, w_pf, w_out, ln_g.reshape(1, d), ln_b.reshape(1, d))


GATE_COL0 = 3 * ATTN_QKV_WIDTH + FOURIER_WIDTH


def _in_proj_pieces():
    a = ATTN_QKV_WIDTH
    return tuple((part * a + g * GROUP_WIDTH, GROUP_WIDTH,
                  HEAD_DIM ** -0.5 * LOG2_E if part == 0 else 1.0)
                 for g in range(N_GROUPS) for part in range(3)) + ((3 * a, FOURIER_WIDTH, 1.0),)


def _mixer_ln(h, w_qkvu, w_gate, b_in, rel_bias, w_pa, w_pf, w_out, ln_g, ln_b,
              *, alpha, batch, seq):
    b_qkvu = jnp.concatenate([b_in[s:s + w] * f for s, w, f in _in_proj_pieces()]).reshape(1, -1)
    b_gate = b_in[GATE_COL0:].reshape(1, -1)
    qkv0, qkv1, qkv2, u = _in_proj(h, w_qkvu, b_qkvu, batch=batch, seq=seq)
    attn = _attention((qkv0, qkv1, qkv2), _attention_bias_rows(rel_bias), batch=batch, seq=seq)
    four = _fft(u, batch=batch, seq=seq)
    return _merge_ln(h, attn.reshape(batch * seq, -1), four.reshape(batch * seq, -1),
                     w_gate, b_gate, w_pa, w_pf, w_out, ln_g, ln_b, alpha=alpha)


def kernel(x, ln1_g, ln1_b, ffn1_w_gate, ffn1_w_up, ffn1_w_down, w_in, b_in, rel_bias,
           w_proj_attn, w_proj_fourier, w_out, ln2_g, ln2_b, ffn2_w_gate, ffn2_w_up,
           ffn2_w_down, ln3_g, ln3_b):
    batch, seq, d = x.shape
    depth = ln1_g.shape[0]
    assert seq % ROW_TILE == 0 and d % MXU_DIM == 0
    assert w_in.shape[2] == GATE_COL0 + 2 * d
    assert rel_bias.shape == (NUM_BUCKETS, N_GROUPS * HEADS_PER_GROUP)
    alpha = (2 * depth) ** 0.25
    h = x.reshape(batch * seq, d)
    for l in range(depth):
        gates = ((GATE_COL0, w_in.shape[2] - GATE_COL0, 1.0),)
        jobs = (CastJob(w_in[l], (_in_proj_pieces(), gates)),
                _whole(w_proj_attn[l]), _whole(w_proj_fourier[l]), _whole(w_out[l]))
        h, (w_qkvu, w_gate, w_pa, w_pf, w_o) = _ffn_ln(
            h, ffn1_w_gate[l].T.astype(BF16), ffn1_w_up[l].T.astype(BF16),
            ffn1_w_down[l].astype(BF16), ln1_g[l], ln1_b[l], alpha=alpha, cast_jobs=jobs)
        h = _mixer_ln(h, w_qkvu, w_gate, b_in[l], rel_bias, w_pa, w_pf, w_o,
                      ln2_g[l], ln2_b[l], alpha=alpha, batch=batch, seq=seq)
        h, _ = _ffn_ln(h, ffn2_w_gate[l].T.astype(BF16), ffn2_w_up[l].T.astype(BF16),
                       ffn2_w_down[l].astype(BF16), ln3_g[l], ln3_b[l], alpha=alpha)
    return h.reshape(batch, seq, d)
```

```python
import functools
import math
from typing import NamedTuple

import numpy as np
import jax
import jax.numpy as jnp
from jax import lax
from jax.experimental import pallas as pl
from jax.experimental.pallas import tpu as pltpu

F32 = jnp.float32
BF16 = jnp.bfloat16

HEAD_DIM = 128
HEADS_PER_GROUP = 4
ATTN_PATTERNS = ((128, 1), (512, 4), (2048, 16))
N_GROUPS = len(ATTN_PATTERNS)
GROUP_WIDTH = HEADS_PER_GROUP * HEAD_DIM
ATTN_QKV_WIDTH = N_GROUPS * GROUP_WIDTH
FOURIER_GROUPS = 4
FOURIER_GROUP_DIM = 128
FOURIER_WIDTH = FOURIER_GROUPS * FOURIER_GROUP_DIM
NUM_BUCKETS = 32
MAX_EXACT = 8
MAX_DISTANCE = 1024
NEG_INF = -1e30
LN_EPS = 1e-5
LOG2_E = math.log2(math.e)

LANES = 128
BF16_SUBLANES = 16
MXU_DIM = 256
V7X_VMEM_BYTES = 64 * 1024 * 1024
VMEM_LIMIT_BYTES = V7X_VMEM_BYTES * 7 // 8

ROW_TILE = 1024
SUB_ROWS = 512
MERGE_SUB_ROWS = 256
SINGLE_OP_STRIDE = 4
MERGE_CHUNK = 2 * MXU_DIM
STAGE_SLOTS = 3

HALF_WIN = 64
Q_TILE = 128
K_TILE = Q_TILE + 2 * HALF_WIN
BIAS_PERIOD = Q_TILE + K_TILE
ATTN_CHUNK = 4096
ATTN_UNROLL = 32
FFT_UNROLL = 64

FFT_N1 = 128
FFT_N2 = 64
FFT_K1_PER_DOT = 2
FFT_GROUPS_PER_STEP = 1
FFT_IN_PITCH = FFT_N2 + 8
FFT_MID_PITCH = FFT_N1 + 8


def _compiler_params(semantics):
    return pltpu.CompilerParams(dimension_semantics=semantics,
                                vmem_limit_bytes=VMEM_LIMIT_BYTES)


def _resident(shape):
    zeros = (0,) * len(shape)
    return pl.BlockSpec(shape, lambda *_: zeros, pipeline_mode=pl.Buffered(1))


def _layer_norm(z, g, b):
    mu = jnp.mean(z, axis=-1, keepdims=True)
    zc = z - mu
    var = jnp.mean(zc * zc, axis=-1, keepdims=True)
    return zc * lax.rsqrt(var + LN_EPS) * g + b


class CastJob(NamedTuple):
    src: jax.Array
    outputs: tuple


def _whole(w):
    return CastJob(w, (((0, w.shape[1], 1.0),),))


def _run_cast_jobs(job_outputs, src_refs, out_refs):
    out_refs = iter(out_refs)
    for outputs, src_ref in zip(job_outputs, src_refs):
        for pieces in outputs:
            out_ref = next(out_refs)
            dst = 0
            for col0, width, factor in pieces:
                blk = src_ref[:, col0:col0 + width]
                out_ref[:, dst:dst + width] = (blk if factor == 1.0 else blk * factor).astype(BF16)
                dst += width


def _ffn_ln_kernel(x_ref, wg_ref, wu_ref, wd_ref, g_ref, b_ref, *refs, alpha, f_chunk, jobs):
    o_ref = refs[len(jobs)]
    _run_cast_jobs(jobs, refs[:len(jobs)], refs[len(jobs) + 1:])
    f = wd_ref.shape[0]
    for s in range(x_ref.shape[0] // SUB_ROWS):
        rows = slice(s * SUB_ROWS, (s + 1) * SUB_ROWS)
        x = x_ref[rows, :]
        xb = x.astype(BF16)
        acts = []
        for c0 in range(0, f, f_chunk):
            sl = slice(c0, min(c0 + f_chunk, f))
            nt = (((1,), (1,)), ((), ()))
            gate = lax.dot_general(xb, wg_ref[sl, :], nt, preferred_element_type=F32)
            up = lax.dot_general(xb, wu_ref[sl, :], nt, preferred_element_type=F32)
            acts.append((gate * jax.nn.sigmoid(gate) * up).astype(BF16))
        y = jnp.dot(jnp.concatenate(acts, axis=1), wd_ref[...], preferred_element_type=F32)
        o_ref[rows, :] = _layer_norm(alpha * x + 0.5 * y, g_ref[...], b_ref[...])


def _ffn_ln(x, w_gate, w_up, w_down, ln_g, ln_b, *, alpha, cast_jobs=(), tm=ROW_TILE,
            f_chunk=MXU_DIM):
    n, d = x.shape
    f = w_down.shape[0]
    steps = n // tm
    row = pl.BlockSpec((tm, d), lambda i: (i, 0))

    def row_block(rows, cols):
        assert rows % steps == 0 and (rows // steps) % BF16_SUBLANES == 0
        return pl.BlockSpec((rows // steps, cols), lambda i: (i, 0))

    job_in_specs = [row_block(*job.src.shape) for job in cast_jobs]
    job_out_widths = [(job.src.shape[0], sum(width for _, width, _ in pieces))
                      for job in cast_jobs for pieces in job.outputs]
    outs = pl.pallas_call(
        functools.partial(_ffn_ln_kernel, alpha=alpha, f_chunk=f_chunk,
                          jobs=tuple(job.outputs for job in cast_jobs)),
        out_shape=(jax.ShapeDtypeStruct((n, d), F32),)
        + tuple(jax.ShapeDtypeStruct(shape, BF16) for shape in job_out_widths),
        grid=(steps,),
        in_specs=[row, _resident((f, d)), _resident((f, d)), _resident((f, d)),
                  _resident((1, d)), _resident((1, d))] + job_in_specs,
        out_specs=(row,) + tuple(row_block(*shape) for shape in job_out_widths),
        compiler_params=_compiler_params(("parallel",)),
        name="ffn_ln",
    )(x, w_gate, w_up, w_down, ln_g.reshape(1, d), ln_b.reshape(1, d),
      *[job.src for job in cast_jobs])
    return outs[0], outs[1:]


def _store_deinterleaved(res, out_ref, dil, t, col0, stage_ref, mid_ref, slot):
    n = res.shape[0]
    n_sub = n // dil
    out_rows = slice(t * n_sub, (t + 1) * n_sub)
    inner = min(dil, SINGLE_OP_STRIDE)
    outer = dil // inner
    part = n // inner
    for h in range(res.shape[1] // LANES):
        cols = slice(col0 + h * LANES, col0 + (h + 1) * LANES)
        stage_ref[slot, h] = res[:, h * LANES:(h + 1) * LANES]
        if outer == 1:
            for r in range(dil):
                out_ref[0, r, out_rows, cols] = (
                    stage_ref[slot, h, pl.ds(r, n_sub, stride=dil), :].astype(BF16))
            continue
        for ra in range(inner):
            mid_ref[slot, h, ra * part:(ra + 1) * part] = (
                stage_ref[slot, h, pl.ds(ra, part, stride=inner), :])
        for ra in range(inner):
            for rb in range(outer):
                out_ref[0, ra + inner * rb, out_rows, cols] = (
                    mid_ref[slot, h, pl.ds(ra * part + rb, n_sub, stride=outer), :].astype(BF16))


def _in_proj_kernel(x_ref, w_ref, b_ref, qkv0_ref, qkv1_ref, qkv2_ref, u_ref, stage_ref, mid_ref):
    gw = 3 * GROUP_WIDTH
    piece = 0
    for t in range(x_ref.shape[0] // SUB_ROWS):
        rows = slice(t * SUB_ROWS, (t + 1) * SUB_ROWS)
        xb = x_ref[rows, :].astype(BF16)

        def proj(c0, xb=xb):
            return (jnp.dot(xb, w_ref[:, c0:c0 + MXU_DIM], preferred_element_type=F32)
                    + b_ref[:, c0:c0 + MXU_DIM])

        for g, out_ref in ((2, qkv2_ref), (1, qkv1_ref)):
            for j in range(gw // MXU_DIM):
                _store_deinterleaved(proj(g * gw + j * MXU_DIM), out_ref, ATTN_PATTERNS[g][1], t,
                                     j * MXU_DIM, stage_ref, mid_ref, piece % STAGE_SLOTS)
                piece += 1

        for j in range(gw // MXU_DIM):
            qkv0_ref[0, 0, rows, j * MXU_DIM:(j + 1) * MXU_DIM] = proj(j * MXU_DIM).astype(BF16)
        for j in range(FOURIER_WIDTH // MXU_DIM):
            u_ref[0, rows, j * MXU_DIM:(j + 1) * MXU_DIM] = (
                proj(N_GROUPS * gw + j * MXU_DIM).astype(BF16))


def _in_proj(h, w, b, *, batch, seq, tm=ROW_TILE):
    n, d = h.shape
    width = w.shape[1]
    gw = 3 * GROUP_WIDTH
    tiles_per_seq = seq // tm
    d1, d2 = ATTN_PATTERNS[1][1], ATTN_PATTERNS[2][1]
    out_shape = (
        jax.ShapeDtypeStruct((batch, 1, seq, gw), BF16),
        jax.ShapeDtypeStruct((batch, d1, seq // d1, gw), BF16),
        jax.ShapeDtypeStruct((batch, d2, seq // d2, gw), BF16),
        jax.ShapeDtypeStruct((batch, seq, FOURIER_WIDTH), BF16),
    )

    def bt(i):
        return i // tiles_per_seq, i % tiles_per_seq

    out_specs = (
        pl.BlockSpec((1, 1, tm, gw), lambda i: (bt(i)[0], 0, bt(i)[1], 0)),
        pl.BlockSpec((1, d1, tm // d1, gw), lambda i: (bt(i)[0], 0, bt(i)[1], 0)),
        pl.BlockSpec((1, d2, tm // d2, gw), lambda i: (bt(i)[0], 0, bt(i)[1], 0)),
        pl.BlockSpec((1, tm, FOURIER_WIDTH), lambda i: (bt(i)[0], bt(i)[1], 0)),
    )
    return pl.pallas_call(
        _in_proj_kernel,
        out_shape=out_shape,
        grid=(n // tm,),
        in_specs=[pl.BlockSpec((tm, d), lambda i: (i, 0)), _resident((d, width)),
                  _resident((1, width))],
        out_specs=out_specs,
        scratch_shapes=[pltpu.VMEM((STAGE_SLOTS, MXU_DIM // LANES, SUB_ROWS, LANES), F32)] * 2,
        compiler_params=_compiler_params(("parallel",)),
        name="in_proj",
    )(h, w, b)


def _t5_bucket_np(rel):
    half = NUM_BUCKETS // 2
    ret = (rel > 0).astype(np.int64) * half
    n = np.abs(rel)
    nf = np.maximum(n, 1).astype(np.float64)
    large = MAX_EXACT + (np.log(nf / MAX_EXACT) / math.log(MAX_DISTANCE / MAX_EXACT)
                         * (half - MAX_EXACT)).astype(np.int64)
    large = np.minimum(large, half - 1)
    return ret + np.where(n < MAX_EXACT, n, large)


def _attention_bias_rows(rel_bias):
    offs = np.arange(-HALF_WIN, HALF_WIN + 1)
    band = []
    for g, (_, dil) in enumerate(ATTN_PATTERNS):
        buckets = _t5_bucket_np(offs * dil)
        heads = rel_bias[:, g * HEADS_PER_GROUP:(g + 1) * HEADS_PER_GROUP]
        band.append(jnp.take(heads, jnp.asarray(buckets, jnp.int32), axis=0).T)
    band = jnp.stack(band).astype(F32) * LOG2_E
    base = jnp.concatenate(
        [band, jnp.full(band.shape[:2] + (BIAS_PERIOD - band.shape[-1],), NEG_INF, F32)], axis=-1)
    rows = [jnp.roll(base, HALF_WIN * (v - 1), axis=-1) for v in range(3)]
    return jnp.stack(rows, axis=2)[:, :, :, None, :]


def _attn_kernel(q0, k0, v0, q1, k1, v1, q2, k2, v2, rows_ref, o_ref, out_ref, lse_ref, bias_ref,
                 *, chunk):
    c = pl.program_id(2)

    @pl.when(c == 0)
    def _():
        for g in range(N_GROUPS):
            for v in range(3):
                row0 = jnp.broadcast_to(rows_ref[g, 0, v], (Q_TILE, BIAS_PERIOD))
                circulant = pltpu.roll(row0, 0, 1, stride=1, stride_axis=0)
                bias_ref[g, v] = circulant[:, :K_TILE]

    for g, (q_ref, k_ref, v_ref) in enumerate(((q0, k0, v0), (q1, k1, v1), (q2, k2, v2))):
        dil = ATTN_PATTERNS[g][1]
        sub_len = k_ref.shape[2]
        rows_per_sub = chunk // dil
        tiles_per_sub = rows_per_sub // Q_TILE

        def tile(i, carry, g=g, dil=dil, sub_len=sub_len, rows_per_sub=rows_per_sub,
                 tiles_per_sub=tiles_per_sub, q_ref=q_ref, k_ref=k_ref, v_ref=v_ref):
            r = i // tiles_per_sub
            t = i % tiles_per_sub
            q_start = c * rows_per_sub + t * Q_TILE
            k_start = jnp.clip(q_start - HALF_WIN, 0, sub_len - K_TILE)
            variant = (q_start - k_start) // HALF_WIN
            k_start = pl.multiple_of(k_start, HALF_WIN)
            q = q_ref[0, r, pl.ds(pl.multiple_of(t * Q_TILE, Q_TILE), Q_TILE), :]
            k = k_ref[0, r, pl.ds(k_start, K_TILE), :]
            v = v_ref[0, r, pl.ds(k_start, K_TILE), :]
            s = lax.dot_general(q, k, (((1,), (1,)), ((), ())), preferred_element_type=F32)
            s = s + bias_ref[g, variant]
            m = jnp.max(s, axis=-1, keepdims=True)
            p = jnp.exp2(s - m)
            v_ones = jnp.concatenate([v, jnp.ones((K_TILE, LANES), BF16)], axis=1)
            pv = jnp.dot(p.astype(BF16), v_ones, preferred_element_type=F32)
            den = pv[:, HEAD_DIM:]
            row0 = t * Q_TILE * dil + r
            rows = pl.ds(row0, Q_TILE) if dil == 1 else pl.ds(row0, Q_TILE, stride=dil)
            out_ref[g, rows, :] = pv[:, :HEAD_DIM] * (1.0 / den)
            lse_ref[g, rows, :] = m + jnp.log2(den)
            return carry

        lax.fori_loop(0, dil * tiles_per_sub, tile, 0, unroll=ATTN_UNROLL)

    blk = 256
    for i in range(chunk // blk):
        rows = slice(i * blk, (i + 1) * blk)
        lses = [lse_ref[g, rows, :] for g in range(N_GROUPS)]
        mx = jnp.maximum(jnp.maximum(lses[0], lses[1]), lses[2])
        ws = [jnp.exp2(l - mx) for l in lses]
        num = ws[0] * out_ref[0, rows, :] + ws[1] * out_ref[1, rows, :] + ws[2] * out_ref[2, rows, :]
        o_ref[0, rows, :] = (num / (ws[0] + ws[1] + ws[2])).astype(BF16)


def _attention(qkv, bias_rows, *, batch, seq, chunk=ATTN_CHUNK):
    assert all(window == 2 * HALF_WIN * dil for window, dil in ATTN_PATTERNS)
    assert all(chunk % (dil * Q_TILE) == 0 and seq // dil >= K_TILE for _, dil in ATTN_PATTERNS)
    in_specs = []
    for g, arr in enumerate(qkv):
        dil = ATTN_PATTERNS[g][1]
        sub_len = seq // dil
        in_specs += [
            pl.BlockSpec((1, dil, chunk // dil, HEAD_DIM), lambda b, h, c: (b, 0, c, h)),
            pl.BlockSpec((1, dil, sub_len, HEAD_DIM),
                         lambda b, h, c: (b, 0, 0, HEADS_PER_GROUP + h)),
            pl.BlockSpec((1, dil, sub_len, HEAD_DIM),
                         lambda b, h, c: (b, 0, 0, 2 * HEADS_PER_GROUP + h)),
        ]
    in_specs.append(pl.BlockSpec((N_GROUPS, 1, 3, 1, BIAS_PERIOD),
                                 lambda b, h, c: (0, h, 0, 0, 0)))
    args = [a for arr in qkv for a in (arr, arr, arr)] + [bias_rows]
    return pl.pallas_call(
        functools.partial(_attn_kernel, chunk=chunk),
        out_shape=jax.ShapeDtypeStruct((batch, seq, GROUP_WIDTH), BF16),
        grid=(batch, HEADS_PER_GROUP, seq // chunk),
        in_specs=in_specs,
        out_specs=pl.BlockSpec((1, chunk, HEAD_DIM), lambda b, h, c: (b, c, h)),
        scratch_shapes=[pltpu.VMEM((N_GROUPS, chunk, LANES), F32)] * 2
        + [pltpu.VMEM((N_GROUPS, 3, Q_TILE, K_TILE), F32)],
        compiler_params=_compiler_params(("parallel", "parallel", "arbitrary")),
        name="attention",
    )(*args)


def _fft_tables(seq):
    assert seq == FFT_N1 * FFT_N2
    k1 = np.arange(FFT_N1)[None, :, None]
    n1 = np.arange(FFT_N1)[None, None, :]
    n2 = np.arange(FFT_N2)[:, None, None]
    ang = 2.0 * np.pi * ((n2 * k1 + FFT_N2 * n1 * k1) % seq) / seq
    t1 = np.concatenate([np.cos(ang), -np.sin(ang)], axis=1)
    k2 = np.arange(FFT_N2)[:, None]
    m2 = np.arange(FFT_N2)[None, :]
    ang2 = 2.0 * np.pi * ((k2 * m2) % FFT_N2) / FFT_N2
    cr, ci = np.cos(ang2), -np.sin(ang2)
    t2 = np.block([[cr, -ci], [ci, cr]])
    cc = np.arange(FOURIER_GROUP_DIM)
    ang3 = 2.0 * np.pi * ((cc[:, None] * cc[None, :]) % FOURIER_GROUP_DIM) / FOURIER_GROUP_DIM
    norm = 1.0 / math.sqrt(seq * FOURIER_GROUP_DIM)
    t3 = np.concatenate([np.cos(ang3), np.sin(ang3)], axis=0) * norm
    return (jnp.asarray(t1, BF16), jnp.asarray(t2, BF16), jnp.asarray(t3, BF16))


def _fft_kernel(u_ref, t1_ref, t2_ref, t3_ref, o_ref, uf_ref, yr_ref, yi_ref, gi_ref):
    slabs = range(FFT_GROUPS_PER_STEP)
    gr_ref = uf_ref

    def lanes(h):
        return slice(h * LANES, (h + 1) * LANES)

    def widen(n1, carry):
        src = pl.multiple_of(n1 * FFT_N2, FFT_N2)
        dst = pl.multiple_of(n1 * FFT_IN_PITCH, 8)
        u = u_ref[0, pl.ds(src, FFT_N2), :].astype(F32)
        for h in slabs:
            uf_ref[h, pl.ds(dst, FFT_N2), :] = u[:, lanes(h)]
        return carry
    lax.fori_loop(0, FFT_N1, widen, 0, unroll=FFT_UNROLL)

    def stage1(n2, carry):
        rows = pl.ds(n2, FFT_N1, stride=FFT_IN_PITCH)
        x = jnp.concatenate([uf_ref[h, rows, :] for h in slabs], axis=1).astype(BF16)
        y = jnp.dot(t1_ref[n2], x, preferred_element_type=F32)
        dst = pl.multiple_of(n2 * FFT_MID_PITCH, 8)
        for h in slabs:
            yr_ref[h, pl.ds(dst, FFT_N1), :] = y[:FFT_N1, lanes(h)]
            yi_ref[h, pl.ds(dst, FFT_N1), :] = y[FFT_N1:, lanes(h)]
        return carry
    lax.fori_loop(0, FFT_N2, stage1, 0, unroll=FFT_UNROLL)

    def stage2(i, carry):
        cols = [(h, pl.ds(FFT_K1_PER_DOT * i + j, FFT_N2, stride=FFT_MID_PITCH))
                for j in range(FFT_K1_PER_DOT) for h in slabs]
        y = jnp.concatenate(
            [jnp.concatenate([yr_ref[h, rows, :], yi_ref[h, rows, :]], axis=0) for h, rows in cols],
            axis=1).astype(BF16)
        gk = jnp.dot(t2_ref[...], y, preferred_element_type=F32)
        for c, (h, rows) in enumerate(cols):
            gr_ref[h, rows, :] = gk[:FFT_N2, lanes(c)]
            gi_ref[h, rows, :] = gk[FFT_N2:, lanes(c)]
        return carry
    lax.fori_loop(0, FFT_N1 // FFT_K1_PER_DOT, stage2, 0, unroll=FFT_UNROLL)

    def stage3(k2, carry):
        src = pl.ds(pl.multiple_of(k2 * FFT_MID_PITCH, 8), FFT_N1)
        dst = pl.ds(pl.multiple_of(k2 * FFT_N1, FFT_N1), FFT_N1)
        for h in slabs:
            gk = jnp.concatenate([gr_ref[h, src, :], gi_ref[h, src, :]], axis=1).astype(BF16)
            out = jnp.dot(gk, t3_ref[...], preferred_element_type=F32)
            o_ref[0, dst, lanes(h)] = out.astype(BF16)
        return carry
    lax.fori_loop(0, FFT_N2, stage3, 0, unroll=FFT_UNROLL)


def _fft(u, *, batch, seq):
    t1, t2, t3 = _fft_tables(seq)
    width = FFT_GROUPS_PER_STEP * FOURIER_GROUP_DIM
    blk = pl.BlockSpec((1, seq, width), lambda b, g: (b, 0, g))
    return pl.pallas_call(
        _fft_kernel,
        out_shape=jax.ShapeDtypeStruct((batch, seq, FOURIER_WIDTH), BF16),
        grid=(batch, FOURIER_WIDTH // width),
        in_specs=[blk, _resident(t1.shape), _resident(t2.shape), _resident(t3.shape)],
        out_specs=blk,
        scratch_shapes=[pltpu.VMEM((FFT_GROUPS_PER_STEP, FFT_N1 * FFT_IN_PITCH, LANES), F32)]
        + [pltpu.VMEM((FFT_GROUPS_PER_STEP, FFT_N2 * FFT_MID_PITCH, LANES), F32)] * 3,
        compiler_params=_compiler_params(("parallel", "parallel")),
        name="fft",
    )(u, t1, t2, t3)


def _merge_ln_kernel(h_ref, a_ref, f_ref, wgate_ref, bgate_ref, wpa_ref, wpf_ref, wout_ref,
                     g_ref, b_ref, o_ref, *, alpha):
    d = h_ref.shape[1]
    for s in range(h_ref.shape[0] // MERGE_SUB_ROWS):
        rows = slice(s * MERGE_SUB_ROWS, (s + 1) * MERGE_SUB_ROWS)
        h = h_ref[rows, :]
        hb = h.astype(BF16)
        a = a_ref[rows, :]
        f = f_ref[rows, :]
        merged = []
        for c in range(d // MERGE_CHUNK):
            ca = slice(c * MERGE_CHUNK, (c + 1) * MERGE_CHUNK)
            cf = slice(d + c * MERGE_CHUNK, d + (c + 1) * MERGE_CHUNK)
            ga = jax.nn.sigmoid(jnp.dot(hb, wgate_ref[:, ca], preferred_element_type=F32)
                                + bgate_ref[:, ca])
            gf = jax.nn.sigmoid(jnp.dot(hb, wgate_ref[:, cf], preferred_element_type=F32)
                                + bgate_ref[:, cf])
            pa = jnp.dot(a, wpa_ref[:, ca], preferred_element_type=F32)
            pf = jnp.dot(f, wpf_ref[:, ca], preferred_element_type=F32)
            merged.append((ga * pa + gf * pf).astype(BF16))
        y = jnp.dot(jnp.concatenate(merged, axis=1), wout_ref[...], preferred_element_type=F32)
        o_ref[rows, :] = _layer_norm(alpha * h + y, g_ref[...], b_ref[...])


def _merge_ln(h, attn, four, w_gate, b_gate, w_pa, w_pf, w_out, ln_g, ln_b, *, alpha,
              tm=ROW_TILE):
    n, d = h.shape
    row = pl.BlockSpec((tm, d), lambda i: (i, 0))
    half = pl.BlockSpec((tm, attn.shape[1]), lambda i: (i, 0))
    return pl.pallas_call(
        functools.partial(_merge_ln_kernel, alpha=alpha),
        out_shape=jax.ShapeDtypeStruct((n, d), F32),
        grid=(n // tm,),
        in_specs=[row, half, half, _resident(w_gate.shape), _resident(b_gate.shape),
                  _resident(w_pa.shape), _resident(w_pf.shape), _resident(w_out.shape),
                  _resident((1, d)), _resident((1, d))],
        out_specs=row,
        compiler_params=_compiler_params(("parallel",)),
        name="merge_ln",
    )(h, attn, four, w_gate, b_gate, w_pa, w_pf, w_out, ln_g.reshape(1, d), ln_b.reshape(1, d))


GATE_COL0 = 3 * ATTN_QKV_WIDTH + FOURIER_WIDTH


def _in_proj_pieces():
    a = ATTN_QKV_WIDTH
    return tuple((part * a + g * GROUP_WIDTH, GROUP_WIDTH,
                  HEAD_DIM ** -0.5 * LOG2_E if part == 0 else 1.0)
                 for g in range(N_GROUPS) for part in range(3)) + ((3 * a, FOURIER_WIDTH, 1.0),)


def _mixer_ln(h, w_qkvu, w_gate, b_in, rel_bias, w_pa, w_pf, w_out, ln_g, ln_b,
              *, alpha, batch, seq):
    b_qkvu = jnp.concatenate([b_in[s:s + w] * f for s, w, f in _in_proj_pieces()]).reshape(1, -1)
    b_gate = b_in[GATE_COL0:].reshape(1, -1)
    qkv0, qkv1, qkv2, u = _in_proj(h, w_qkvu, b_qkvu, batch=batch, seq=seq)
    attn = _attention((qkv0, qkv1, qkv2), _attention_bias_rows(rel_bias), batch=batch, seq=seq)
    four = _fft(u, batch=batch, seq=seq)
    return _merge_ln(h, attn.reshape(batch * seq, -1), four.reshape(batch * seq, -1),
                     w_gate, b_gate, w_pa, w_pf, w_out, ln_g, ln_b, alpha=alpha)


def kernel(x, ln1_g, ln1_b, ffn1_w_gate, ffn1_w_up, ffn1_w_down, w_in, b_in, rel_bias,
           w_proj_attn, w_proj_fourier, w_out, ln2_g, ln2_b, ffn2_w_gate, ffn2_w_up,
           ffn2_w_down, ln3_g, ln3_b):
    batch, seq, d = x.shape
    depth = ln1_g.shape[0]
    assert seq % ROW_TILE == 0 and d % MXU_DIM == 0
    assert w_in.shape[2] == GATE_COL0 + 2 * d
    assert rel_bias.shape == (NUM_BUCKETS, N_GROUPS * HEADS_PER_GROUP)
    alpha = (2 * depth) ** 0.25
    h = x.reshape(batch * seq, d)
    for l in range(depth):
        gates = ((GATE_COL0, w_in.shape[2] - GATE_COL0, 1.0),)
        jobs = (CastJob(w_in[l], (_in_proj_pieces(), gates)),
                _whole(w_proj_attn[l]), _whole(w_proj_fourier[l]), _whole(w_out[l]))
        h, (w_qkvu, w_gate, w_pa, w_pf, w_o) = _ffn_ln(
            h, ffn1_w_gate[l].T.astype(BF16), ffn1_w_up[l].T.astype(BF16),
            ffn1_w_down[l].astype(BF16), ln1_g[l], ln1_b[l], alpha=alpha, cast_jobs=jobs)
        h = _mixer_ln(h, w_qkvu, w_gate, b_in[l], rel_bias, w_pa, w_pf, w_o,
                      ln2_g[l], ln2_b[l], alpha=alpha, batch=batch, seq=seq)
        h, _ = _ffn_ln(h, ffn2_w_gate[l].T.astype(BF16), ffn2_w_up[l].T.astype(BF16),
                       ffn2_w_down[l].astype(BF16), ln3_g[l], ln3_b[l], alpha=alpha)
    return h.reshape(batch, seq, d)
```
